```python
import jax
import jax.numpy as jnp
from jax import lax
import numpy as np

D_MODEL = 1024
BATCH = 4
SEQ = 8192
DEPTH = 4

GRID_W = 64
CTX_LEN = 256

MLA_HEADS = 8
MLA_Q_LORA = 256
MLA_KV_LORA = 128
MLA_NOPE = 64
MLA_ROPE = 32
MLA_V = 64
ROPE_BASE = 10000.0
ATTN_QUERY_BLOCK = 128

FNET_GROUPS = 4
FNET_GROUP_DIM = 128
FNET_WIDTH = FNET_GROUPS * FNET_GROUP_DIM

NAT_HEADS = 8
NAT_HEAD_DIM = 64
NAT_WIDTH = NAT_HEADS * NAT_HEAD_DIM
NAT_WIN_H = 8
NAT_WIN_W = 16

N_BRANCHES = 3
IN_SIZES = (MLA_Q_LORA, MLA_KV_LORA, MLA_ROPE, FNET_WIDTH, NAT_WIDTH, NAT_WIDTH, NAT_WIDTH, N_BRANCHES * D_MODEL)
IN_DIM = sum(IN_SIZES)

N_GROUPS = 4
EXPERTS_PER_GROUP = 8
N_EXPERTS = N_GROUPS * EXPERTS_PER_GROUP
TOP_K = 2
D_EXPERT = 512
MOE_BLOCK = 128

LN_EPS = 1e-6

kernel_name = 'hybrid_mla_fnet_natten_hmoe_dit'


def layer_norm(x, g=None, b=None):
    xf = x.astype(jnp.float32)
    mu = jnp.mean(xf, axis=-1, keepdims=True)
    xc = xf - mu
    var = jnp.mean(xc * xc, axis=-1, keepdims=True)
    y = xc * lax.rsqrt(var + LN_EPS)
    if g is not None:
        y = y * g.astype(jnp.float32) + b.astype(jnp.float32)
    return y.astype(x.dtype)


def rms_norm(x, g):
    xf = x.astype(jnp.float32)
    y = xf * lax.rsqrt(jnp.mean(xf * xf, axis=-1, keepdims=True) + LN_EPS)
    return (y * g.astype(jnp.float32)).astype(x.dtype)


def modulate(x, shift, scale):
    return layer_norm(x) * (1 + scale) + shift


def split_cols(u):
    pts, acc = [], 0
    for sz in IN_SIZES[:-1]:
        acc += sz
        pts.append(acc)
    return jnp.split(u, pts, axis=-1)


def axial_rope_tables(n_tok):
    t = jnp.arange(n_tok)
    rows = (t // GRID_W).astype(jnp.float32)
    cols = (t % GRID_W).astype(jnp.float32)
    n_freq = MLA_ROPE // 4
    inv = jnp.power(ROPE_BASE, -jnp.arange(n_freq, dtype=jnp.float32) / n_freq)
    ar = rows[:, None] * inv[None, :]
    ac = cols[:, None] * inv[None, :]
    return (jnp.cos(ar), jnp.sin(ar), jnp.cos(ac), jnp.sin(ac))


def _rotate(z, cos, sin):
    n = z.shape[-1] // 2
    z1, z2 = z[..., :n], z[..., n:]
    return jnp.concatenate([z1 * cos - z2 * sin, z1 * sin + z2 * cos], axis=-1)


def apply_axial_rope(x, tabs):
    cos_r, sin_r, cos_c, sin_c = tabs
    half = MLA_ROPE // 2
    xf = x.astype(jnp.float32)
    y = jnp.concatenate([_rotate(xf[..., :half], cos_r, sin_r), _rotate(xf[..., half:], cos_c, sin_c)], axis=-1)
    return y.astype(x.dtype)


def dense_attention(q, k, v, scale):
    s = jnp.einsum('bqhd,bkhd->bhqk', q, k, preferred_element_type=jnp.float32) * scale
    p = jax.nn.softmax(s, axis=-1).astype(v.dtype)
    return jnp.einsum('bhqk,bkhd->bqhd', p, v)


def blocked_attention(q, k, v, scale):
    b, s, h, d = q.shape
    nb = s // ATTN_QUERY_BLOCK
    qb = q.reshape(b, nb, ATTN_QUERY_BLOCK, h, d).transpose(1, 0, 2, 3, 4)
    ob = lax.map(lambda qi: dense_attention(qi, k, v, scale), qb)
    return ob.transpose(1, 0, 2, 3, 4).reshape(b, s, h, v.shape[-1])


def mla_qkv(cq, ckv, kr, q_norm_g, kv_norm_g, w_uq, w_uk, w_uv, tabs):
    b, s, _ = cq.shape
    q = (rms_norm(cq, q_norm_g) @ w_uq).reshape(b, s, MLA_HEADS, MLA_NOPE + MLA_ROPE)
    q_nope, q_pe = q[..., :MLA_NOPE], q[..., MLA_NOPE:]
    ckv_n = rms_norm(ckv, kv_norm_g)
    k_nope = (ckv_n @ w_uk).reshape(b, s, MLA_HEADS, MLA_NOPE)
    v = (ckv_n @ w_uv).reshape(b, s, MLA_HEADS, MLA_V)
    if tabs is not None:
        q_pe = apply_axial_rope(q_pe, tuple(t[:, None, :] for t in tabs))
        kr = apply_axial_rope(kr, tabs)
    k_pe = jnp.broadcast_to(kr[:, :, None, :], (b, s, MLA_HEADS, MLA_ROPE))
    return (jnp.concatenate([q_nope, q_pe], axis=-1), jnp.concatenate([k_nope, k_pe], axis=-1), v)


def neighbourhood_attention(q, k, v, k_ctx, v_ctx, rpb):
    b, s, h, dh = q.shape
    rows_n = s // GRID_W
    kh = min(NAT_WIN_H, rows_n)
    scale = float(dh ** -0.5)
    qg = q.reshape(b, rows_n, GRID_W, h, dh)
    kg = k.reshape(b, rows_n, GRID_W, h, dh)
    vg = v.reshape(b, rows_n, GRID_W, h, dh)
    col = jnp.arange(GRID_W)
    col_start = jnp.clip(col - NAT_WIN_W // 2, 0, GRID_W - NAT_WIN_W)
    col_idx = col_start[:, None] + jnp.arange(NAT_WIN_W)
    dc_idx = col_idx - col[:, None] + NAT_WIN_W - 1
    n_loc = kh * NAT_WIN_W

    def one_row(r):
        rs = jnp.clip(r - kh // 2, 0, rows_n - kh)
        q_r = lax.dynamic_index_in_dim(qg, r, axis=1, keepdims=False)
        k_rows = lax.dynamic_slice_in_dim(kg, rs, kh, axis=1)
        v_rows = lax.dynamic_slice_in_dim(vg, rs, kh, axis=1)
        k_win = k_rows[:, :, col_idx]
        v_win = v_rows[:, :, col_idx]
        s_loc = jnp.einsum('bqhd,brqwhd->bhqrw', q_r, k_win, preferred_element_type=jnp.float32) * scale
        dr_idx = rs + jnp.arange(kh) - r + NAT_WIN_H - 1
        bias = rpb[:, dr_idx[None, :, None], dc_idx[:, None, :]]
        s_loc = s_loc + bias.astype(jnp.float32)[None]
        s_ctx = jnp.einsum('bqhd,bkhd->bhqk', q_r, k_ctx, preferred_element_type=jnp.float32) * scale
        sc = jnp.concatenate([s_loc.reshape(b, h, GRID_W, n_loc), s_ctx], axis=-1)
        p = jax.nn.softmax(sc, axis=-1).astype(v.dtype)
        p_loc = p[..., :n_loc].reshape(b, h, GRID_W, kh, NAT_WIN_W)
        p_ctx = p[..., n_loc:]
        return (jnp.einsum('bhqrw,brqwhd->bqhd', p_loc, v_win) + jnp.einsum('bhqk,bkhd->bqhd', p_ctx, v_ctx))

    out = lax.map(one_row, jnp.arange(rows_n))
    return out.transpose(1, 0, 2, 3, 4).reshape(b, s, h * dh)


def fourier_mix(f):
    b, s, _ = f.shape
    g = f.reshape(b, s, FNET_GROUPS, FNET_GROUP_DIM).astype(jnp.float32)
    y = jnp.real(jnp.fft.fftn(g, axes=(1, 3), norm='ortho'))
    return y.reshape(b, s, FNET_WIDTH).astype(f.dtype)


def merge_branches(o_mla, o_fnet, o_nat, g_lin, w_o_mla, w_o_fnet, w_o_nat, w_out, b_out):
    g = jax.nn.sigmoid(g_lin.astype(jnp.float32)).astype(o_mla.dtype)
    g_mla, g_fnet, g_nat = jnp.split(g, N_BRANCHES, axis=-1)
    m = g_mla * (o_mla @ w_o_mla) + g_fnet * (o_fnet @ w_o_fnet) + g_nat * (o_nat @ w_o_nat)
    return m @ w_out + b_out


def token_mixer(h_lat, h_ctx, w_in, b_in, q_norm_g, kv_norm_g, w_uq, w_uk, w_uv, rpb, w_o_mla, w_o_fnet, w_o_nat, w_out, b_out, tabs, with_ctx):
    b, s, _ = h_lat.shape
    n_ctx = h_ctx.shape[1]
    cq_l, ckv_l, kr_l, f_l, nq_l, nk_l, nv_l, g_l = split_cols(h_lat @ w_in + b_in)
    cq_c, ckv_c, kr_c, f_c, nq_c, nk_c, nv_c, g_c = split_cols(h_ctx @ w_in + b_in)
    q_l, k_l, v_l = mla_qkv(cq_l, ckv_l, kr_l, q_norm_g, kv_norm_g, w_uq, w_uk, w_uv, tabs)
    q_c, k_c, v_c = mla_qkv(cq_c, ckv_c, kr_c, q_norm_g, kv_norm_g, w_uq, w_uk, w_uv, None)
    mla_scale = float((MLA_NOPE + MLA_ROPE) ** -0.5)
    o_mla_l = blocked_attention(q_l, jnp.concatenate([k_c, k_l], axis=1), jnp.concatenate([v_c, v_l], axis=1), mla_scale).reshape(b, s, MLA_HEADS * MLA_V)
    def nat_heads(z):
        return z.reshape(z.shape[0], z.shape[1], NAT_HEADS, NAT_HEAD_DIM)
    nk_ch, nv_ch = nat_heads(nk_c), nat_heads(nv_c)
    o_nat_l = neighbourhood_attention(nat_heads(nq_l), nat_heads(nk_l), nat_heads(nv_l), nk_ch, nv_ch, rpb)
    o_f_l = fourier_mix(f_l)
    y_lat = merge_branches(o_mla_l, o_f_l, o_nat_l, g_l, w_o_mla, w_o_fnet, w_o_nat, w_out, b_out)
    if not with_ctx:
        return y_lat, None
    o_mla_c = dense_attention(q_c, k_c, v_c, mla_scale).reshape(b, n_ctx, MLA_HEADS * MLA_V)
    o_nat_c = dense_attention(nat_heads(nq_c), nk_ch, nv_ch, float(NAT_HEAD_DIM ** -0.5)).reshape(b, n_ctx, NAT_WIDTH)
    o_f_c = fourier_mix(f_c)
    y_ctx = merge_branches(o_mla_c, o_f_c, o_nat_c, g_c, w_o_mla, w_o_fnet, w_o_nat, w_out, b_out)
    return y_lat, y_ctx


def hier_moe(h, w_rg, b_rg, w_re, b_re, w_gate, w_up, w_down):
    t, d = h.shape
    p_grp = jax.nn.softmax((h @ w_rg + b_rg).astype(jnp.float32), axis=-1)
    p_top, g_idx = lax.top_k(p_grp, 1)
    e_logits = (h @ w_re + b_re).astype(jnp.float32).reshape(t, N_GROUPS, EXPERTS_PER_GROUP)
    e_in = jnp.take_along_axis(e_logits, g_idx[:, :, None], axis=1)[:, 0]
    l_top, e_loc = lax.top_k(e_in, TOP_K)
    w_tok = (p_top * jax.nn.softmax(l_top, axis=-1)).astype(h.dtype)
    e_ids = g_idx * EXPERTS_PER_GROUP + e_loc
    a = t * TOP_K
    flat_e = e_ids.reshape(a)
    flat_w = w_tok.reshape(a)
    flat_tok = jnp.repeat(jnp.arange(t, dtype=jnp.int32), TOP_K)
    order = jnp.argsort(flat_e)
    e_sorted = flat_e[order]
    counts = jnp.bincount(flat_e, length=N_EXPERTS)
    starts = jnp.cumsum(counts) - counts
    padded = (counts + MOE_BLOCK - 1) // MOE_BLOCK * MOE_BLOCK
    pends = jnp.cumsum(padded)
    pstarts = pends - padded
    dest = pstarts[e_sorted] + jnp.arange(a, dtype=jnp.int32) - starts[e_sorted]
    n_blocks = (a + MOE_BLOCK - 1) // MOE_BLOCK + N_EXPERTS
    n_rows = n_blocks * MOE_BLOCK
    row_tok = jnp.full((n_rows,), t, jnp.int32).at[dest].set(flat_tok[order])
    row_w = jnp.zeros((n_rows,), h.dtype).at[dest].set(flat_w[order])
    block_e = jnp.minimum(jnp.searchsorted(pends, jnp.arange(n_blocks, dtype=jnp.int32) * MOE_BLOCK, side='right'), N_EXPERTS - 1)
    h_pad = jnp.concatenate([h, jnp.zeros((1, d), h.dtype)], axis=0)
    xb = h_pad[row_tok].reshape(n_blocks, MOE_BLOCK, d)

    def expert_block(args):
        xblk, e = args
        return (jax.nn.silu(xblk @ w_gate[e]) * (xblk @ w_up[e])) @ w_down[e]

    yb = lax.map(expert_block, (xb, block_e)).reshape(n_rows, d)
    out = jnp.zeros((t + 1, d), yb.dtype).at[row_tok].add(yb * row_w[:, None])
    return out[:t]


def setup_inputs(seed: int = 0) -> dict:
    key = jax.random.key(seed)
    ks = jax.random.split(key, 30)
    L, D = DEPTH, D_MODEL
    beta = float((8 * DEPTH) ** -0.25)

    def nrm(k, shape, scale):
        return scale * jax.random.normal(k, shape, jnp.float32)

    return {
        'x': nrm(ks[0], (BATCH, SEQ, D), 1.0),
        'c': nrm(ks[1], (BATCH, D), 1.0),
        'ctx': nrm(ks[2], (BATCH, CTX_LEN, D), 1.0),
        'c_ctx': nrm(ks[3], (D,), 1.0),
        'w_ada': nrm(ks[4], (L, D, 6 * D), 0.5 * D ** -0.5),
        'b_ada': nrm(ks[5], (L, 6 * D), 0.02),
        'w_in': nrm(ks[6], (L, D, IN_DIM), D ** -0.5),
        'b_in': nrm(ks[7], (L, IN_DIM), 0.02),
        'q_norm_g': 1.0 + nrm(ks[8], (L, MLA_Q_LORA), 0.05),
        'kv_norm_g': 1.0 + nrm(ks[9], (L, MLA_KV_LORA), 0.05),
        'w_uq': nrm(ks[10], (L, MLA_Q_LORA, MLA_HEADS * (MLA_NOPE + MLA_ROPE)), MLA_Q_LORA ** -0.5),
        'w_uk': nrm(ks[11], (L, MLA_KV_LORA, MLA_HEADS * MLA_NOPE), MLA_KV_LORA ** -0.5),
        'w_uv': nrm(ks[12], (L, MLA_KV_LORA, MLA_HEADS * MLA_V), MLA_KV_LORA ** -0.5),
        'rpb': nrm(ks[13], (L, NAT_HEADS, 2 * NAT_WIN_H - 1, 2 * NAT_WIN_W - 1), 0.5),
        'w_o_mla': nrm(ks[14], (L, MLA_HEADS * MLA_V, D), (MLA_HEADS * MLA_V) ** -0.5),
        'w_o_fnet': nrm(ks[15], (L, FNET_WIDTH, D), FNET_WIDTH ** -0.5),
        'w_o_nat': nrm(ks[16], (L, NAT_WIDTH, D), NAT_WIDTH ** -0.5),
        'w_out': nrm(ks[17], (L, D, D), beta * D ** -0.5),
        'b_out': nrm(ks[18], (L, D), 0.02),
        'ln1_g': 1.0 + nrm(ks[19], (L, D), 0.05),
        'ln1_b': nrm(ks[20], (L, D), 0.02),
        'w_rg': nrm(ks[21], (L, D, N_GROUPS), D ** -0.5),
        'b_rg': nrm(ks[22], (L, N_GROUPS), 0.01),
        'w_re': nrm(ks[23], (L, D, N_EXPERTS), D ** -0.5),
        'b_re': nrm(ks[24], (L, N_EXPERTS), 0.01),
        'w_gate_e': nrm(ks[25], (L, N_EXPERTS, D, D_EXPERT), D ** -0.5),
        'w_up_e': nrm(ks[26], (L, N_EXPERTS, D, D_EXPERT), D ** -0.5),
        'w_down_e': nrm(ks[27], (L, N_EXPERTS, D_EXPERT, D), beta * D_EXPERT ** -0.5),
        'ln2_g': 1.0 + nrm(ks[28], (L, D), 0.05),
        'ln2_b': nrm(ks[29], (L, D), 0.02),
    }


def reference(x, c, ctx, c_ctx, w_ada, b_ada, w_in, b_in, q_norm_g, kv_norm_g, w_uq, w_uk, w_uv, rpb, w_o_mla, w_o_fnet, w_o_nat, w_out, b_out, ln1_g, ln1_b, w_rg, b_rg, w_re, b_re, w_gate_e, w_up_e, w_down_e, ln2_g, ln2_b):
    b, s, d = x.shape
    n_ctx = ctx.shape[1]
    alpha = float((2 * DEPTH) ** 0.25)
    tabs = axial_rope_tables(s)
    xc = ctx
    c_act = jax.nn.silu(c)
    c_ctx_act = jax.nn.silu(c_ctx)
    for l in range(DEPTH):
        with_ctx = l < DEPTH - 1
        mod = (c_act @ w_ada[l] + b_ada[l])[:, None, :]
        mod_c = (c_ctx_act @ w_ada[l] + b_ada[l])[None, None, :]
        sh_a, sc_a, g_a, sh_f, sc_f, g_f = jnp.split(mod, 6, axis=-1)
        shc_a, scc_a, gc_a, shc_f, scc_f, gc_f = jnp.split(mod_c, 6, axis=-1)
        y_lat, y_ctx = token_mixer(modulate(x, sh_a, sc_a), modulate(xc, shc_a, scc_a), w_in[l], b_in[l], q_norm_g[l], kv_norm_g[l], w_uq[l], w_uk[l], w_uv[l], rpb[l], w_o_mla[l], w_o_fnet[l], w_o_nat[l], w_out[l], b_out[l], tabs, with_ctx)
        x = layer_norm(alpha * x + g_a * y_lat, ln1_g[l], ln1_b[l])
        h_f = modulate(x, sh_f, sc_f).reshape(b * s, d)
        if with_ctx:
            xc = layer_norm(alpha * xc + gc_a * y_ctx, ln1_g[l], ln1_b[l])
            h_fc = modulate(xc, shc_f, scc_f).reshape(b * n_ctx, d)
            f_all = hier_moe(jnp.concatenate([h_fc, h_f], axis=0), w_rg[l], b_rg[l], w_re[l], b_re[l], w_gate_e[l], w_up_e[l], w_down_e[l])
            f_ctx = f_all[:b * n_ctx].reshape(b, n_ctx, d)
            f_lat = f_all[b * n_ctx:].reshape(b, s, d)
            xc = layer_norm(alpha * xc + gc_f * f_ctx, ln2_g[l], ln2_b[l])
        else:
            f_lat = hier_moe(h_f, w_rg[l], b_rg[l], w_re[l], b_re[l], w_gate_e[l], w_up_e[l], w_down_e[l]).reshape(b, s, d)
        x = layer_norm(alpha * x + g_f * f_lat, ln2_g[l], ln2_b[l])
    return x
```

```python
import functools
import math

import numpy as np
import jax
import jax.numpy as jnp
from jax import lax
from jax.experimental import pallas as pl
from jax.experimental.pallas import tpu as pltpu

F32 = jnp.float32
BF16 = jnp.bfloat16

GRID_W = 64
MLA_HEADS = 8
MLA_Q_LORA = 256
MLA_KV_LORA = 128
MLA_NOPE = 64
MLA_ROPE = 32
MLA_V = 64
ROPE_BASE = 10000.0
FNET_GROUPS = 4
FNET_GROUP_DIM = 128
FNET_WIDTH = FNET_GROUPS * FNET_GROUP_DIM
NAT_HEADS = 8
NAT_HEAD_DIM = 64
NAT_WIDTH = NAT_HEADS * NAT_HEAD_DIM
NAT_WIN_H = 8
NAT_WIN_W = 16
N_GROUPS = 4
EXPERTS_PER_GROUP = 8
N_EXPERTS = N_GROUPS * EXPERTS_PER_GROUP
D_EXPERT = 512
LN_EPS = 1e-6

LANES = 128
HEAD_PAD = 128
LOG2E = 1.4426950408889634
NEG_BIG = -1e30
VMEM_LIMIT = 56 * 1024 * 1024

U_F, U_NQ, U_NK, U_NV, U_G, U_CQ, U_CKV, U_KR, U_W = 0, 512, 1024, 1536, 2048, 5120, 5376, 5504, 5632

TM = 512
NAT_ROWS = 4
NAT_KROWS = 12
MOE_BM = 256


def _cparams(sem):
    return pltpu.CompilerParams(dimension_semantics=sem, vmem_limit_bytes=VMEM_LIMIT)


def _ln(x):
    mu = jnp.mean(x, axis=-1, keepdims=True)
    xc = x - mu
    var = jnp.mean(xc * xc, axis=-1, keepdims=True)
    return xc * lax.rsqrt(var + LN_EPS)


def _ada_kernel(c_ref, w_ref, b_ref, o_ref):
    c = c_ref[...]
    a = c * (1.0 / (1.0 + jnp.exp(-c)))
    o_ref[...] = jnp.dot(a, w_ref[...], preferred_element_type=F32,
                         precision=lax.Precision.HIGHEST) + b_ref[...]


def _ada_all(cc, w_ada, b_ada):
    L, D, N = w_ada.shape
    tn = 1536
    return pl.pallas_call(
        _ada_kernel,
        grid=(L, N // tn),
        in_specs=[pl.BlockSpec((8, D), lambda l, j: (0, 0)),
                  pl.BlockSpec((None, D, tn), lambda l, j: (l, 0, j)),
                  pl.BlockSpec((None, 1, tn), lambda l, j: (l, 0, j))],
        out_specs=pl.BlockSpec((None, 8, tn), lambda l, j: (l, 0, j)),
        out_shape=jax.ShapeDtypeStruct((L, 8, N), F32),
        compiler_params=_cparams(("arbitrary", "arbitrary")),
        name="ada_mod",
    )(cc, w_ada, b_ada.reshape(L, 1, N))


def _in_proj_kernel(x_ref, mod_ref, w_ref, b_ref, o_ref, *, cw):
    y = _ln(x_ref[...])
    h = (y * (1.0 + mod_ref[1:2, :]) + mod_ref[0:1, :]).astype(BF16)
    for j in range(o_ref.shape[1] // cw):
        sl = slice(j * cw, (j + 1) * cw)
        acc = jnp.dot(h, w_ref[:, sl], preferred_element_type=F32)
        o_ref[:, sl] = (acc + b_ref[:, sl]).astype(o_ref.dtype)


def _in_proj(xall, mod, w_in_r, b_in_r, mod_row):
    TA, D = xall.shape
    N = w_in_r.shape[1]
    return pl.pallas_call(
        functools.partial(_in_proj_kernel, cw=512),
        grid=(TA // TM,),
        in_specs=[pl.BlockSpec((TM, D), lambda i: (i, 0)),
                  pl.BlockSpec((None, 8, D), lambda i: (mod_row(i), 0, 0)),
                  pl.BlockSpec((D, N), lambda i: (0, 0)),
                  pl.BlockSpec((1, N), lambda i: (0, 0))],
        out_specs=pl.BlockSpec((TM, N), lambda i: (i, 0)),
        out_shape=jax.ShapeDtypeStruct((TA, N), BF16),
        compiler_params=_cparams(("arbitrary",)),
        name="in_proj",
    )(xall, mod, w_in_r, b_in_r)


def _rms(x, g):
    return x * lax.rsqrt(jnp.mean(x * x, axis=-1, keepdims=True) + LN_EPS) * g


def _qkv_kernel(cq_ref, ckv_ref, kr_ref, tc_ref, ts_ref, qg_ref, kvg_ref, wqa_ref, wqb_ref,
                wk_ref, wv_ref, pa_ref, pb_ref, vone_ref, q_ref, k_ref, v_ref, *, qscale):
    cqn = _rms(cq_ref[...].astype(F32), qg_ref[...]).astype(BF16)
    ckvn = _rms(ckv_ref[...].astype(F32), kvg_ref[...]).astype(BF16)
    cos = jnp.concatenate([tc_ref[...]] * MLA_HEADS, axis=1)
    sin = jnp.concatenate([ts_ref[...]] * MLA_HEADS, axis=1)
    qa = jnp.dot(cqn, wqa_ref[...], preferred_element_type=F32)
    qb = jnp.dot(cqn, wqb_ref[...], preferred_element_type=F32)
    q_ref[...] = ((qa * cos + qb * sin) * qscale).astype(q_ref.dtype)
    kr = kr_ref[...]
    ka = jnp.dot(kr, pa_ref[...], preferred_element_type=F32)
    kb = jnp.dot(kr, pb_ref[...], preferred_element_type=F32)
    kn = jnp.dot(ckvn, wk_ref[...], preferred_element_type=F32)
    k_ref[...] = (kn + ka * cos + kb * sin).astype(k_ref.dtype)
    v = jnp.dot(ckvn, wv_ref[...], preferred_element_type=F32) + vone_ref[...]
    v_ref[...] = v.astype(v_ref.dtype)


def _qkv(u, tabc, tabs, tab_row, qg, kvg, wqa, wqb, wk, wv, pa, pb, vone, qscale):
    TA = u.shape[0]
    W = MLA_HEADS * HEAD_PAD
    const = lambda i: (0, 0)
    out = jax.ShapeDtypeStruct((TA, W), BF16)
    return pl.pallas_call(
        functools.partial(_qkv_kernel, qscale=qscale),
        grid=(TA // TM,),
        in_specs=[pl.BlockSpec((TM, MLA_Q_LORA), lambda i: (i, U_CQ // MLA_Q_LORA)),
                  pl.BlockSpec((TM, MLA_KV_LORA), lambda i: (i, U_CKV // MLA_KV_LORA)),
                  pl.BlockSpec((TM, LANES), lambda i: (i, U_KR // LANES)),
                  pl.BlockSpec((TM, LANES), lambda i: (tab_row(i), 0)),
                  pl.BlockSpec((TM, LANES), lambda i: (tab_row(i), 0)),
                  pl.BlockSpec((1, MLA_Q_LORA), const),
                  pl.BlockSpec((1, MLA_KV_LORA), const),
                  pl.BlockSpec((MLA_Q_LORA, W), const),
                  pl.BlockSpec((MLA_Q_LORA, W), const),
                  pl.BlockSpec((MLA_KV_LORA, W), const),
                  pl.BlockSpec((MLA_KV_LORA, W), const),
                  pl.BlockSpec((LANES, W), const),
                  pl.BlockSpec((LANES, W), const),
                  pl.BlockSpec((1, W), const)],
        out_specs=[pl.BlockSpec((TM, W), lambda i: (i, 0))] * 3,
        out_shape=[out, out, out],
        compiler_params=_cparams(("arbitrary",)),
        name="mla_qkv",
    )(u, u, u, tabc, tabs, qg, kvg, wqa, wqb, wk, wv, pa, pb, vone)


def _mla_attend(q_ref, k_ref, v_ref, acc_ref, m_ref):
    tk = k_ref.shape[0]
    for h in range(MLA_HEADS):
        hs = slice(h * HEAD_PAD, (h + 1) * HEAD_PAD)
        s = lax.dot_general(q_ref[:, hs], k_ref[:, hs], (((1,), (1,)), ((), ())),
                            preferred_element_type=F32)
        m_prev = m_ref[h]
        m_new = jnp.maximum(m_prev, jnp.max(s, axis=1, keepdims=True))
        alpha = jnp.exp2(m_prev - m_new)
        p = jnp.exp2(s - jnp.concatenate([m_new] * (tk // LANES), axis=1))
        pv = jnp.dot(p.astype(BF16), v_ref[:, hs], preferred_element_type=F32)
        acc_ref[h] = acc_ref[h] * alpha + pv
        m_ref[h] = m_new


def _mla_finish(o_ref, acc_ref):
    lane = lax.broadcasted_iota(jnp.int32, acc_ref.shape[1:], 1)
    for hp in range(MLA_HEADS // 2):
        a0 = acc_ref[2 * hp]
        a1 = acc_ref[2 * hp + 1]
        l0 = jnp.sum(jnp.where(lane == MLA_V, a0, 0.0), axis=1, keepdims=True)
        l1 = jnp.sum(jnp.where(lane == 0, a1, 0.0), axis=1, keepdims=True)
        o = jnp.where(lane < MLA_V, a0 / l0, a1 / l1)
        o_ref[:, hp * LANES:(hp + 1) * LANES] = o.astype(o_ref.dtype)


def _mla_lat_kernel(q_ref, kc_ref, vc_ref, kl_ref, vl_ref, o_ref, acc_ref, m_ref):
    j = pl.program_id(2)

    @pl.when(j == 0)
    def _():
        m_ref[...] = jnp.full(m_ref.shape, NEG_BIG, F32)
        acc_ref[...] = jnp.zeros(acc_ref.shape, F32)
        _mla_attend(q_ref, kc_ref, vc_ref, acc_ref, m_ref)

    _mla_attend(q_ref, kl_ref, vl_ref, acc_ref, m_ref)

    @pl.when(j == pl.num_programs(2) - 1)
    def _():
        _mla_finish(o_ref, acc_ref)


def _mla_ctx_kernel(q_ref, kc_ref, vc_ref, o_ref, acc_ref, m_ref):
    m_ref[...] = jnp.full(m_ref.shape, NEG_BIG, F32)
    acc_ref[...] = jnp.zeros(acc_ref.shape, F32)
    _mla_attend(q_ref, kc_ref, vc_ref, acc_ref, m_ref)
    _mla_finish(o_ref, acc_ref)


def _mla_attention(q, k, v, B, S, NC, tq, tk):
    TA, W = q.shape
    TC = B * NC
    OW = MLA_HEADS * MLA_V
    nq, nk = S // tq, S // tk
    scratch = lambda t: [pltpu.VMEM((MLA_HEADS, t, HEAD_PAD), F32), pltpu.VMEM((MLA_HEADS, t, HEAD_PAD), F32)]
    o_lat = pl.pallas_call(
        _mla_lat_kernel,
        grid=(B, nq, nk),
        in_specs=[pl.BlockSpec((tq, W), lambda b, i, j: (TC // tq + b * nq + i, 0)),
                  pl.BlockSpec((NC, W), lambda b, i, j: (b, 0)),
                  pl.BlockSpec((NC, W), lambda b, i, j: (b, 0)),
                  pl.BlockSpec((tk, W), lambda b, i, j: (TC // tk + b * nk + j, 0)),
                  pl.BlockSpec((tk, W), lambda b, i, j: (TC // tk + b * nk + j, 0))],
        out_specs=pl.BlockSpec((tq, OW), lambda b, i, j: (b * nq + i, 0)),
        out_shape=jax.ShapeDtypeStruct((B * S, OW), BF16),
        scratch_shapes=scratch(tq),
        compiler_params=_cparams(("arbitrary", "arbitrary", "arbitrary")),
        name="mla_attn_lat",
    )(q, k, v, k, v)
    o_ctx = pl.pallas_call(
        _mla_ctx_kernel,
        grid=(B,),
        in_specs=[pl.BlockSpec((NC, W), lambda b: (b, 0))] * 3,
        out_specs=pl.BlockSpec((NC, OW), lambda b: (b, 0)),
        out_shape=jax.ShapeDtypeStruct((TC, OW), BF16),
        scratch_shapes=scratch(NC),
        compiler_params=_cparams(("arbitrary",)),
        name="mla_attn_ctx",
    )(q, k, v)
    return o_ctx, o_lat


def _nat_softmax_pv(scores, values):
    m = scores[0].max(axis=1, keepdims=True)
    for s in scores[1:]:
        m = jnp.maximum(m, s.max(axis=1, keepdims=True))
    l = None
    o = None
    for s, v in zip(scores, values):
        p = jnp.exp2(s - m)
        ls = jnp.sum(p, axis=1, keepdims=True)
        pv = jnp.dot(p.astype(BF16), v, preferred_element_type=F32)
        l = ls if l is None else l + ls
        o = pv if o is None else o + pv
    return o / l


def _nat_lat_kernel(q_ref, k0_ref, k1_ref, k2_ref, v0_ref, v1_ref, v2_ref, kc_ref, vc_ref, bias_ref, o_ref,
                    *, qscale):
    tq = q_ref.shape[0]
    ck = k0_ref.shape[0]
    lane = lax.broadcasted_iota(jnp.int32, (tq, LANES), 1)
    dn = (((1,), (1,)), ((), ()))
    for hp in range(NAT_HEADS // 2):
        ls = slice(hp * LANES, (hp + 1) * LANES)
        qp = q_ref[:, ls].astype(F32) * qscale
        keys = [r[:, ls] for r in (k0_ref, k1_ref, k2_ref, kc_ref)]
        vals = [r[:, ls] for r in (v0_ref, v1_ref, v2_ref, vc_ref)]
        outs = []
        for hh in range(2):
            mask = (lane < NAT_HEAD_DIM) if hh == 0 else (lane >= NAT_HEAD_DIM)
            qm = jnp.where(mask, qp, 0.0).astype(BF16)
            scores = []
            for c, kk in enumerate(keys):
                s = lax.dot_general(qm, kk, dn, preferred_element_type=F32)
                if c < 3:
                    s = s + bias_ref[2 * hp + hh, :, c * ck:(c + 1) * ck]
                scores.append(s)
            outs.append(_nat_softmax_pv(scores, vals))
        o_ref[:, ls] = jnp.where(lane < NAT_HEAD_DIM, outs[0], outs[1]).astype(o_ref.dtype)


def _nat_ctx_kernel(q_ref, kc_ref, vc_ref, o_ref, *, qscale):
    tq = q_ref.shape[0]
    lane = lax.broadcasted_iota(jnp.int32, (tq, LANES), 1)
    dn = (((1,), (1,)), ((), ()))
    for hp in range(NAT_HEADS // 2):
        ls = slice(hp * LANES, (hp + 1) * LANES)
        qp = q_ref[:, ls].astype(F32) * qscale
        outs = []
        for hh in range(2):
            mask = (lane < NAT_HEAD_DIM) if hh == 0 else (lane >= NAT_HEAD_DIM)
            qm = jnp.where(mask, qp, 0.0).astype(BF16)
            s = lax.dot_general(qm, kc_ref[:, ls], dn, preferred_element_type=F32)
            outs.append(_nat_softmax_pv([s], [vc_ref[:, ls]]))
        o_ref[:, ls] = jnp.where(lane < NAT_HEAD_DIM, outs[0], outs[1]).astype(o_ref.dtype)


def _nat_bias_table(rpb, rows_n):
    qc = np.arange(GRID_W)
    cs = np.clip(qc - NAT_WIN_W // 2, 0, GRID_W - NAT_WIN_W)
    kc = np.arange(GRID_W)
    colvalid = (kc[None, :] >= cs[:, None]) & (kc[None, :] < cs[:, None] + NAT_WIN_W)
    dc = np.clip(kc[None, :] - qc[:, None] + NAT_WIN_W - 1, 0, 2 * NAT_WIN_W - 2)
    bt = jnp.where(colvalid[None, None], rpb[:, :, dc] * LOG2E, NEG_BIG)
    kh = NAT_WIN_H
    dr = np.zeros((3, NAT_ROWS, NAT_KROWS), np.int32)
    valid = np.zeros((3, NAT_ROWS, NAT_KROWS), bool)
    for v, r0 in enumerate((0, 2 * NAT_ROWS, rows_n - NAT_ROWS)):
        kstart = int(np.clip(r0 - kh // 2, 0, rows_n - NAT_KROWS))
        for i in range(NAT_ROWS):
            r = r0 + i
            rs = int(np.clip(r - kh // 2, 0, rows_n - kh))
            for jj in range(NAT_KROWS):
                kr = kstart + jj
                valid[v, i, jj] = rs <= kr < rs + kh
                dr[v, i, jj] = np.clip(kr - r + NAT_WIN_H - 1, 0, 2 * NAT_WIN_H - 2)
    t = bt[:, dr]
    t = jnp.where(valid[None, :, :, :, None, None], t, NEG_BIG)
    t = t.transpose(1, 0, 2, 4, 3, 5)
    return t.reshape(3, NAT_HEADS, NAT_ROWS * GRID_W, NAT_KROWS * GRID_W)


def _nat_attention(u, bias, B, S, NC):
    TA = u.shape[0]
    TC = B * NC
    rows_n = S // GRID_W
    nblk = rows_n // NAT_ROWS
    tq = NAT_ROWS * GRID_W
    ck = tq
    qscale = float(NAT_HEAD_DIM ** -0.5) * LOG2E
    cq, ckk, cv = U_NQ // NAT_WIDTH, U_NK // NAT_WIDTH, U_NV // NAT_WIDTH
    base = lambda b: TC // tq + b * nblk

    def kstart_blk(i):
        return jnp.clip(i * NAT_ROWS - NAT_WIN_H // 2, 0, rows_n - NAT_KROWS) // NAT_ROWS

    def kv_spec(col, c):
        return pl.BlockSpec((ck, NAT_WIDTH), lambda b, i: (base(b) + kstart_blk(i) + c, col))

    def variant(i):
        return jnp.where(i == 0, 0, jnp.where(i == nblk - 1, 2, 1))

    o_lat = pl.pallas_call(
        functools.partial(_nat_lat_kernel, qscale=qscale),
        grid=(B, nblk),
        in_specs=[pl.BlockSpec((tq, NAT_WIDTH), lambda b, i: (base(b) + i, cq)),
                  kv_spec(ckk, 0), kv_spec(ckk, 1), kv_spec(ckk, 2),
                  kv_spec(cv, 0), kv_spec(cv, 1), kv_spec(cv, 2),
                  pl.BlockSpec((NC, NAT_WIDTH), lambda b, i: (b, ckk)),
                  pl.BlockSpec((NC, NAT_WIDTH), lambda b, i: (b, cv)),
                  pl.BlockSpec((None, NAT_HEADS, tq, NAT_KROWS * GRID_W), lambda b, i: (variant(i), 0, 0, 0))],
        out_specs=pl.BlockSpec((tq, NAT_WIDTH), lambda b, i: (b * nblk + i, 0)),
        out_shape=jax.ShapeDtypeStruct((B * S, NAT_WIDTH), BF16),
        compiler_params=_cparams(("arbitrary", "arbitrary")),
        name="nat_attn_lat",
    )(u, u, u, u, u, u, u, u, u, bias)
    o_ctx = pl.pallas_call(
        functools.partial(_nat_ctx_kernel, qscale=qscale),
        grid=(B,),
        in_specs=[pl.BlockSpec((NC, NAT_WIDTH), lambda b: (b, cq)),
                  pl.BlockSpec((NC, NAT_WIDTH), lambda b: (b, ckk)),
                  pl.BlockSpec((NC, NAT_WIDTH), lambda b: (b, cv))],
        out_specs=pl.BlockSpec((NC, NAT_WIDTH), lambda b: (b, 0)),
        out_shape=jax.ShapeDtypeStruct((TC, NAT_WIDTH), BF16),
        compiler_params=_cparams(("arbitrary",)),
        name="nat_attn_ctx",
    )(u, u, u)
    return o_ctx, o_lat


def _fnet_chan_kernel(f_ref, w_ref, o_ref):
    for g in range(FNET_GROUPS):
        gs = slice(g * FNET_GROUP_DIM, (g + 1) * FNET_GROUP_DIM)
        o_ref[:, gs] = jnp.dot(f_ref[:, gs], w_ref[...], preferred_element_type=F32).astype(o_ref.dtype)


def _fnet_seq_kernel(a_ref, r_ref, o_ref, acc_ref):
    k = pl.program_id(2)

    @pl.when(k == 0)
    def _():
        acc_ref[...] = jnp.zeros(acc_ref.shape, F32)

    acc_ref[...] += jnp.dot(a_ref[...], r_ref[...], preferred_element_type=F32)

    @pl.when(k == pl.num_programs(2) - 1)
    def _():
        o_ref[...] = acc_ref[...].astype(o_ref.dtype)


def _dft_tables(n, scale):
    j = jnp.arange(n, dtype=jnp.int32)
    idx = (j[:, None] * j[None, :]) & (n - 1)
    ang = idx.astype(F32) * (2.0 * math.pi / n)
    return jnp.cos(ang) * scale, jnp.sin(ang) * scale


def _seq_dft_matrix(n):
    scale = n ** -0.5
    if n <= 512:
        c, s = _dft_tables(n, scale)
        return jnp.concatenate([c, -s], axis=1).astype(BF16)
    lo = 128
    hi = n // lo
    k = jnp.arange(n, dtype=jnp.int32)
    a_hi = ((k[:, None] * (jnp.arange(hi, dtype=jnp.int32) * lo)[None, :]) & (n - 1)).astype(F32) * (2.0 * math.pi / n)
    a_lo = ((k[:, None] * jnp.arange(lo, dtype=jnp.int32)[None, :]) & (n - 1)).astype(F32) * (2.0 * math.pi / n)
    ch, sh = jnp.cos(a_hi)[:, :, None], jnp.sin(a_hi)[:, :, None]
    cl, sl = jnp.cos(a_lo)[:, None, :] * scale, jnp.sin(a_lo)[:, None, :] * scale
    c = (ch * cl - sh * sl).reshape(n, n)
    s = (sh * cl + ch * sl).reshape(n, n)
    return jnp.concatenate([c.astype(BF16), (-s).astype(BF16)], axis=1)


def _fourier_mix(u, row0, B, n, a_mat, w_chan, ts, tm, tk):
    nt = n // ts
    rr = pl.pallas_call(
        _fnet_chan_kernel,
        grid=(B, nt, 2),
        in_specs=[pl.BlockSpec((ts, FNET_WIDTH), lambda b, i, p: (row0 // ts + b * nt + i, U_F // FNET_WIDTH)),
                  pl.BlockSpec((FNET_GROUP_DIM, FNET_GROUP_DIM), lambda b, i, p: (0, p))],
        out_specs=pl.BlockSpec((ts, FNET_WIDTH), lambda b, i, p: (p * nt + i, b)),
        out_shape=jax.ShapeDtypeStruct((2 * n, B * FNET_WIDTH), BF16),
        compiler_params=_cparams(("arbitrary", "arbitrary", "arbitrary")),
        name="fnet_chan",
    )(u, w_chan)
    tn = min(B * FNET_WIDTH, 2048)
    return pl.pallas_call(
        _fnet_seq_kernel,
        grid=(n // tm, (B * FNET_WIDTH) // tn, (2 * n) // tk),
        in_specs=[pl.BlockSpec((tm, tk), lambda i, j, k: (i, k)),
                  pl.BlockSpec((tk, tn), lambda i, j, k: (k, j))],
        out_specs=pl.BlockSpec((tm, tn), lambda i, j, k: (i, j)),
        out_shape=jax.ShapeDtypeStruct((n, B * FNET_WIDTH), BF16),
        scratch_shapes=[pltpu.VMEM((tm, tn), F32)],
        compiler_params=_cparams(("arbitrary", "arbitrary", "arbitrary")),
        name="fnet_seq",
    )(a_mat, rr)


def _merge_kernel(om_ref, ofl_ref, ofc_ref, on_ref, gm_ref, gf_ref, gn_ref, x_ref, mod_ref, wm_ref, wf_ref, wn_ref,
                  wo_ref, bo_ref, lg_ref, lb_ref, wr_ref, br_ref, xo_ref, h_ref, lo_ref, *, alpha, nct):
    def gate(g_ref):
        return 1.0 / (1.0 + jnp.exp(-g_ref[...].astype(F32)))

    of = jnp.where(pl.program_id(0) < nct, ofc_ref[...], ofl_ref[...])
    m = gate(gm_ref) * jnp.dot(om_ref[...], wm_ref[...], preferred_element_type=F32)
    m = m + gate(gf_ref) * jnp.dot(of, wf_ref[...], preferred_element_type=F32)
    m = m + gate(gn_ref) * jnp.dot(on_ref[...], wn_ref[...], preferred_element_type=F32)
    y = jnp.dot(m.astype(BF16), wo_ref[...], preferred_element_type=F32) + bo_ref[...]
    z = alpha * x_ref[...] + mod_ref[2:3, :] * y
    xn = _ln(z) * lg_ref[...] + lb_ref[...]
    xo_ref[...] = xn
    h = (_ln(xn) * (1.0 + mod_ref[4:5, :]) + mod_ref[3:4, :]).astype(BF16)
    h_ref[...] = h
    lo_ref[...] = jnp.dot(h, wr_ref[...], preferred_element_type=F32) + br_ref[...]


def _merge(om, of_lat, of_ctx, on, u, xall, mod, mod_row, wm, wf, wn, wo, bo, lg, lb, wr, br, alpha, B, S, NC):
    TA, D = xall.shape
    TC = B * NC
    nct = TC // TM
    npb = S // TM
    const = lambda i: (0, 0)
    row = lambda i: (i, 0)
    gcol = U_G // D

    lat = lambda i: jnp.maximum(i - nct, 0)
    outs = pl.pallas_call(
        functools.partial(_merge_kernel, alpha=alpha, nct=nct),
        grid=(TA // TM,),
        in_specs=[pl.BlockSpec((TM, MLA_HEADS * MLA_V), row),
                  pl.BlockSpec((TM, FNET_WIDTH), lambda i: (lat(i) % npb, lat(i) // npb)),
                  pl.BlockSpec((TM, FNET_WIDTH), lambda i: (jnp.minimum(i, nct - 1), 0)),
                  pl.BlockSpec((TM, NAT_WIDTH), row),
                  pl.BlockSpec((TM, D), lambda i: (i, gcol)),
                  pl.BlockSpec((TM, D), lambda i: (i, gcol + 1)),
                  pl.BlockSpec((TM, D), lambda i: (i, gcol + 2)),
                  pl.BlockSpec((TM, D), row),
                  pl.BlockSpec((None, 8, D), lambda i: (mod_row(i), 0, 0)),
                  pl.BlockSpec((MLA_HEADS * MLA_V, D), const),
                  pl.BlockSpec((FNET_WIDTH, D), const),
                  pl.BlockSpec((NAT_WIDTH, D), const),
                  pl.BlockSpec((D, D), const),
                  pl.BlockSpec((1, D), const),
                  pl.BlockSpec((1, D), const),
                  pl.BlockSpec((1, D), const),
                  pl.BlockSpec((D, LANES), const),
                  pl.BlockSpec((1, LANES), const)],
        out_specs=[pl.BlockSpec((TM, D), row), pl.BlockSpec((TM, D), row), pl.BlockSpec((TM, LANES), row)],
        out_shape=[jax.ShapeDtypeStruct((TA, D), F32), jax.ShapeDtypeStruct((TA, D), BF16),
                   jax.ShapeDtypeStruct((TA, LANES), F32)],
        compiler_params=_cparams(("arbitrary",)),
        name="merge",
    )(om, of_lat, of_ctx, on, u, u, u, xall, mod, wm, wf, wn, wo, bo, lg, lb, wr, br)
    return outs


def _router_kernel(lg_ref, e_ref, w_ref):
    lg = lg_ref[...]
    lane = lax.broadcasted_iota(jnp.int32, lg.shape, 1)
    big = jnp.int32(1 << 20)
    is_g = lane < N_GROUPS
    gl = jnp.where(is_g, lg, -jnp.inf)
    ge = jnp.exp(gl - jnp.max(gl, axis=1, keepdims=True))
    p = ge / jnp.sum(ge, axis=1, keepdims=True)
    p_top = jnp.max(p, axis=1, keepdims=True)
    g_idx = jnp.min(jnp.where(is_g & (p == p_top), lane, big), axis=1, keepdims=True)
    lo = N_GROUPS + g_idx * EXPERTS_PER_GROUP
    in_grp = (lane >= lo) & (lane < lo + EXPERTS_PER_GROUP)
    el = jnp.where(in_grp, lg, -jnp.inf)
    m1 = jnp.max(el, axis=1, keepdims=True)
    i1 = jnp.min(jnp.where(in_grp & (el == m1), lane, big), axis=1, keepdims=True)
    rest = in_grp & (lane != i1)
    el2 = jnp.where(rest, lg, -jnp.inf)
    m2 = jnp.max(el2, axis=1, keepdims=True)
    i2 = jnp.min(jnp.where(rest & (el2 == m2), lane, big), axis=1, keepdims=True)
    d = jnp.exp(m2 - m1)
    w1 = p_top * (1.0 / (1.0 + d))
    w2 = p_top * (d / (1.0 + d))
    e_ref[...] = jnp.where(lane == 0, i1 - N_GROUPS, jnp.where(lane == 1, i2 - N_GROUPS, 0))
    w_ref[...] = jnp.where(lane == 0, w1, jnp.where(lane == 1, w2, 0.0))


def _router(logits):
    TA = logits.shape[0]
    row = lambda i: (i, 0)
    return pl.pallas_call(
        _router_kernel,
        grid=(TA // TM,),
        in_specs=[pl.BlockSpec((TM, LANES), row)],
        out_specs=[pl.BlockSpec((TM, LANES), row)] * 2,
        out_shape=[jax.ShapeDtypeStruct((TA, LANES), jnp.int32), jax.ShapeDtypeStruct((TA, LANES), F32)],
        compiler_params=_cparams(("arbitrary",)),
        name="router",
    )(logits)


def _moe_kernel(be_ref, nu_ref, x_ref, wg_ref, wu_ref, wd_ref, o_ref, wgb_ref, wub_ref, wdb_ref):
    i = pl.program_id(0)
    changed = jnp.logical_or(i == 0, be_ref[i] != be_ref[jnp.maximum(i - 1, 0)])

    @pl.when(changed)
    def _():
        wgb_ref[...] = wg_ref[...].astype(BF16)
        wub_ref[...] = wu_ref[...].astype(BF16)
        wdb_ref[...] = wd_ref[...].astype(BF16)

    @pl.when(i < nu_ref[0])
    def _():
        x = x_ref[...]
        a = jnp.dot(x, wgb_ref[...], preferred_element_type=F32)
        b = jnp.dot(x, wub_ref[...], preferred_element_type=F32)
        hmid = (a * (1.0 / (1.0 + jnp.exp(-a))) * b).astype(BF16)
        o_ref[...] = jnp.dot(hmid, wdb_ref[...], preferred_element_type=F32).astype(o_ref.dtype)

    @pl.when(i >= nu_ref[0])
    def _():
        o_ref[...] = jnp.zeros(o_ref.shape, o_ref.dtype)


def _moe_experts(xb, block_e, n_used, w_gate, w_up, w_down):
    n_rows, D = xb.shape
    nb = n_rows // MOE_BM
    E, _, DE = w_gate.shape
    xrow = lambda i, be, nu: (jnp.minimum(i, nu[0] - 1), 0)
    grid_spec = pltpu.PrefetchScalarGridSpec(
        num_scalar_prefetch=2,
        grid=(nb,),
        in_specs=[pl.BlockSpec((MOE_BM, D), xrow),
                  pl.BlockSpec((None, D, DE), lambda i, be, nu: (be[i], 0, 0)),
                  pl.BlockSpec((None, D, DE), lambda i, be, nu: (be[i], 0, 0)),
                  pl.BlockSpec((None, DE, D), lambda i, be, nu: (be[i], 0, 0))],
        out_specs=pl.BlockSpec((MOE_BM, D), lambda i, be, nu: (i, 0)),
        scratch_shapes=[pltpu.VMEM((D, DE), BF16), pltpu.VMEM((D, DE), BF16), pltpu.VMEM((DE, D), BF16)],
    )
    return pl.pallas_call(
        _moe_kernel,
        grid_spec=grid_spec,
        out_shape=jax.ShapeDtypeStruct((n_rows, D), F32),
        compiler_params=_cparams(("arbitrary",)),
        name="moe_experts",
    )(block_e, n_used, xb, w_gate, w_up, w_down)


def _ffn_out_kernel(x_ref, y0_ref, y1_ref, w_ref, mod_ref, lg_ref, lb_ref, o_ref, *, alpha):
    w = w_ref[...]
    f = y0_ref[...].astype(F32) * w[:, 0:1] + y1_ref[...].astype(F32) * w[:, 1:2]
    z = alpha * x_ref[...] + mod_ref[5:6, :] * f
    o_ref[...] = _ln(z) * lg_ref[...] + lb_ref[...]


def _ffn_out(xall, y0, y1, w_tok, mod, mod_row, lg, lb, alpha):
    TA, D = xall.shape
    row = lambda i: (i, 0)
    const = lambda i: (0, 0)
    return pl.pallas_call(
        functools.partial(_ffn_out_kernel, alpha=alpha),
        grid=(TA // TM,),
        in_specs=[pl.BlockSpec((TM, D), row), pl.BlockSpec((TM, D), row), pl.BlockSpec((TM, D), row),
                  pl.BlockSpec((TM, LANES), row),
                  pl.BlockSpec((None, 8, D), lambda i: (mod_row(i), 0, 0)),
                  pl.BlockSpec((1, D), const), pl.BlockSpec((1, D), const)],
        out_specs=pl.BlockSpec((TM, D), row),
        out_shape=jax.ShapeDtypeStruct((TA, D), F32),
        compiler_params=_cparams(("arbitrary",)),
        name="ffn_out",
    )(xall, y0, y1, w_tok, mod, lg, lb)


def _reorder_w_in(w_in, b_in):
    D = w_in.shape[0]
    o_cq, o_ckv, o_kr = 0, MLA_Q_LORA, MLA_Q_LORA + MLA_KV_LORA
    o_f = o_kr + MLA_ROPE
    o_nq = o_f + FNET_WIDTH
    o_g = o_nq + 3 * NAT_WIDTH
    rot_src, rot_sign = _rope_partner()

    def build(a):
        kr = a[..., o_kr:o_kr + MLA_ROPE]
        kr_rot = kr[..., rot_src] * rot_sign
        pad = jnp.zeros(a.shape[:-1] + (LANES - 2 * MLA_ROPE,), a.dtype)
        return jnp.concatenate([a[..., o_f:o_g], a[..., o_g:], a[..., o_cq:o_kr], kr, kr_rot, pad], axis=-1)

    return build(w_in).astype(BF16), build(b_in[None, :])


def _rope_partner():
    q = MLA_ROPE // 4
    src = np.zeros(MLA_ROPE, np.int32)
    sign = np.zeros(MLA_ROPE, np.float32)
    for half in range(2):
        o = half * 2 * q
        for i in range(q):
            src[o + i], sign[o + i] = o + q + i, -1.0
            src[o + q + i], sign[o + q + i] = o + i, 1.0
    return src, sign


def _pad_heads(w, width, offset=0):
    K = w.shape[0]
    w = w.reshape(K, MLA_HEADS, width)
    out = jnp.zeros((K, MLA_HEADS, HEAD_PAD), w.dtype)
    return out.at[:, :, offset:offset + width].set(w).reshape(K, MLA_HEADS * HEAD_PAD)


def _mla_weights(w_uq, w_uk, w_uv):
    rot_src, rot_sign = _rope_partner()
    dq = MLA_NOPE + MLA_ROPE
    wq = w_uq.reshape(-1, MLA_HEADS, dq)
    wq_pe_rot = wq[:, :, MLA_NOPE:][:, :, rot_src] * rot_sign
    wqa = _pad_heads(wq.reshape(-1, MLA_HEADS * dq), dq)
    wqb = _pad_heads(wq_pe_rot.reshape(-1, MLA_HEADS * MLA_ROPE), MLA_ROPE, MLA_NOPE)
    wk = _pad_heads(w_uk, MLA_NOPE)
    K = w_uv.shape[0]
    wv3 = w_uv.reshape(K, MLA_HEADS // 2, 2, MLA_V)
    wv = jnp.zeros((K, MLA_HEADS // 2, 2, HEAD_PAD), w_uv.dtype)
    wv = wv.at[:, :, 0, :MLA_V].set(wv3[:, :, 0]).at[:, :, 1, MLA_V:].set(wv3[:, :, 1])
    wv = wv.reshape(K, MLA_HEADS * HEAD_PAD)
    vone = np.zeros((MLA_HEADS // 2, 2, HEAD_PAD), np.float32)
    vone[:, 0, MLA_V] = 1.0
    vone[:, 1, 0] = 1.0
    pa = np.zeros((LANES, MLA_HEADS, HEAD_PAD), np.float32)
    pb = np.zeros((LANES, MLA_HEADS, HEAD_PAD), np.float32)
    for i in range(MLA_ROPE):
        pa[i, :, MLA_NOPE + i] = 1.0
        pb[MLA_ROPE + i, :, MLA_NOPE + i] = 1.0
    W = MLA_HEADS * HEAD_PAD
    return (wqa.astype(BF16), wqb.astype(BF16), wk.astype(BF16), wv.astype(BF16),
            jnp.asarray(pa.reshape(LANES, W), BF16), jnp.asarray(pb.reshape(LANES, W), BF16),
            jnp.asarray(vone.reshape(1, W)))


def _rope_lane_tables(S):
    t = jnp.arange(S)
    rows = (t // GRID_W).astype(F32)
    cols = (t % GRID_W).astype(F32)
    n_freq = MLA_ROPE // 4
    inv = jnp.power(ROPE_BASE, -jnp.arange(n_freq, dtype=F32) / n_freq)
    ar = rows[:, None] * inv[None, :]
    ac = cols[:, None] * inv[None, :]
    ones = jnp.ones((S, MLA_NOPE), F32)
    zpad = jnp.zeros((S, HEAD_PAD - MLA_NOPE - MLA_ROPE), F32)
    cos = jnp.concatenate([ones, jnp.cos(ar), jnp.cos(ar), jnp.cos(ac), jnp.cos(ac), zpad], axis=1)
    sin = jnp.concatenate([0 * ones, jnp.sin(ar), jnp.sin(ar), jnp.sin(ac), jnp.sin(ac), zpad], axis=1)
    idc = jnp.concatenate([jnp.ones((TM, MLA_NOPE + MLA_ROPE), F32), jnp.zeros((TM, HEAD_PAD - MLA_NOPE - MLA_ROPE), F32)], axis=1)
    return jnp.concatenate([idc, cos], axis=0), jnp.concatenate([jnp.zeros((TM, HEAD_PAD), F32), sin], axis=0)


def _moe_plan(e_ids, n_blocks):
    a = e_ids.shape[0] * 2
    flat_e = e_ids.reshape(a)
    order = jnp.argsort(flat_e, stable=True).astype(jnp.int32)
    e_sorted = flat_e[order]
    counts = jnp.zeros((N_EXPERTS,), jnp.int32).at[flat_e].add(1)
    starts = jnp.cumsum(counts) - counts
    padded = (counts + MOE_BM - 1) // MOE_BM * MOE_BM
    pends = jnp.cumsum(padded)
    pstarts = pends - padded
    dest = pstarts[e_sorted] + jnp.arange(a, dtype=jnp.int32) - starts[e_sorted]
    n_rows = n_blocks * MOE_BM
    row_asg = jnp.full((n_rows,), -1, jnp.int32).at[dest].set(order)
    pos = jnp.zeros((a,), jnp.int32).at[order].set(dest)
    block_e = jnp.minimum(jnp.searchsorted(pends, jnp.arange(n_blocks, dtype=jnp.int32) * MOE_BM, side='right'),
                          N_EXPERTS - 1).astype(jnp.int32)
    n_used = (pends[-1] // MOE_BM).astype(jnp.int32).reshape(1)
    return row_asg, pos, block_e, n_used


def kernel(x, c, ctx, c_ctx, w_ada, b_ada, w_in, b_in, q_norm_g, kv_norm_g, w_uq, w_uk, w_uv, rpb, w_o_mla, w_o_fnet, w_o_nat, w_out, b_out, ln1_g, ln1_b, w_rg, b_rg, w_re, b_re, w_gate_e, w_up_e, w_down_e, ln2_g, ln2_b):
    B, S, D = x.shape
    NC = ctx.shape[1]
    L = w_ada.shape[0]
    TC, TL = B * NC, B * S
    TA = TC + TL
    assert TC % TM == 0 and S % TM == 0 and NC % 128 == 0 and B < 8
    alpha = float((2 * L) ** 0.25)
    nct, npb = TC // TM, S // TM

    def mod_row(i):
        return jnp.where(i < nct, B, jnp.maximum(i - nct, 0) // npb)

    def tab_row(i):
        return jnp.where(i < nct, 0, 1 + jnp.maximum(i - nct, 0) % npb)

    cc = jnp.concatenate([c, c_ctx[None, :], jnp.zeros((8 - B - 1, D), F32)], axis=0)
    mod_all = _ada_all(cc, w_ada, b_ada).reshape(L, 8, 6, D)
    mod_all = jnp.concatenate([mod_all, jnp.zeros((L, 8, 2, D), F32)], axis=2)

    tabc, tabs = _rope_lane_tables(S)
    chan_scale = FNET_GROUP_DIM ** -0.5
    cch, sch = _dft_tables(FNET_GROUP_DIM, chan_scale)
    w_chan = jnp.concatenate([cch, sch], axis=1).astype(BF16)
    a_lat = _seq_dft_matrix(S)
    a_ctx = _seq_dft_matrix(NC)
    qscale = float((MLA_NOPE + MLA_ROPE) ** -0.5) * LOG2E
    rows_n = S // GRID_W
    n_blocks = (2 * TA) // MOE_BM + N_EXPERTS
    tq = min(512, S)
    tk = math.gcd(1024, math.gcd(S, TC))

    xall = jnp.concatenate([ctx.reshape(TC, D), x.reshape(TL, D)], axis=0)
    for l in range(L):
        mod = mod_all[l]
        w_in_r, b_in_r = _reorder_w_in(w_in[l], b_in[l])
        u = _in_proj(xall, mod, w_in_r, b_in_r, mod_row)

        wqa, wqb, wk, wv, pa, pb, vone = _mla_weights(w_uq[l], w_uk[l], w_uv[l])
        q, k, v = _qkv(u, tabc, tabs, tab_row, q_norm_g[l][None, :], kv_norm_g[l][None, :],
                       wqa, wqb, wk, wv, pa, pb, vone, qscale)
        om_ctx, om_lat = _mla_attention(q, k, v, B, S, NC, tq, tk)

        bias = _nat_bias_table(rpb[l], rows_n)
        on_ctx, on_lat = _nat_attention(u, bias, B, S, NC)

        of_lat = _fourier_mix(u, TC, B, S, a_lat, w_chan, ts=TM, tm=min(1024, S), tk=min(2048, 2 * S))
        of_ctx = _fourier_mix(u, 0, B, NC, a_ctx, w_chan, ts=NC, tm=NC, tk=2 * NC)
        of_ctx = of_ctx.reshape(NC, B, FNET_WIDTH).transpose(1, 0, 2).reshape(TC, FNET_WIDTH)

        om = jnp.concatenate([om_ctx, om_lat], axis=0)
        on = jnp.concatenate([on_ctx, on_lat], axis=0)
        w_r = jnp.concatenate([w_rg[l], w_re[l], jnp.zeros((D, LANES - N_GROUPS - N_EXPERTS), F32)], axis=1).astype(BF16)
        b_r = jnp.concatenate([b_rg[l], b_re[l], jnp.zeros((LANES - N_GROUPS - N_EXPERTS,), F32)])[None, :]
        xall, h_f, logits = _merge(om, of_lat, of_ctx, on, u, xall, mod, mod_row,
                                   w_o_mla[l].astype(BF16), w_o_fnet[l].astype(BF16), w_o_nat[l].astype(BF16),
                                   w_out[l].astype(BF16), b_out[l][None, :], ln1_g[l][None, :], ln1_b[l][None, :],
                                   w_r, b_r, alpha, B, S, NC)

        e_out, w_tok = _router(logits)
        row_asg, pos, block_e, n_used = _moe_plan(e_out[:, :2], n_blocks)
        xb = jnp.take(h_f, jnp.maximum(row_asg, 0) // 2, axis=0)
        yb = _moe_experts(xb, block_e, n_used, w_gate_e[l], w_up_e[l], w_down_e[l])
        pos2 = pos.reshape(TA, 2)
        y0 = jnp.take(yb, pos2[:, 0], axis=0)
        y1 = jnp.take(yb, pos2[:, 1], axis=0)
        xall = _ffn_out(xall, y0, y1, w_tok, mod, mod_row, ln2_g[l][None, :], ln2_b[l][None, :], alpha)
    return xall[TC:].reshape(B, S, D)
```

```python
import functools
import math

import numpy as np
import jax
import jax.numpy as jnp
from jax import lax
from jax.experimental import pallas as pl
from jax.experimental.pallas import tpu as pltpu

F32 = jnp.float32
BF16 = jnp.bfloat16

GRID_W = 64
MLA_HEADS = 8
MLA_Q_LORA = 256
MLA_KV_LORA = 128
MLA_NOPE = 64
MLA_ROPE = 32
MLA_V = 64
ROPE_BASE = 10000.0
FNET_GROUPS = 4
FNET_GROUP_DIM = 128
FNET_WIDTH = FNET_GROUPS * FNET_GROUP_DIM
NAT_HEADS = 8
NAT_HEAD_DIM = 64
NAT_WIDTH = NAT_HEADS * NAT_HEAD_DIM
NAT_WIN_H = 8
NAT_WIN_W = 16
N_GROUPS = 4
EXPERTS_PER_GROUP = 8
N_EXPERTS = N_GROUPS * EXPERTS_PER_GROUP
D_EXPERT = 512
LN_EPS = 1e-6

LANES = 128
HEAD_PAD = 128
LOG2E = 1.4426950408889634
NEG_BIG = -1e30
VMEM_LIMIT = 56 * 1024 * 1024

U_F, U_NQ, U_NK, U_NV, U_G, U_CQ, U_CKV, U_KR, U_W = 0, 512, 1024, 1536, 2048, 5120, 5376, 5504, 5632

TM = 512
NAT_ROWS = 4
NAT_KROWS = 12
MOE_BM = 256


def _cparams(sem):
    return pltpu.CompilerParams(dimension_semantics=sem, vmem_limit_bytes=VMEM_LIMIT)


def _ln(x):
    mu = jnp.mean(x, axis=-1, keepdims=True)
    xc = x - mu
    var = jnp.mean(xc * xc, axis=-1, keepdims=True)
    return xc * lax.rsqrt(var + LN_EPS)


def _ada_kernel(c_ref, w_ref, b_ref, o_ref):
    c = c_ref[...]
    a = c * (1.0 / (1.0 + jnp.exp(-c)))
    o_ref[...] = jnp.dot(a, w_ref[...], preferred_element_type=F32,
                         precision=lax.Precision.HIGHEST) + b_ref[...]


def _ada_all(cc, w_ada, b_ada):
    L, D, N = w_ada.shape
    tn = 1536
    return pl.pallas_call(
        _ada_kernel,
        grid=(L, N // tn),
        in_specs=[pl.BlockSpec((8, D), lambda l, j: (0, 0)),
                  pl.BlockSpec((None, D, tn), lambda l, j: (l, 0, j)),
                  pl.BlockSpec((None, 1, tn), lambda l, j: (l, 0, j))],
        out_specs=pl.BlockSpec((None, 8, tn), lambda l, j: (l, 0, j)),
        out_shape=jax.ShapeDtypeStruct((L, 8, N), F32),
        compiler_params=_cparams(("arbitrary", "arbitrary")),
        name="ada_mod",
    )(cc, w_ada, b_ada.reshape(L, 1, N))


def _in_proj_kernel(x_ref, mod_ref, w_ref, b_ref, o_ref, *, cw):
    y = _ln(x_ref[...])
    h = (y * (1.0 + mod_ref[1:2, :]) + mod_ref[0:1, :]).astype(BF16)
    for j in range(o_ref.shape[1] // cw):
        sl = slice(j * cw, (j + 1) * cw)
        acc = jnp.dot(h, w_ref[:, sl], preferred_element_type=F32)
        o_ref[:, sl] = (acc + b_ref[:, sl]).astype(o_ref.dtype)


def _in_proj(xall, mod, w_in_r, b_in_r, mod_row):
    TA, D = xall.shape
    N = w_in_r.shape[1]
    return pl.pallas_call(
        functools.partial(_in_proj_kernel, cw=512),
        grid=(TA // TM,),
        in_specs=[pl.BlockSpec((TM, D), lambda i: (i, 0)),
                  pl.BlockSpec((None, 8, D), lambda i: (mod_row(i), 0, 0)),
                  pl.BlockSpec((D, N), lambda i: (0, 0)),
                  pl.BlockSpec((1, N), lambda i: (0, 0))],
        out_specs=pl.BlockSpec((TM, N), lambda i: (i, 0)),
        out_shape=jax.ShapeDtypeStruct((TA, N), BF16),
        compiler_params=_cparams(("arbitrary",)),
        name="in_proj",
    )(xall, mod, w_in_r, b_in_r)


def _rms(x, g):
    return x * lax.rsqrt(jnp.mean(x * x, axis=-1, keepdims=True) + LN_EPS) * g


def _qkv_kernel(cq_ref, ckv_ref, kr_ref, tc_ref, ts_ref, qg_ref, kvg_ref, wqa_ref, wqb_ref,
                wk_ref, wv_ref, pa_ref, pb_ref, vone_ref, q_ref, k_ref, v_ref, *, qscale):
    cqn = _rms(cq_ref[...].astype(F32), qg_ref[...]).astype(BF16)
    ckvn = _rms(ckv_ref[...].astype(F32), kvg_ref[...]).astype(BF16)
    cos = jnp.concatenate([tc_ref[...]] * MLA_HEADS, axis=1)
    sin = jnp.concatenate([ts_ref[...]] * MLA_HEADS, axis=1)
    qa = jnp.dot(cqn, wqa_ref[...], preferred_element_type=F32)
    qb = jnp.dot(cqn, wqb_ref[...], preferred_element_type=F32)
    q_ref[...] = ((qa * cos + qb * sin) * qscale).astype(q_ref.dtype)
    kr = kr_ref[...]
    ka = jnp.dot(kr, pa_ref[...], preferred_element_type=F32)
    kb = jnp.dot(kr, pb_ref[...], preferred_element_type=F32)
    kn = jnp.dot(ckvn, wk_ref[...], preferred_element_type=F32)
    k_ref[...] = (kn + ka * cos + kb * sin).astype(k_ref.dtype)
    v = jnp.dot(ckvn, wv_ref[...], preferred_element_type=F32) + vone_ref[...]
    v_ref[...] = v.astype(v_ref.dtype)


def _qkv(u, tabc, tabs, tab_row, qg, kvg, wqa, wqb, wk, wv, pa, pb, vone, qscale):
    TA = u.shape[0]
    W = MLA_HEADS * HEAD_PAD
    const = lambda i: (0, 0)
    out = jax.ShapeDtypeStruct((TA, W), BF16)
    return pl.pallas_call(
        functools.partial(_qkv_kernel, qscale=qscale),
        grid=(TA // TM,),
        in_specs=[pl.BlockSpec((TM, MLA_Q_LORA), lambda i: (i, U_CQ // MLA_Q_LORA)),
                  pl.BlockSpec((TM, MLA_KV_LORA), lambda i: (i, U_CKV // MLA_KV_LORA)),
                  pl.BlockSpec((TM, LANES), lambda i: (i, U_KR // LANES)),
                  pl.BlockSpec((TM, LANES), lambda i: (tab_row(i), 0)),
                  pl.BlockSpec((TM, LANES), lambda i: (tab_row(i), 0)),
                  pl.BlockSpec((1, MLA_Q_LORA), const),
                  pl.BlockSpec((1, MLA_KV_LORA), const),
                  pl.BlockSpec((MLA_Q_LORA, W), const),
                  pl.BlockSpec((MLA_Q_LORA, W), const),
                  pl.BlockSpec((MLA_KV_LORA, W), const),
                  pl.BlockSpec((MLA_KV_LORA, W), const),
                  pl.BlockSpec((LANES, W), const),
                  pl.BlockSpec((LANES, W), const),
                  pl.BlockSpec((1, W), const)],
        out_specs=[pl.BlockSpec((TM, W), lambda i: (i, 0))] * 3,
        out_shape=[out, out, out],
        compiler_params=_cparams(("arbitrary",)),
        name="mla_qkv",
    )(u, u, u, tabc, tabs, qg, kvg, wqa, wqb, wk, wv, pa, pb, vone)


def _mla_attend(q_ref, k_ref, v_ref, acc_ref, m_ref):
    tk = k_ref.shape[0]
    for h in range(MLA_HEADS):
        hs = slice(h * HEAD_PAD, (h + 1) * HEAD_PAD)
        s = lax.dot_general(q_ref[:, hs], k_ref[:, hs], (((1,), (1,)), ((), ())),
                            preferred_element_type=F32)
        m_prev = m_ref[h]
        m_new = jnp.maximum(m_prev, jnp.max(s, axis=1, keepdims=True))
        alpha = jnp.exp2(m_prev - m_new)
        p = jnp.exp2(s - jnp.concatenate([m_new] * (tk // LANES), axis=1))
        pv = jnp.dot(p.astype(BF16), v_ref[:, hs], preferred_element_type=F32)
        acc_ref[h] = acc_ref[h] * alpha + pv
        m_ref[h] = m_new


def _mla_finish(o_ref, acc_ref):
    lane = lax.broadcasted_iota(jnp.int32, acc_ref.shape[1:], 1)
    for hp in range(MLA_HEADS // 2):
        a0 = acc_ref[2 * hp]
        a1 = acc_ref[2 * hp + 1]
        l0 = jnp.sum(jnp.where(lane == MLA_V, a0, 0.0), axis=1, keepdims=True)
        l1 = jnp.sum(jnp.where(lane == 0, a1, 0.0), axis=1, keepdims=True)
        o = jnp.where(lane < MLA_V, a0 / l0, a1 / l1)
        o_ref[:, hp * LANES:(hp + 1) * LANES] = o.astype(o_ref.dtype)


def _mla_lat_kernel(q_ref, kc_ref, vc_ref, kl_ref, vl_ref, o_ref, acc_ref, m_ref):
    j = pl.program_id(2)

    @pl.when(j == 0)
    def _():
        m_ref[...] = jnp.full(m_ref.shape, NEG_BIG, F32)
        acc_ref[...] = jnp.zeros(acc_ref.shape, F32)
        _mla_attend(q_ref, kc_ref, vc_ref, acc_ref, m_ref)

    _mla_attend(q_ref, kl_ref, vl_ref, acc_ref, m_ref)

    @pl.when(j == pl.num_programs(2) - 1)
    def _():
        _mla_finish(o_ref, acc_ref)


def _mla_ctx_kernel(q_ref, kc_ref, vc_ref, o_ref, acc_ref, m_ref):
    m_ref[...] = jnp.full(m_ref.shape, NEG_BIG, F32)
    acc_ref[...] = jnp.zeros(acc_ref.shape, F32)
    _mla_attend(q_ref, kc_ref, vc_ref, acc_ref, m_ref)
    _mla_finish(o_ref, acc_ref)


def _mla_attention(q, k, v, B, S, NC, tq, tk):
    TA, W = q.shape
    TC = B * NC
    OW = MLA_HEADS * MLA_V
    nq, nk = S // tq, S // tk
    scratch = lambda t: [pltpu.VMEM((MLA_HEADS, t, HEAD_PAD), F32), pltpu.VMEM((MLA_HEADS, t, HEAD_PAD), F32)]
    o_lat = pl.pallas_call(
        _mla_lat_kernel,
        grid=(B, nq, nk),
        in_specs=[pl.BlockSpec((tq, W), lambda b, i, j: (TC // tq + b * nq + i, 0)),
                  pl.BlockSpec((NC, W), lambda b, i, j: (b, 0)),
                  pl.BlockSpec((NC, W), lambda b, i, j: (b, 0)),
                  pl.BlockSpec((tk, W), lambda b, i, j: (TC // tk + b * nk + j, 0)),
                  pl.BlockSpec((tk, W), lambda b, i, j: (TC // tk + b * nk + j, 0))],
        out_specs=pl.BlockSpec((tq, OW), lambda b, i, j: (b * nq + i, 0)),
        out_shape=jax.ShapeDtypeStruct((B * S, OW), BF16),
        scratch_shapes=scratch(tq),
        compiler_params=_cparams(("arbitrary", "arbitrary", "arbitrary")),
        name="mla_attn_lat",
    )(q, k, v, k, v)
    o_ctx = pl.pallas_call(
        _mla_ctx_kernel,
        grid=(B,),
        in_specs=[pl.BlockSpec((NC, W), lambda b: (b, 0))] * 3,
        out_specs=pl.BlockSpec((NC, OW), lambda b: (b, 0)),
        out_shape=jax.ShapeDtypeStruct((TC, OW), BF16),
        scratch_shapes=scratch(NC),
        compiler_params=_cparams(("arbitrary",)),
        name="mla_attn_ctx",
    )(q, k, v)
    return o_ctx, o_lat


def _nat_softmax_pv(scores, values):
    m = scores[0].max(axis=1, keepdims=True)
    for s in scores[1:]:
        m = jnp.maximum(m, s.max(axis=1, keepdims=True))
    l = None
    o = None
    for s, v in zip(scores, values):
        p = jnp.exp2(s - m)
        ls = jnp.sum(p, axis=1, keepdims=True)
        pv = jnp.dot(p.astype(BF16), v, preferred_element_type=F32)
        l = ls if l is None else l + ls
        o = pv if o is None else o + pv
    return o / l


def _nat_lat_kernel(q_ref, k0_ref, k1_ref, k2_ref, v0_ref, v1_ref, v2_ref, kc_ref, vc_ref, bias_ref, o_ref,
                    *, qscale):
    tq = q_ref.shape[0]
    ck = k0_ref.shape[0]
    lane = lax.broadcasted_iota(jnp.int32, (tq, LANES), 1)
    dn = (((1,), (1,)), ((), ()))
    for hp in range(NAT_HEADS // 2):
        ls = slice(hp * LANES, (hp + 1) * LANES)
        qp = q_ref[:, ls].astype(F32) * qscale
        keys = [r[:, ls] for r in (k0_ref, k1_ref, k2_ref, kc_ref)]
        vals = [r[:, ls] for r in (v0_ref, v1_ref, v2_ref, vc_ref)]
        outs = []
        for hh in range(2):
            mask = (lane < NAT_HEAD_DIM) if hh == 0 else (lane >= NAT_HEAD_DIM)
            qm = jnp.where(mask, qp, 0.0).astype(BF16)
            scores = []
            for c, kk in enumerate(keys):
                s = lax.dot_general(qm, kk, dn, preferred_element_type=F32)
                if c < 3:
                    s = s + bias_ref[2 * hp + hh, :, c * ck:(c + 1) * ck].astype(F32)
                scores.append(s)
            outs.append(_nat_softmax_pv(scores, vals))
        o_ref[:, ls] = jnp.where(lane < NAT_HEAD_DIM, outs[0], outs[1]).astype(o_ref.dtype)


def _nat_ctx_kernel(q_ref, kc_ref, vc_ref, o_ref, *, qscale):
    tq = q_ref.shape[0]
    lane = lax.broadcasted_iota(jnp.int32, (tq, LANES), 1)
    dn = (((1,), (1,)), ((), ()))
    for hp in range(NAT_HEADS // 2):
        ls = slice(hp * LANES, (hp + 1) * LANES)
        qp = q_ref[:, ls].astype(F32) * qscale
        outs = []
        for hh in range(2):
            mask = (lane < NAT_HEAD_DIM) if hh == 0 else (lane >= NAT_HEAD_DIM)
            qm = jnp.where(mask, qp, 0.0).astype(BF16)
            s = lax.dot_general(qm, kc_ref[:, ls], dn, preferred_element_type=F32)
            outs.append(_nat_softmax_pv([s], [vc_ref[:, ls]]))
        o_ref[:, ls] = jnp.where(lane < NAT_HEAD_DIM, outs[0], outs[1]).astype(o_ref.dtype)


def _nat_bias_table(rpb, rows_n):
    L = rpb.shape[0]
    qc = np.arange(GRID_W)
    cs = np.clip(qc - NAT_WIN_W // 2, 0, GRID_W - NAT_WIN_W)
    kc = np.arange(GRID_W)
    colvalid = (kc[None, :] >= cs[:, None]) & (kc[None, :] < cs[:, None] + NAT_WIN_W)
    dc = np.clip(kc[None, :] - qc[:, None] + NAT_WIN_W - 1, 0, 2 * NAT_WIN_W - 2)
    bt = jnp.where(colvalid, rpb[:, :, :, dc] * LOG2E, NEG_BIG)
    n_dr = 2 * NAT_WIN_H - 1
    bt = jnp.concatenate([bt, jnp.full((L, NAT_HEADS, 1, GRID_W, GRID_W), NEG_BIG, F32)], axis=2).astype(BF16)
    kh = NAT_WIN_H
    dr = np.full((3, NAT_ROWS, NAT_KROWS), n_dr, np.int32)
    for v, r0 in enumerate((0, 2 * NAT_ROWS, rows_n - NAT_ROWS)):
        kstart = int(np.clip(r0 - kh // 2, 0, rows_n - NAT_KROWS))
        for i in range(NAT_ROWS):
            r = r0 + i
            rs = int(np.clip(r - kh // 2, 0, rows_n - kh))
            for jj in range(NAT_KROWS):
                kr = kstart + jj
                if rs <= kr < rs + kh:
                    dr[v, i, jj] = kr - r + NAT_WIN_H - 1
    t = bt[:, :, dr]
    t = t.transpose(0, 2, 1, 3, 5, 4, 6)
    return t.reshape(L, 3, NAT_HEADS, NAT_ROWS * GRID_W, NAT_KROWS * GRID_W)


def _nat_attention(u, bias, layer, B, S, NC):
    TA = u.shape[0]
    TC = B * NC
    rows_n = S // GRID_W
    nblk = rows_n // NAT_ROWS
    tq = NAT_ROWS * GRID_W
    ck = tq
    qscale = float(NAT_HEAD_DIM ** -0.5) * LOG2E
    cq, ckk, cv = U_NQ // NAT_WIDTH, U_NK // NAT_WIDTH, U_NV // NAT_WIDTH
    base = lambda b: TC // tq + b * nblk

    def kstart_blk(i):
        return jnp.clip(i * NAT_ROWS - NAT_WIN_H // 2, 0, rows_n - NAT_KROWS) // NAT_ROWS

    def kv_spec(col, c):
        return pl.BlockSpec((ck, NAT_WIDTH), lambda b, i: (base(b) + kstart_blk(i) + c, col))

    def variant(i):
        return jnp.where(i == 0, 0, jnp.where(i == nblk - 1, 2, 1))

    o_lat = pl.pallas_call(
        functools.partial(_nat_lat_kernel, qscale=qscale),
        grid=(B, nblk),
        in_specs=[pl.BlockSpec((tq, NAT_WIDTH), lambda b, i: (base(b) + i, cq)),
                  kv_spec(ckk, 0), kv_spec(ckk, 1), kv_spec(ckk, 2),
                  kv_spec(cv, 0), kv_spec(cv, 1), kv_spec(cv, 2),
                  pl.BlockSpec((NC, NAT_WIDTH), lambda b, i: (b, ckk)),
                  pl.BlockSpec((NC, NAT_WIDTH), lambda b, i: (b, cv)),
                  pl.BlockSpec((None, None, NAT_HEADS, tq, NAT_KROWS * GRID_W),
                               lambda b, i: (layer, variant(i), 0, 0, 0))],
        out_specs=pl.BlockSpec((tq, NAT_WIDTH), lambda b, i: (b * nblk + i, 0)),
        out_shape=jax.ShapeDtypeStruct((B * S, NAT_WIDTH), BF16),
        compiler_params=_cparams(("arbitrary", "arbitrary")),
        name="nat_attn_lat",
    )(u, u, u, u, u, u, u, u, u, bias)
    o_ctx = pl.pallas_call(
        functools.partial(_nat_ctx_kernel, qscale=qscale),
        grid=(B,),
        in_specs=[pl.BlockSpec((NC, NAT_WIDTH), lambda b: (b, cq)),
                  pl.BlockSpec((NC, NAT_WIDTH), lambda b: (b, ckk)),
                  pl.BlockSpec((NC, NAT_WIDTH), lambda b: (b, cv))],
        out_specs=pl.BlockSpec((NC, NAT_WIDTH), lambda b: (b, 0)),
        out_shape=jax.ShapeDtypeStruct((TC, NAT_WIDTH), BF16),
        compiler_params=_cparams(("arbitrary",)),
        name="nat_attn_ctx",
    )(u, u, u)
    return o_ctx, o_lat


def _fnet_chan_kernel(f_ref, w_ref, o_ref):
    for g in range(FNET_GROUPS):
        gs = slice(g * FNET_GROUP_DIM, (g + 1) * FNET_GROUP_DIM)
        o_ref[:, gs] = jnp.dot(f_ref[:, gs], w_ref[...], preferred_element_type=F32).astype(o_ref.dtype)


def _fnet_seq_kernel(a_ref, r_ref, o_ref, acc_ref):
    k = pl.program_id(2)

    @pl.when(k == 0)
    def _():
        acc_ref[...] = jnp.zeros(acc_ref.shape, F32)

    acc_ref[...] += jnp.dot(a_ref[...], r_ref[...], preferred_element_type=F32)

    @pl.when(k == pl.num_programs(2) - 1)
    def _():
        o_ref[...] = acc_ref[...].astype(o_ref.dtype)


def _dft_tables(n, scale):
    j = jnp.arange(n, dtype=jnp.int32)
    idx = (j[:, None] * j[None, :]) & (n - 1)
    ang = idx.astype(F32) * (2.0 * math.pi / n)
    return jnp.cos(ang) * scale, jnp.sin(ang) * scale


def _seq_dft_matrix(n):
    c, s = _dft_tables(n, n ** -0.5)
    return jnp.concatenate([c, -s], axis=1).astype(BF16)


FFT_N2 = 128
FFT_NB = 16


def _fft_tables(n):
    n2 = FFT_N2
    n1 = n // n2
    j1 = jnp.arange(n1, dtype=jnp.int32)[:, None, None]
    k2 = jnp.arange(n2, dtype=jnp.int32)[None, :, None]
    j2 = jnp.arange(n2, dtype=jnp.int32)[None, None, :]
    ang = ((k2 * (j1 + n1 * j2)) & (n - 1)).astype(F32) * (2.0 * math.pi / n)
    s1 = n2 ** -0.5
    m1 = jnp.concatenate([jnp.cos(ang) * s1, -jnp.sin(ang) * s1], axis=1).astype(BF16)
    c1, sn1 = _dft_tables(n1, n1 ** -0.5)
    top = jnp.stack([c1, sn1], axis=2).reshape(n1, 2 * n1)
    bot = jnp.stack([-sn1, c1], axis=2).reshape(n1, 2 * n1)
    w2 = jnp.concatenate([top, bot], axis=0).astype(BF16)
    return m1, w2


def _load_slab(ref, i):
    rows = ref.shape[1] // FFT_NB
    parts = [ref[g, pl.ds(i, rows, stride=FFT_NB), :] for g in range(FNET_GROUPS)]
    return jnp.concatenate(parts, axis=1)


def _fft1_kernel(f_ref, m_ref, o_ref):
    n2 = m_ref.shape[2]
    for i in range(FFT_NB):
        r = jnp.dot(m_ref[i], _load_slab(f_ref, i).astype(BF16), preferred_element_type=F32)
        for g in range(FNET_GROUPS):
            gs = slice(g * FNET_GROUP_DIM, (g + 1) * FNET_GROUP_DIM)
            for p in range(2):
                for kb in range(n2 // FFT_NB):
                    src = p * n2 + kb * FFT_NB
                    dst = (2 * i + p) * FFT_NB
                    o_ref[g, kb, dst:dst + FFT_NB, :] = r[src:src + FFT_NB, gs]


def _fft2_kernel(b_ref, w_ref, cc_ref, cs_ref, o_ref):
    n1 = o_ref.shape[1] // FFT_NB
    for i in range(FFT_NB):
        x = jnp.dot(w_ref[...], _load_slab(b_ref, i).astype(BF16), preferred_element_type=F32).astype(BF16)
        for g in range(FNET_GROUPS):
            gs = slice(g * FNET_GROUP_DIM, (g + 1) * FNET_GROUP_DIM)
            y = (jnp.dot(x[:n1, gs], cc_ref[...], preferred_element_type=F32)
                 + jnp.dot(x[n1:, gs], cs_ref[...], preferred_element_type=F32))
            o_ref[g, pl.ds(i, n1, stride=FFT_NB), :] = y


def _fourier_mix_long(f_lat, m1, w2, w_chan, B, n):
    n2 = FFT_N2
    n1 = n // n2
    G, GD, NB = FNET_GROUPS, FNET_GROUP_DIM, FFT_NB
    bmid = pl.pallas_call(
        _fft1_kernel,
        grid=(B, n1 // NB),
        in_specs=[pl.BlockSpec((None, G, None, n2 * NB, GD), lambda b, j: (b, 0, j, 0, 0)),
                  pl.BlockSpec((NB, 2 * n2, n2), lambda b, j: (j, 0, 0))],
        out_specs=pl.BlockSpec((None, G, n2 // NB, 2 * NB * NB, GD), lambda b, j: (b, 0, 0, j, 0)),
        out_shape=jax.ShapeDtypeStruct((B, G, n2 // NB, 2 * n1 * NB, GD), F32),
        compiler_params=_cparams(("arbitrary", "arbitrary")),
        name="fnet_fft1",
    )(f_lat, m1)
    return pl.pallas_call(
        _fft2_kernel,
        grid=(B, n2 // NB),
        in_specs=[pl.BlockSpec((None, G, None, 2 * n1 * NB, GD), lambda b, j: (b, 0, j, 0, 0)),
                  pl.BlockSpec((2 * n1, 2 * n1), lambda b, j: (0, 0)),
                  pl.BlockSpec((GD, GD), lambda b, j: (0, 0)),
                  pl.BlockSpec((GD, GD), lambda b, j: (0, 1))],
        out_specs=pl.BlockSpec((None, G, None, n1 * NB, GD), lambda b, j: (b, 0, j, 0, 0)),
        out_shape=jax.ShapeDtypeStruct((B, G, n2 // NB, n1 * NB, GD), F32),
        compiler_params=_cparams(("arbitrary", "arbitrary")),
        name="fnet_fft2",
    )(bmid, w2, w_chan, w_chan)


def _fourier_mix(u, row0, B, n, a_mat, w_chan, ts, tm, tk):
    nt = n // ts
    rr = pl.pallas_call(
        _fnet_chan_kernel,
        grid=(B, nt, 2),
        in_specs=[pl.BlockSpec((ts, FNET_WIDTH), lambda b, i, p: (row0 // ts + b * nt + i, U_F // FNET_WIDTH)),
                  pl.BlockSpec((FNET_GROUP_DIM, FNET_GROUP_DIM), lambda b, i, p: (0, p))],
        out_specs=pl.BlockSpec((ts, FNET_WIDTH), lambda b, i, p: (p * nt + i, b)),
        out_shape=jax.ShapeDtypeStruct((2 * n, B * FNET_WIDTH), BF16),
        compiler_params=_cparams(("arbitrary", "arbitrary", "arbitrary")),
        name="fnet_chan",
    )(u, w_chan)
    tn = min(B * FNET_WIDTH, 2048)
    return pl.pallas_call(
        _fnet_seq_kernel,
        grid=(n // tm, (B * FNET_WIDTH) // tn, (2 * n) // tk),
        in_specs=[pl.BlockSpec((tm, tk), lambda i, j, k: (i, k)),
                  pl.BlockSpec((tk, tn), lambda i, j, k: (k, j))],
        out_specs=pl.BlockSpec((tm, tn), lambda i, j, k: (i, j)),
        out_shape=jax.ShapeDtypeStruct((n, B * FNET_WIDTH), BF16),
        scratch_shapes=[pltpu.VMEM((tm, tn), F32)],
        compiler_params=_cparams(("arbitrary", "arbitrary", "arbitrary")),
        name="fnet_seq",
    )(a_mat, rr)


def _merge_kernel(om_ref, ofl_ref, ofc_ref, on_ref, gm_ref, gf_ref, gn_ref, x_ref, mod_ref, wm_ref, wf_ref, wn_ref,
                  wo_ref, bo_ref, lg_ref, lb_ref, wr_ref, br_ref, xo_ref, h_ref, lo_ref, *, alpha, nct):
    def gate(g_ref):
        return 1.0 / (1.0 + jnp.exp(-g_ref[...].astype(F32)))

    nkb, rows = ofl_ref.shape[1], ofl_ref.shape[2]
    ofl = jnp.concatenate(
        [jnp.concatenate([ofl_ref[g, kb, k1 * FFT_NB:(k1 + 1) * FFT_NB, :]
                          for k1 in range(rows // FFT_NB) for kb in range(nkb)], axis=0)
         for g in range(FNET_GROUPS)], axis=1)
    of = jnp.where(pl.program_id(0) < nct, ofc_ref[...].astype(F32), ofl).astype(BF16)
    m = gate(gm_ref) * jnp.dot(om_ref[...], wm_ref[...], preferred_element_type=F32)
    m = m + gate(gf_ref) * jnp.dot(of, wf_ref[...], preferred_element_type=F32)
    m = m + gate(gn_ref) * jnp.dot(on_ref[...], wn_ref[...], preferred_element_type=F32)
    y = jnp.dot(m.astype(BF16), wo_ref[...], preferred_element_type=F32) + bo_ref[...]
    z = alpha * x_ref[...] + mod_ref[2:3, :] * y
    xn = _ln(z) * lg_ref[...] + lb_ref[...]
    xo_ref[...] = xn
    h = (_ln(xn) * (1.0 + mod_ref[4:5, :]) + mod_ref[3:4, :]).astype(BF16)
    h_ref[...] = h
    lo_ref[...] = jnp.dot(h, wr_ref[...], preferred_element_type=F32) + br_ref[...]


def _merge(om, of_lat, of_ctx, on, u, xall, mod, mod_row, wm, wf, wn, wo, bo, lg, lb, wr, br, alpha, B, S, NC):
    TA, D = xall.shape
    TC = B * NC
    nct = TC // TM
    npb = S // TM
    const = lambda i: (0, 0)
    row = lambda i: (i, 0)
    gcol = U_G // D

    lat = lambda i: jnp.maximum(i - nct, 0)
    outs = pl.pallas_call(
        functools.partial(_merge_kernel, alpha=alpha, nct=nct),
        grid=(TA // TM,),
        in_specs=[pl.BlockSpec((TM, MLA_HEADS * MLA_V), row),
                  pl.BlockSpec((None, FNET_GROUPS, FFT_N2 // FFT_NB, (TM // FFT_N2) * FFT_NB, FNET_GROUP_DIM),
                               lambda i: (lat(i) // npb, 0, 0, lat(i) % npb, 0)),
                  pl.BlockSpec((TM, FNET_WIDTH), lambda i: (jnp.minimum(i, nct - 1), 0)),
                  pl.BlockSpec((TM, NAT_WIDTH), row),
                  pl.BlockSpec((TM, D), lambda i: (i, gcol)),
                  pl.BlockSpec((TM, D), lambda i: (i, gcol + 1)),
                  pl.BlockSpec((TM, D), lambda i: (i, gcol + 2)),
                  pl.BlockSpec((TM, D), row),
                  pl.BlockSpec((None, 8, D), lambda i: (mod_row(i), 0, 0)),
                  pl.BlockSpec((MLA_HEADS * MLA_V, D), const),
                  pl.BlockSpec((FNET_WIDTH, D), const),
                  pl.BlockSpec((NAT_WIDTH, D), const),
                  pl.BlockSpec((D, D), const),
                  pl.BlockSpec((1, D), const),
                  pl.BlockSpec((1, D), const),
                  pl.BlockSpec((1, D), const),
                  pl.BlockSpec((D, LANES), const),
                  pl.BlockSpec((1, LANES), const)],
        out_specs=[pl.BlockSpec((TM, D), row), pl.BlockSpec((TM, D), row), pl.BlockSpec((TM, LANES), row)],
        out_shape=[jax.ShapeDtypeStruct((TA, D), F32), jax.ShapeDtypeStruct((TA, D), BF16),
                   jax.ShapeDtypeStruct((TA, LANES), F32)],
        compiler_params=_cparams(("arbitrary",)),
        name="merge",
    )(om, of_lat, of_ctx, on, u, u, u, xall, mod, wm, wf, wn, wo, bo, lg, lb, wr, br)
    return outs


def _router_kernel(lg_ref, e_ref, w_ref, r_ref, cnt_ref, carry_ref):
    lg = lg_ref[...]
    lane = lax.broadcasted_iota(jnp.int32, lg.shape, 1)
    big = jnp.int32(1 << 20)
    is_g = lane < N_GROUPS
    gl = jnp.where(is_g, lg, -jnp.inf)
    ge = jnp.exp(gl - jnp.max(gl, axis=1, keepdims=True))
    p = ge / jnp.sum(ge, axis=1, keepdims=True)
    p_top = jnp.max(p, axis=1, keepdims=True)
    g_idx = jnp.min(jnp.where(is_g & (p == p_top), lane, big), axis=1, keepdims=True)
    lo = N_GROUPS + g_idx * EXPERTS_PER_GROUP
    in_grp = (lane >= lo) & (lane < lo + EXPERTS_PER_GROUP)
    el = jnp.where(in_grp, lg, -jnp.inf)
    m1 = jnp.max(el, axis=1, keepdims=True)
    i1 = jnp.min(jnp.where(in_grp & (el == m1), lane, big), axis=1, keepdims=True)
    rest = in_grp & (lane != i1)
    el2 = jnp.where(rest, lg, -jnp.inf)
    m2 = jnp.max(el2, axis=1, keepdims=True)
    i2 = jnp.min(jnp.where(rest & (el2 == m2), lane, big), axis=1, keepdims=True)
    d = jnp.exp(m2 - m1)
    w1 = p_top * (1.0 / (1.0 + d))
    w2 = p_top * (d / (1.0 + d))
    e_ref[...] = jnp.where(lane == 0, i1 - N_GROUPS, jnp.where(lane == 1, i2 - N_GROUPS, 0))
    w_ref[...] = jnp.where(lane == 0, w1, jnp.where(lane == 1, w2, 0.0))

    @pl.when(pl.program_id(0) == 0)
    def _():
        carry_ref[...] = jnp.zeros(carry_ref.shape, F32)

    tm = lg.shape[0]
    oh1 = lane == i1
    oh2 = lane == i2
    oh = jnp.where(oh1, 1.0, jnp.where(oh2, 1.0, 0.0))
    rr = lax.broadcasted_iota(jnp.int32, (tm, tm), 0)
    cc = lax.broadcasted_iota(jnp.int32, (tm, tm), 1)
    lower = jnp.where(cc < rr, 1.0, 0.0).astype(BF16)
    before = jnp.dot(lower, oh.astype(BF16), preferred_element_type=F32) + carry_ref[0:1, :]
    r1 = jnp.sum(jnp.where(oh1, before, 0.0), axis=1, keepdims=True)
    r2 = jnp.sum(jnp.where(oh2, before, 0.0), axis=1, keepdims=True)
    r_ref[...] = jnp.where(lane == 0, r1, jnp.where(lane == 1, r2, 0.0)).astype(jnp.int32)
    carry_ref[...] = carry_ref[...] + jnp.sum(oh, axis=0, keepdims=True)
    cnt_ref[...] = carry_ref[...].astype(jnp.int32)


def _router(logits):
    TA = logits.shape[0]
    row = lambda i: (i, 0)
    return pl.pallas_call(
        _router_kernel,
        grid=(TA // TM,),
        in_specs=[pl.BlockSpec((TM, LANES), row)],
        out_specs=[pl.BlockSpec((TM, LANES), row)] * 3 + [pl.BlockSpec((8, LANES), lambda i: (0, 0))],
        out_shape=[jax.ShapeDtypeStruct((TA, LANES), jnp.int32), jax.ShapeDtypeStruct((TA, LANES), F32),
                   jax.ShapeDtypeStruct((TA, LANES), jnp.int32), jax.ShapeDtypeStruct((8, LANES), jnp.int32)],
        scratch_shapes=[pltpu.VMEM((8, LANES), F32)],
        compiler_params=_cparams(("arbitrary",)),
        name="router",
    )(logits)


def _moe_kernel(be_ref, nu_ref, x_ref, wg_ref, wu_ref, wd_ref, o_ref, wgb_ref, wub_ref, wdb_ref):
    i = pl.program_id(0)
    changed = jnp.logical_or(i == 0, be_ref[i] != be_ref[jnp.maximum(i - 1, 0)])

    @pl.when(changed)
    def _():
        wgb_ref[...] = wg_ref[...].astype(BF16)
        wub_ref[...] = wu_ref[...].astype(BF16)
        wdb_ref[...] = wd_ref[...].astype(BF16)

    @pl.when(i < nu_ref[0])
    def _():
        x = x_ref[...]
        a = jnp.dot(x, wgb_ref[...], preferred_element_type=F32)
        b = jnp.dot(x, wub_ref[...], preferred_element_type=F32)
        hmid = (a * (1.0 / (1.0 + jnp.exp(-a))) * b).astype(BF16)
        o_ref[...] = jnp.dot(hmid, wdb_ref[...], preferred_element_type=F32).astype(o_ref.dtype)

    @pl.when(i >= nu_ref[0])
    def _():
        o_ref[...] = jnp.zeros(o_ref.shape, o_ref.dtype)


def _moe_experts(xb, block_e, n_used, w_gate, w_up, w_down, layer):
    n_rows, D = xb.shape
    nb = n_rows // MOE_BM
    DE = w_gate.shape[-1]
    xrow = lambda i, be, nu: (jnp.minimum(i, nu[0] - 1), 0)
    wmap = lambda i, be, nu: (layer, be[i], 0, 0)
    grid_spec = pltpu.PrefetchScalarGridSpec(
        num_scalar_prefetch=2,
        grid=(nb,),
        in_specs=[pl.BlockSpec((MOE_BM, D), xrow),
                  pl.BlockSpec((None, None, D, DE), wmap),
                  pl.BlockSpec((None, None, D, DE), wmap),
                  pl.BlockSpec((None, None, DE, D), wmap)],
        out_specs=pl.BlockSpec((MOE_BM, D), lambda i, be, nu: (i, 0)),
        scratch_shapes=[pltpu.VMEM((D, DE), BF16), pltpu.VMEM((D, DE), BF16), pltpu.VMEM((DE, D), BF16)],
    )
    return pl.pallas_call(
        _moe_kernel,
        grid_spec=grid_spec,
        out_shape=jax.ShapeDtypeStruct((n_rows, D), BF16),
        compiler_params=_cparams(("arbitrary",)),
        name="moe_experts",
    )(block_e, n_used, xb, w_gate, w_up, w_down)


def _ffn_out_kernel(x_ref, y_ref, w_ref, mod_ref, lg_ref, lb_ref, o_ref, *, alpha):
    D = x_ref.shape[1]
    w = w_ref[...]
    f = y_ref[:, :D].astype(F32) * w[:, 0:1] + y_ref[:, D:].astype(F32) * w[:, 1:2]
    z = alpha * x_ref[...] + mod_ref[5:6, :] * f
    o_ref[...] = _ln(z) * lg_ref[...] + lb_ref[...]


def _ffn_out(xall, y2, w_tok, mod, mod_row, lg, lb, alpha):
    TA, D = xall.shape
    row = lambda i: (i, 0)
    const = lambda i: (0, 0)
    return pl.pallas_call(
        functools.partial(_ffn_out_kernel, alpha=alpha),
        grid=(TA // TM,),
        in_specs=[pl.BlockSpec((TM, D), row), pl.BlockSpec((TM, 2 * D), row),
                  pl.BlockSpec((TM, LANES), row),
                  pl.BlockSpec((None, 8, D), lambda i: (mod_row(i), 0, 0)),
                  pl.BlockSpec((1, D), const), pl.BlockSpec((1, D), const)],
        out_specs=pl.BlockSpec((TM, D), row),
        out_shape=jax.ShapeDtypeStruct((TA, D), F32),
        compiler_params=_cparams(("arbitrary",)),
        name="ffn_out",
    )(xall, y2, w_tok, mod, lg, lb)


def _reorder_w_in(w_in, b_in):
    D = w_in.shape[0]
    o_cq, o_ckv, o_kr = 0, MLA_Q_LORA, MLA_Q_LORA + MLA_KV_LORA
    o_f = o_kr + MLA_ROPE
    o_nq = o_f + FNET_WIDTH
    o_g = o_nq + 3 * NAT_WIDTH
    rot_src, rot_sign = _rope_partner()

    def build(a):
        kr = a[..., o_kr:o_kr + MLA_ROPE]
        kr_rot = kr[..., rot_src] * rot_sign
        pad = jnp.zeros(a.shape[:-1] + (LANES - 2 * MLA_ROPE,), a.dtype)
        return jnp.concatenate([a[..., o_f:o_g], a[..., o_g:], a[..., o_cq:o_kr], kr, kr_rot, pad], axis=-1)

    return build(w_in).astype(BF16), build(b_in[None, :])


def _rope_partner():
    q = MLA_ROPE // 4
    src = np.zeros(MLA_ROPE, np.int32)
    sign = np.zeros(MLA_ROPE, np.float32)
    for half in range(2):
        o = half * 2 * q
        for i in range(q):
            src[o + i], sign[o + i] = o + q + i, -1.0
            src[o + q + i], sign[o + q + i] = o + i, 1.0
    return src, sign


def _pad_heads(w, width, offset=0):
    K = w.shape[0]
    w = w.reshape(K, MLA_HEADS, width)
    out = jnp.zeros((K, MLA_HEADS, HEAD_PAD), w.dtype)
    return out.at[:, :, offset:offset + width].set(w).reshape(K, MLA_HEADS * HEAD_PAD)


def _mla_weights(w_uq, w_uk, w_uv):
    rot_src, rot_sign = _rope_partner()
    dq = MLA_NOPE + MLA_ROPE
    wq = w_uq.reshape(-1, MLA_HEADS, dq)
    wq_pe_rot = wq[:, :, MLA_NOPE:][:, :, rot_src] * rot_sign
    wqa = _pad_heads(wq.reshape(-1, MLA_HEADS * dq), dq)
    wqb = _pad_heads(wq_pe_rot.reshape(-1, MLA_HEADS * MLA_ROPE), MLA_ROPE, MLA_NOPE)
    wk = _pad_heads(w_uk, MLA_NOPE)
    K = w_uv.shape[0]
    wv3 = w_uv.reshape(K, MLA_HEADS // 2, 2, MLA_V)
    wv = jnp.zeros((K, MLA_HEADS // 2, 2, HEAD_PAD), w_uv.dtype)
    wv = wv.at[:, :, 0, :MLA_V].set(wv3[:, :, 0]).at[:, :, 1, MLA_V:].set(wv3[:, :, 1])
    wv = wv.reshape(K, MLA_HEADS * HEAD_PAD)
    vone = np.zeros((MLA_HEADS // 2, 2, HEAD_PAD), np.float32)
    vone[:, 0, MLA_V] = 1.0
    vone[:, 1, 0] = 1.0
    pa = np.zeros((LANES, MLA_HEADS, HEAD_PAD), np.float32)
    pb = np.zeros((LANES, MLA_HEADS, HEAD_PAD), np.float32)
    for i in range(MLA_ROPE):
        pa[i, :, MLA_NOPE + i] = 1.0
        pb[MLA_ROPE + i, :, MLA_NOPE + i] = 1.0
    W = MLA_HEADS * HEAD_PAD
    return (wqa.astype(BF16), wqb.astype(BF16), wk.astype(BF16), wv.astype(BF16),
            jnp.asarray(pa.reshape(LANES, W), BF16), jnp.asarray(pb.reshape(LANES, W), BF16),
            jnp.asarray(vone.reshape(1, W)))


def _rope_lane_tables(S):
    t = jnp.arange(S)
    rows = (t // GRID_W).astype(F32)
    cols = (t % GRID_W).astype(F32)
    n_freq = MLA_ROPE // 4
    inv = jnp.power(ROPE_BASE, -jnp.arange(n_freq, dtype=F32) / n_freq)
    ar = rows[:, None] * inv[None, :]
    ac = cols[:, None] * inv[None, :]
    ones = jnp.ones((S, MLA_NOPE), F32)
    zpad = jnp.zeros((S, HEAD_PAD - MLA_NOPE - MLA_ROPE), F32)
    cos = jnp.concatenate([ones, jnp.cos(ar), jnp.cos(ar), jnp.cos(ac), jnp.cos(ac), zpad], axis=1)
    sin = jnp.concatenate([0 * ones, jnp.sin(ar), jnp.sin(ar), jnp.sin(ac), jnp.sin(ac), zpad], axis=1)
    idc = jnp.concatenate([jnp.ones((TM, MLA_NOPE + MLA_ROPE), F32), jnp.zeros((TM, HEAD_PAD - MLA_NOPE - MLA_ROPE), F32)], axis=1)
    return jnp.concatenate([idc, cos], axis=0), jnp.concatenate([jnp.zeros((TM, HEAD_PAD), F32), sin], axis=0)


def _moe_plan(e_ids, ranks, counts, n_blocks):
    t = e_ids.shape[0]
    a = 2 * t
    starts = jnp.cumsum(counts) - counts
    padded = (counts + MOE_BM - 1) // MOE_BM * MOE_BM
    pends = jnp.cumsum(padded)
    pstarts = pends - padded
    eo = jnp.arange(N_EXPERTS, dtype=jnp.int32)
    pos = ranks + jnp.sum(jnp.where(e_ids[:, :, None] == eo, pstarts, 0), axis=-1)
    blk0 = jnp.arange(n_blocks, dtype=jnp.int32) * MOE_BM
    block_e = jnp.minimum(jnp.sum((pends[None, :] <= blk0[:, None]).astype(jnp.int32), axis=1), N_EXPERTS - 1)
    n_used = (pends[-1] // MOE_BM).astype(jnp.int32).reshape(1)
    _, order = lax.sort((e_ids.reshape(a), jnp.arange(a, dtype=jnp.int32)), num_keys=1, is_stable=True)
    off = jnp.arange(MOE_BM, dtype=jnp.int32)[None, :] + (blk0 - pstarts[block_e])[:, None]
    src = jnp.where(off < counts[block_e][:, None], off + starts[block_e][:, None], 0)
    row_tok = jnp.take(order, src.reshape(n_blocks * MOE_BM), mode="clip") // 2
    return row_tok, pos, block_e, n_used


def kernel(x, c, ctx, c_ctx, w_ada, b_ada, w_in, b_in, q_norm_g, kv_norm_g, w_uq, w_uk, w_uv, rpb, w_o_mla, w_o_fnet, w_o_nat, w_out, b_out, ln1_g, ln1_b, w_rg, b_rg, w_re, b_re, w_gate_e, w_up_e, w_down_e, ln2_g, ln2_b):
    B, S, D = x.shape
    NC = ctx.shape[1]
    L = w_ada.shape[0]
    TC, TL = B * NC, B * S
    TA = TC + TL
    assert TC % TM == 0 and S % TM == 0 and NC % 128 == 0 and B < 8
    alpha = float((2 * L) ** 0.25)
    nct, npb = TC // TM, S // TM

    def mod_row(i):
        return jnp.where(i < nct, B, jnp.maximum(i - nct, 0) // npb)

    def tab_row(i):
        return jnp.where(i < nct, 0, 1 + jnp.maximum(i - nct, 0) % npb)

    cc = jnp.concatenate([c, c_ctx[None, :], jnp.zeros((8 - B - 1, D), F32)], axis=0)
    mod_all = _ada_all(cc, w_ada, b_ada).reshape(L, 8, 6, D)
    mod_all = jnp.concatenate([mod_all, jnp.zeros((L, 8, 2, D), F32)], axis=2)

    tabc, tabs = _rope_lane_tables(S)
    chan_scale = FNET_GROUP_DIM ** -0.5
    cch, sch = _dft_tables(FNET_GROUP_DIM, chan_scale)
    w_chan = jnp.concatenate([cch, sch], axis=1).astype(BF16)
    assert S % (FFT_N2 * FFT_NB) == 0
    m1, w2 = _fft_tables(S)
    a_ctx = _seq_dft_matrix(NC)
    nat_bias = _nat_bias_table(rpb, S // GRID_W)
    qscale = float((MLA_NOPE + MLA_ROPE) ** -0.5) * LOG2E
    rows_n = S // GRID_W
    n_blocks = (2 * TA) // MOE_BM + N_EXPERTS
    tq = min(512, S)
    tk = math.gcd(1024, math.gcd(S, TC))

    xall = jnp.concatenate([ctx.reshape(TC, D), x.reshape(TL, D)], axis=0)
    for l in range(L):
        mod = mod_all[l]
        w_in_r, b_in_r = _reorder_w_in(w_in[l], b_in[l])
        u = _in_proj(xall, mod, w_in_r, b_in_r, mod_row)

        wqa, wqb, wk, wv, pa, pb, vone = _mla_weights(w_uq[l], w_uk[l], w_uv[l])
        q, k, v = _qkv(u, tabc, tabs, tab_row, q_norm_g[l][None, :], kv_norm_g[l][None, :],
                       wqa, wqb, wk, wv, pa, pb, vone, qscale)
        om_ctx, om_lat = _mla_attention(q, k, v, B, S, NC, tq, tk)

        on_ctx, on_lat = _nat_attention(u, nat_bias, l, B, S, NC)

        n1 = S // FFT_N2
        f_lat = u[TC:, U_F:U_F + FNET_WIDTH].astype(F32).reshape(B, FFT_N2, n1 // FFT_NB, FFT_NB, FNET_GROUPS,
                                                                   FNET_GROUP_DIM)
        f_lat = f_lat.transpose(0, 4, 2, 1, 3, 5).reshape(B, FNET_GROUPS, n1 // FFT_NB, FFT_N2 * FFT_NB, FNET_GROUP_DIM)
        of_lat = _fourier_mix_long(f_lat, m1, w2, w_chan, B, S)
        of_ctx = _fourier_mix(u, 0, B, NC, a_ctx, w_chan, ts=NC, tm=NC, tk=2 * NC)
        of_ctx = of_ctx.reshape(NC, B, FNET_WIDTH).transpose(1, 0, 2).reshape(TC, FNET_WIDTH)

        om = jnp.concatenate([om_ctx, om_lat], axis=0)
        on = jnp.concatenate([on_ctx, on_lat], axis=0)
        w_r = jnp.concatenate([w_rg[l], w_re[l], jnp.zeros((D, LANES - N_GROUPS - N_EXPERTS), F32)], axis=1).astype(BF16)
        b_r = jnp.concatenate([b_rg[l], b_re[l], jnp.zeros((LANES - N_GROUPS - N_EXPERTS,), F32)])[None, :]
        xall, h_f, logits = _merge(om, of_lat, of_ctx, on, u, xall, mod, mod_row,
                                   w_o_mla[l].astype(BF16), w_o_fnet[l].astype(BF16), w_o_nat[l].astype(BF16),
                                   w_out[l].astype(BF16), b_out[l][None, :], ln1_g[l][None, :], ln1_b[l][None, :],
                                   w_r, b_r, alpha, B, S, NC)

        e_out, w_tok, r_out, cnt = _router(logits)
        counts = cnt[0, N_GROUPS:N_GROUPS + N_EXPERTS]
        row_tok, pos, block_e, n_used = _moe_plan(e_out[:, :2], r_out[:, :2], counts, n_blocks)
        xb = jnp.take(h_f, row_tok, axis=0, mode="clip")
        yb = _moe_experts(xb, block_e, n_used, w_gate_e, w_up_e, w_down_e, l)
        y2 = jnp.take(yb, pos.reshape(2 * TA), axis=0, mode="clip").reshape(TA, 2 * D)
        xall = _ffn_out(xall, y2, w_tok, mod, mod_row, ln2_g[l][None, :], ln2_b[l][None, :], alpha)
    return xall[TC:].reshape(B, S, D)
```

```python
import functools
import math

import numpy as np
import jax
import jax.numpy as jnp
from jax import lax
from jax.experimental import pallas as pl
from jax.experimental.pallas import tpu as pltpu

F32 = jnp.float32
BF16 = jnp.bfloat16

GRID_W = 64
MLA_HEADS = 8
MLA_Q_LORA = 256
MLA_KV_LORA = 128
MLA_NOPE = 64
MLA_ROPE = 32
MLA_V = 64
ROPE_BASE = 10000.0
FNET_GROUPS = 4
FNET_GROUP_DIM = 128
FNET_WIDTH = FNET_GROUPS * FNET_GROUP_DIM
NAT_HEADS = 8
NAT_HEAD_DIM = 64
NAT_WIDTH = NAT_HEADS * NAT_HEAD_DIM
NAT_WIN_H = 8
NAT_WIN_W = 16
N_GROUPS = 4
EXPERTS_PER_GROUP = 8
N_EXPERTS = N_GROUPS * EXPERTS_PER_GROUP
D_EXPERT = 512
LN_EPS = 1e-6

LANES = 128
HEAD_PAD = 128
LOG2E = 1.4426950408889634
NEG_BIG = -1e30
VMEM_LIMIT = 56 * 1024 * 1024

U_F, U_NQ, U_NK, U_NV, U_G, U_CQ, U_CKV, U_KR, U_W = 0, 512, 1024, 1536, 2048, 5120, 5376, 5504, 5632

TM = 512
NAT_ROWS = 4
NAT_KROWS = 12
MOE_BM = 256


def _cparams(sem):
    return pltpu.CompilerParams(dimension_semantics=sem, vmem_limit_bytes=VMEM_LIMIT)


def _ln(x):
    mu = jnp.mean(x, axis=-1, keepdims=True)
    xc = x - mu
    var = jnp.mean(xc * xc, axis=-1, keepdims=True)
    return xc * lax.rsqrt(var + LN_EPS)


def _ada_kernel(c_ref, w_ref, b_ref, o_ref):
    c = c_ref[...]
    a = c * (1.0 / (1.0 + jnp.exp(-c)))
    o_ref[...] = jnp.dot(a, w_ref[...], preferred_element_type=F32,
                         precision=lax.Precision.HIGHEST) + b_ref[...]


def _ada_all(cc, w_ada, b_ada):
    L, D, N = w_ada.shape
    tn = 1536
    return pl.pallas_call(
        _ada_kernel,
        grid=(L, N // tn),
        in_specs=[pl.BlockSpec((8, D), lambda l, j: (0, 0)),
                  pl.BlockSpec((None, D, tn), lambda l, j: (l, 0, j)),
                  pl.BlockSpec((None, 1, tn), lambda l, j: (l, 0, j))],
        out_specs=pl.BlockSpec((None, 8, tn), lambda l, j: (l, 0, j)),
        out_shape=jax.ShapeDtypeStruct((L, 8, N), F32),
        compiler_params=_cparams(("arbitrary", "arbitrary")),
        name="ada_mod",
    )(cc, w_ada, b_ada.reshape(L, 1, N))


def _in_proj_kernel(x_ref, mod_ref, w_ref, b_ref, o_ref, *, cw):
    y = _ln(x_ref[...])
    h = (y * (1.0 + mod_ref[1:2, :]) + mod_ref[0:1, :]).astype(BF16)
    for j in range(o_ref.shape[1] // cw):
        sl = slice(j * cw, (j + 1) * cw)
        acc = jnp.dot(h, w_ref[:, sl], preferred_element_type=F32)
        o_ref[:, sl] = (acc + b_ref[:, sl]).astype(o_ref.dtype)


def _in_proj(xall, mod, w_in_r, b_in_r, mod_row):
    TA, D = xall.shape
    N = w_in_r.shape[1]
    return pl.pallas_call(
        functools.partial(_in_proj_kernel, cw=512),
        grid=(TA // TM,),
        in_specs=[pl.BlockSpec((TM, D), lambda i: (i, 0)),
                  pl.BlockSpec((None, 8, D), lambda i: (mod_row(i), 0, 0)),
                  pl.BlockSpec((D, N), lambda i: (0, 0)),
                  pl.BlockSpec((1, N), lambda i: (0, 0))],
        out_specs=pl.BlockSpec((TM, N), lambda i: (i, 0)),
        out_shape=jax.ShapeDtypeStruct((TA, N), BF16),
        compiler_params=_cparams(("arbitrary",)),
        name="in_proj",
    )(xall, mod, w_in_r, b_in_r)


def _rms(x, g):
    return x * lax.rsqrt(jnp.mean(x * x, axis=-1, keepdims=True) + LN_EPS) * g


def _qkv_kernel(cq_ref, ckv_ref, kr_ref, tc_ref, ts_ref, qg_ref, kvg_ref, wqa_ref, wqb_ref,
                wk_ref, wv_ref, pa_ref, pb_ref, vone_ref, q_ref, k_ref, v_ref, *, qscale):
    cqn = _rms(cq_ref[...].astype(F32), qg_ref[...]).astype(BF16)
    ckvn = _rms(ckv_ref[...].astype(F32), kvg_ref[...]).astype(BF16)
    cos = jnp.concatenate([tc_ref[...]] * MLA_HEADS, axis=1)
    sin = jnp.concatenate([ts_ref[...]] * MLA_HEADS, axis=1)
    qa = jnp.dot(cqn, wqa_ref[...], preferred_element_type=F32)
    qb = jnp.dot(cqn, wqb_ref[...], preferred_element_type=F32)
    q_ref[...] = ((qa * cos + qb * sin) * qscale).astype(q_ref.dtype)
    kr = kr_ref[...]
    ka = jnp.dot(kr, pa_ref[...], preferred_element_type=F32)
    kb = jnp.dot(kr, pb_ref[...], preferred_element_type=F32)
    kn = jnp.dot(ckvn, wk_ref[...], preferred_element_type=F32)
    k_ref[...] = (kn + ka * cos + kb * sin).astype(k_ref.dtype)
    v = jnp.dot(ckvn, wv_ref[...], preferred_element_type=F32) + vone_ref[...]
    v_ref[...] = v.astype(v_ref.dtype)


def _qkv(u, tabc, tabs, tab_row, qg, kvg, wqa, wqb, wk, wv, pa, pb, vone, qscale):
    TA = u.shape[0]
    W = MLA_HEADS * HEAD_PAD
    const = lambda i: (0, 0)
    out = jax.ShapeDtypeStruct((TA, W), BF16)
    return pl.pallas_call(
        functools.partial(_qkv_kernel, qscale=qscale),
        grid=(TA // TM,),
        in_specs=[pl.BlockSpec((TM, MLA_Q_LORA), lambda i: (i, U_CQ // MLA_Q_LORA)),
                  pl.BlockSpec((TM, MLA_KV_LORA), lambda i: (i, U_CKV // MLA_KV_LORA)),
                  pl.BlockSpec((TM, LANES), lambda i: (i, U_KR // LANES)),
                  pl.BlockSpec((TM, LANES), lambda i: (tab_row(i), 0)),
                  pl.BlockSpec((TM, LANES), lambda i: (tab_row(i), 0)),
                  pl.BlockSpec((1, MLA_Q_LORA), const),
                  pl.BlockSpec((1, MLA_KV_LORA), const),
                  pl.BlockSpec((MLA_Q_LORA, W), const),
                  pl.BlockSpec((MLA_Q_LORA, W), const),
                  pl.BlockSpec((MLA_KV_LORA, W), const),
                  pl.BlockSpec((MLA_KV_LORA, W), const),
                  pl.BlockSpec((LANES, W), const),
                  pl.BlockSpec((LANES, W), const),
                  pl.BlockSpec((1, W), const)],
        out_specs=[pl.BlockSpec((TM, W), lambda i: (i, 0))] * 3,
        out_shape=[out, out, out],
        compiler_params=_cparams(("arbitrary",)),
        name="mla_qkv",
    )(u, u, u, tabc, tabs, qg, kvg, wqa, wqb, wk, wv, pa, pb, vone)


def _mla_attend(q_ref, k_ref, v_ref, acc_ref, m_ref):
    tk = k_ref.shape[0]
    for h in range(MLA_HEADS):
        hs = slice(h * HEAD_PAD, (h + 1) * HEAD_PAD)
        s = lax.dot_general(q_ref[:, hs], k_ref[:, hs], (((1,), (1,)), ((), ())),
                            preferred_element_type=F32)
        m_prev = m_ref[h]
        m_new = jnp.maximum(m_prev, jnp.max(s, axis=1, keepdims=True))
        alpha = jnp.exp2(m_prev - m_new)
        p = jnp.exp2((s - jnp.concatenate([m_new] * (tk // LANES), axis=1)).astype(BF16))
        pv = jnp.dot(p, v_ref[:, hs], preferred_element_type=F32)
        acc_ref[h] = acc_ref[h] * alpha + pv
        m_ref[h] = m_new


def _mla_finish(o_ref, acc_ref):
    lane = lax.broadcasted_iota(jnp.int32, acc_ref.shape[1:], 1)
    for hp in range(MLA_HEADS // 2):
        a0 = acc_ref[2 * hp]
        a1 = acc_ref[2 * hp + 1]
        l0 = jnp.sum(jnp.where(lane == MLA_V, a0, 0.0), axis=1, keepdims=True)
        l1 = jnp.sum(jnp.where(lane == 0, a1, 0.0), axis=1, keepdims=True)
        o = jnp.where(lane < MLA_V, a0 / l0, a1 / l1)
        o_ref[:, hp * LANES:(hp + 1) * LANES] = o.astype(o_ref.dtype)


def _mla_lat_kernel(q_ref, kc_ref, vc_ref, kl_ref, vl_ref, o_ref, acc_ref, m_ref):
    j = pl.program_id(2)

    @pl.when(j == 0)
    def _():
        m_ref[...] = jnp.full(m_ref.shape, NEG_BIG, F32)
        acc_ref[...] = jnp.zeros(acc_ref.shape, F32)
        _mla_attend(q_ref, kc_ref, vc_ref, acc_ref, m_ref)

    _mla_attend(q_ref, kl_ref, vl_ref, acc_ref, m_ref)

    @pl.when(j == pl.num_programs(2) - 1)
    def _():
        _mla_finish(o_ref, acc_ref)


def _mla_ctx_kernel(q_ref, kc_ref, vc_ref, o_ref, acc_ref, m_ref):
    m_ref[...] = jnp.full(m_ref.shape, NEG_BIG, F32)
    acc_ref[...] = jnp.zeros(acc_ref.shape, F32)
    _mla_attend(q_ref, kc_ref, vc_ref, acc_ref, m_ref)
    _mla_finish(o_ref, acc_ref)


def _mla_attention(q, k, v, B, S, NC, tq, tk):
    TA, W = q.shape
    TC = B * NC
    OW = MLA_HEADS * MLA_V
    nq, nk = S // tq, S // tk
    scratch = lambda t: [pltpu.VMEM((MLA_HEADS, t, HEAD_PAD), F32), pltpu.VMEM((MLA_HEADS, t, HEAD_PAD), F32)]
    o_lat = pl.pallas_call(
        _mla_lat_kernel,
        grid=(B, nq, nk),
        in_specs=[pl.BlockSpec((tq, W), lambda b, i, j: (TC // tq + b * nq + i, 0)),
                  pl.BlockSpec((NC, W), lambda b, i, j: (b, 0)),
                  pl.BlockSpec((NC, W), lambda b, i, j: (b, 0)),
                  pl.BlockSpec((tk, W), lambda b, i, j: (TC // tk + b * nk + j, 0)),
                  pl.BlockSpec((tk, W), lambda b, i, j: (TC // tk + b * nk + j, 0))],
        out_specs=pl.BlockSpec((tq, OW), lambda b, i, j: (b * nq + i, 0)),
        out_shape=jax.ShapeDtypeStruct((B * S, OW), BF16),
        scratch_shapes=scratch(tq),
        compiler_params=_cparams(("arbitrary", "arbitrary", "arbitrary")),
        name="mla_attn_lat",
    )(q, k, v, k, v)
    o_ctx = pl.pallas_call(
        _mla_ctx_kernel,
        grid=(B,),
        in_specs=[pl.BlockSpec((NC, W), lambda b: (b, 0))] * 3,
        out_specs=pl.BlockSpec((NC, OW), lambda b: (b, 0)),
        out_shape=jax.ShapeDtypeStruct((TC, OW), BF16),
        scratch_shapes=scratch(NC),
        compiler_params=_cparams(("arbitrary",)),
        name="mla_attn_ctx",
    )(q, k, v)
    return o_ctx, o_lat


def _nat_softmax_pv(scores, values):
    m = scores[0].max(axis=1, keepdims=True)
    for s in scores[1:]:
        m = jnp.maximum(m, s.max(axis=1, keepdims=True))
    l = None
    o = None
    for s, v in zip(scores, values):
        p = jnp.exp2(s - m)
        ls = jnp.sum(p, axis=1, keepdims=True)
        pv = jnp.dot(p.astype(BF16), v, preferred_element_type=F32)
        l = ls if l is None else l + ls
        o = pv if o is None else o + pv
    return o / l


def _nat_lat_kernel(q_ref, k0_ref, k1_ref, k2_ref, v0_ref, v1_ref, v2_ref, kc_ref, vc_ref, bias_ref, o_ref,
                    *, qscale):
    tq = q_ref.shape[0]
    ck = k0_ref.shape[0]
    lane = lax.broadcasted_iota(jnp.int32, (tq, LANES), 1)
    dn = (((1,), (1,)), ((), ()))
    for hp in range(NAT_HEADS // 2):
        ls = slice(hp * LANES, (hp + 1) * LANES)
        qp = q_ref[:, ls].astype(F32) * qscale
        keys = [r[:, ls] for r in (k0_ref, k1_ref, k2_ref, kc_ref)]
        vals = [r[:, ls] for r in (v0_ref, v1_ref, v2_ref, vc_ref)]
        outs = []
        for hh in range(2):
            mask = (lane < NAT_HEAD_DIM) if hh == 0 else (lane >= NAT_HEAD_DIM)
            qm = jnp.where(mask, qp, 0.0).astype(BF16)
            scores = []
            for c, kk in enumerate(keys):
                s = lax.dot_general(qm, kk, dn, preferred_element_type=F32)
                if c < 3:
                    s = s + bias_ref[2 * hp + hh, :, c * ck:(c + 1) * ck].astype(F32)
                scores.append(s)
            outs.append(_nat_softmax_pv(scores, vals))
        o_ref[:, ls] = jnp.where(lane < NAT_HEAD_DIM, outs[0], outs[1]).astype(o_ref.dtype)


def _nat_ctx_kernel(q_ref, kc_ref, vc_ref, o_ref, *, qscale):
    tq = q_ref.shape[0]
    lane = lax.broadcasted_iota(jnp.int32, (tq, LANES), 1)
    dn = (((1,), (1,)), ((), ()))
    for hp in range(NAT_HEADS // 2):
        ls = slice(hp * LANES, (hp + 1) * LANES)
        qp = q_ref[:, ls].astype(F32) * qscale
        outs = []
        for hh in range(2):
            mask = (lane < NAT_HEAD_DIM) if hh == 0 else (lane >= NAT_HEAD_DIM)
            qm = jnp.where(mask, qp, 0.0).astype(BF16)
            s = lax.dot_general(qm, kc_ref[:, ls], dn, preferred_element_type=F32)
            outs.append(_nat_softmax_pv([s], [vc_ref[:, ls]]))
        o_ref[:, ls] = jnp.where(lane < NAT_HEAD_DIM, outs[0], outs[1]).astype(o_ref.dtype)


def _nat_bias_table(rpb, rows_n):
    L = rpb.shape[0]
    qc = np.arange(GRID_W)
    cs = np.clip(qc - NAT_WIN_W // 2, 0, GRID_W - NAT_WIN_W)
    kc = np.arange(GRID_W)
    colvalid = (kc[None, :] >= cs[:, None]) & (kc[None, :] < cs[:, None] + NAT_WIN_W)
    dc = np.clip(kc[None, :] - qc[:, None] + NAT_WIN_W - 1, 0, 2 * NAT_WIN_W - 2)
    bt = jnp.where(colvalid, rpb[:, :, :, dc] * LOG2E, NEG_BIG)
    n_dr = 2 * NAT_WIN_H - 1
    bt = jnp.concatenate([bt, jnp.full((L, NAT_HEADS, 1, GRID_W, GRID_W), NEG_BIG, F32)], axis=2).astype(BF16)
    kh = NAT_WIN_H
    dr = np.full((3, NAT_ROWS, NAT_KROWS), n_dr, np.int32)
    for v, r0 in enumerate((0, 2 * NAT_ROWS, rows_n - NAT_ROWS)):
        kstart = int(np.clip(r0 - kh // 2, 0, rows_n - NAT_KROWS))
        for i in range(NAT_ROWS):
            r = r0 + i
            rs = int(np.clip(r - kh // 2, 0, rows_n - kh))
            for jj in range(NAT_KROWS):
                kr = kstart + jj
                if rs <= kr < rs + kh:
                    dr[v, i, jj] = kr - r + NAT_WIN_H - 1
    t = bt[:, :, dr]
    t = t.transpose(0, 2, 1, 3, 5, 4, 6)
    return t.reshape(L, 3, NAT_HEADS, NAT_ROWS * GRID_W, NAT_KROWS * GRID_W)


def _nat_attention(u, bias, layer, B, S, NC):
    TA = u.shape[0]
    TC = B * NC
    rows_n = S // GRID_W
    nblk = rows_n // NAT_ROWS
    tq = NAT_ROWS * GRID_W
    ck = tq
    qscale = float(NAT_HEAD_DIM ** -0.5) * LOG2E
    cq, ckk, cv = U_NQ // NAT_WIDTH, U_NK // NAT_WIDTH, U_NV // NAT_WIDTH
    base = lambda b: TC // tq + b * nblk

    def kstart_blk(i):
        return jnp.clip(i * NAT_ROWS - NAT_WIN_H // 2, 0, rows_n - NAT_KROWS) // NAT_ROWS

    def kv_spec(col, c):
        return pl.BlockSpec((ck, NAT_WIDTH), lambda b, i: (base(b) + kstart_blk(i) + c, col))

    def variant(i):
        return jnp.where(i == 0, 0, jnp.where(i == nblk - 1, 2, 1))

    o_lat = pl.pallas_call(
        functools.partial(_nat_lat_kernel, qscale=qscale),
        grid=(B, nblk),
        in_specs=[pl.BlockSpec((tq, NAT_WIDTH), lambda b, i: (base(b) + i, cq)),
                  kv_spec(ckk, 0), kv_spec(ckk, 1), kv_spec(ckk, 2),
                  kv_spec(cv, 0), kv_spec(cv, 1), kv_spec(cv, 2),
                  pl.BlockSpec((NC, NAT_WIDTH), lambda b, i: (b, ckk)),
                  pl.BlockSpec((NC, NAT_WIDTH), lambda b, i: (b, cv)),
                  pl.BlockSpec((None, None, NAT_HEADS, tq, NAT_KROWS * GRID_W),
                               lambda b, i: (layer, variant(i), 0, 0, 0))],
        out_specs=pl.BlockSpec((tq, NAT_WIDTH), lambda b, i: (b * nblk + i, 0)),
        out_shape=jax.ShapeDtypeStruct((B * S, NAT_WIDTH), BF16),
        compiler_params=_cparams(("arbitrary", "arbitrary")),
        name="nat_attn_lat",
    )(u, u, u, u, u, u, u, u, u, bias)
    o_ctx = pl.pallas_call(
        functools.partial(_nat_ctx_kernel, qscale=qscale),
        grid=(B,),
        in_specs=[pl.BlockSpec((NC, NAT_WIDTH), lambda b: (b, cq)),
                  pl.BlockSpec((NC, NAT_WIDTH), lambda b: (b, ckk)),
                  pl.BlockSpec((NC, NAT_WIDTH), lambda b: (b, cv))],
        out_specs=pl.BlockSpec((NC, NAT_WIDTH), lambda b: (b, 0)),
        out_shape=jax.ShapeDtypeStruct((TC, NAT_WIDTH), BF16),
        compiler_params=_cparams(("arbitrary",)),
        name="nat_attn_ctx",
    )(u, u, u)
    return o_ctx, o_lat


def _fnet_chan_kernel(f_ref, w_ref, o_ref):
    for g in range(FNET_GROUPS):
        gs = slice(g * FNET_GROUP_DIM, (g + 1) * FNET_GROUP_DIM)
        o_ref[:, gs] = jnp.dot(f_ref[:, gs], w_ref[...], preferred_element_type=F32).astype(o_ref.dtype)


def _fnet_seq_kernel(a_ref, r_ref, o_ref, acc_ref):
    k = pl.program_id(2)

    @pl.when(k == 0)
    def _():
        acc_ref[...] = jnp.zeros(acc_ref.shape, F32)

    acc_ref[...] += jnp.dot(a_ref[...], r_ref[...], preferred_element_type=F32)

    @pl.when(k == pl.num_programs(2) - 1)
    def _():
        o_ref[...] = acc_ref[...].astype(o_ref.dtype)


def _dft_tables(n, scale):
    j = jnp.arange(n, dtype=jnp.int32)
    idx = (j[:, None] * j[None, :]) & (n - 1)
    ang = idx.astype(F32) * (2.0 * math.pi / n)
    return jnp.cos(ang) * scale, jnp.sin(ang) * scale


def _seq_dft_matrix(n):
    c, s = _dft_tables(n, n ** -0.5)
    return jnp.concatenate([c, -s], axis=1).astype(BF16)


FFT_N2 = 128
FFT_NB = 16


def _fft_tables(n):
    n2 = FFT_N2
    n1 = n // n2
    j1 = jnp.arange(n1, dtype=jnp.int32)[:, None, None]
    k2 = jnp.arange(n2, dtype=jnp.int32)[None, :, None]
    j2 = jnp.arange(n2, dtype=jnp.int32)[None, None, :]
    ang = ((k2 * (j1 + n1 * j2)) & (n - 1)).astype(F32) * (2.0 * math.pi / n)
    s1 = n2 ** -0.5
    m1 = jnp.concatenate([jnp.cos(ang) * s1, -jnp.sin(ang) * s1], axis=1).astype(BF16)
    c1, sn1 = _dft_tables(n1, n1 ** -0.5)
    top = jnp.stack([c1, sn1], axis=2).reshape(n1, 2 * n1)
    bot = jnp.stack([-sn1, c1], axis=2).reshape(n1, 2 * n1)
    w2 = jnp.concatenate([top, bot], axis=0).astype(BF16)
    return m1, w2


def _load_slab(ref, i):
    rows = ref.shape[1] // FFT_NB
    parts = [ref[g, pl.ds(i, rows, stride=FFT_NB), :] for g in range(FNET_GROUPS)]
    return jnp.concatenate(parts, axis=1)


def _fft1_kernel(f_ref, m_ref, o_ref):
    n2 = m_ref.shape[2]
    for i in range(FFT_NB):
        r = jnp.dot(m_ref[i], _load_slab(f_ref, i).astype(BF16), preferred_element_type=F32)
        for g in range(FNET_GROUPS):
            gs = slice(g * FNET_GROUP_DIM, (g + 1) * FNET_GROUP_DIM)
            for p in range(2):
                for kb in range(n2 // FFT_NB):
                    src = p * n2 + kb * FFT_NB
                    dst = (2 * i + p) * FFT_NB
                    o_ref[g, kb, dst:dst + FFT_NB, :] = r[src:src + FFT_NB, gs]


def _fft2_kernel(b_ref, w_ref, cc_ref, cs_ref, o_ref):
    n1 = o_ref.shape[1] // FFT_NB
    for i in range(FFT_NB):
        x = jnp.dot(w_ref[...], _load_slab(b_ref, i).astype(BF16), preferred_element_type=F32).astype(BF16)
        for g in range(FNET_GROUPS):
            gs = slice(g * FNET_GROUP_DIM, (g + 1) * FNET_GROUP_DIM)
            y = (jnp.dot(x[:n1, gs], cc_ref[...], preferred_element_type=F32)
                 + jnp.dot(x[n1:, gs], cs_ref[...], preferred_element_type=F32))
            o_ref[g, pl.ds(i, n1, stride=FFT_NB), :] = y


def _fourier_mix_long(f_lat, m1, w2, w_chan, B, n):
    n2 = FFT_N2
    n1 = n // n2
    G, GD, NB = FNET_GROUPS, FNET_GROUP_DIM, FFT_NB
    bmid = pl.pallas_call(
        _fft1_kernel,
        grid=(B, n1 // NB),
        in_specs=[pl.BlockSpec((None, G, None, n2 * NB, GD), lambda b, j: (b, 0, j, 0, 0)),
                  pl.BlockSpec((NB, 2 * n2, n2), lambda b, j: (j, 0, 0))],
        out_specs=pl.BlockSpec((None, G, n2 // NB, 2 * NB * NB, GD), lambda b, j: (b, 0, 0, j, 0)),
        out_shape=jax.ShapeDtypeStruct((B, G, n2 // NB, 2 * n1 * NB, GD), F32),
        compiler_params=_cparams(("arbitrary", "arbitrary")),
        name="fnet_fft1",
    )(f_lat, m1)
    return pl.pallas_call(
        _fft2_kernel,
        grid=(B, n2 // NB),
        in_specs=[pl.BlockSpec((None, G, None, 2 * n1 * NB, GD), lambda b, j: (b, 0, j, 0, 0)),
                  pl.BlockSpec((2 * n1, 2 * n1), lambda b, j: (0, 0)),
                  pl.BlockSpec((GD, GD), lambda b, j: (0, 0)),
                  pl.BlockSpec((GD, GD), lambda b, j: (0, 1))],
        out_specs=pl.BlockSpec((None, G, None, n1 * NB, GD), lambda b, j: (b, 0, j, 0, 0)),
        out_shape=jax.ShapeDtypeStruct((B, G, n2 // NB, n1 * NB, GD), F32),
        compiler_params=_cparams(("arbitrary", "arbitrary")),
        name="fnet_fft2",
    )(bmid, w2, w_chan, w_chan)


def _fourier_mix(u, row0, B, n, a_mat, w_chan, ts, tm, tk):
    nt = n // ts
    rr = pl.pallas_call(
        _fnet_chan_kernel,
        grid=(B, nt, 2),
        in_specs=[pl.BlockSpec((ts, FNET_WIDTH), lambda b, i, p: (row0 // ts + b * nt + i, U_F // FNET_WIDTH)),
                  pl.BlockSpec((FNET_GROUP_DIM, FNET_GROUP_DIM), lambda b, i, p: (0, p))],
        out_specs=pl.BlockSpec((ts, FNET_WIDTH), lambda b, i, p: (p * nt + i, b)),
        out_shape=jax.ShapeDtypeStruct((2 * n, B * FNET_WIDTH), BF16),
        compiler_params=_cparams(("arbitrary", "arbitrary", "arbitrary")),
        name="fnet_chan",
    )(u, w_chan)
    tn = min(B * FNET_WIDTH, 2048)
    return pl.pallas_call(
        _fnet_seq_kernel,
        grid=(n // tm, (B * FNET_WIDTH) // tn, (2 * n) // tk),
        in_specs=[pl.BlockSpec((tm, tk), lambda i, j, k: (i, k)),
                  pl.BlockSpec((tk, tn), lambda i, j, k: (k, j))],
        out_specs=pl.BlockSpec((tm, tn), lambda i, j, k: (i, j)),
        out_shape=jax.ShapeDtypeStruct((n, B * FNET_WIDTH), BF16),
        scratch_shapes=[pltpu.VMEM((tm, tn), F32)],
        compiler_params=_cparams(("arbitrary", "arbitrary", "arbitrary")),
        name="fnet_seq",
    )(a_mat, rr)


def _merge_kernel(oml_ref, omc_ref, ofl_ref, ofc_ref, onl_ref, onc_ref, gm_ref, gf_ref, gn_ref, x_ref, mod_ref,
                  wm_ref, wf_ref, wn_ref, wo_ref, bo_ref, lg_ref, lb_ref, wr_ref, br_ref, xo_ref, h_ref, lo_ref,
                  *, alpha, nct):
    def gate(g_ref):
        return 1.0 / (1.0 + jnp.exp(-g_ref[...].astype(F32)))

    is_ctx = pl.program_id(0) < nct
    nkb, rows = ofl_ref.shape[1], ofl_ref.shape[2]
    ofl = jnp.concatenate(
        [jnp.concatenate([ofl_ref[g, kb, k1 * FFT_NB:(k1 + 1) * FFT_NB, :]
                          for k1 in range(rows // FFT_NB) for kb in range(nkb)], axis=0)
         for g in range(FNET_GROUPS)], axis=1)
    of = jnp.where(is_ctx, ofc_ref[...].astype(F32), ofl).astype(BF16)
    om = jnp.where(is_ctx, omc_ref[...], oml_ref[...])
    on = jnp.where(is_ctx, onc_ref[...], onl_ref[...])
    m = gate(gm_ref) * jnp.dot(om, wm_ref[...], preferred_element_type=F32)
    m = m + gate(gf_ref) * jnp.dot(of, wf_ref[...], preferred_element_type=F32)
    m = m + gate(gn_ref) * jnp.dot(on, wn_ref[...], preferred_element_type=F32)
    y = jnp.dot(m.astype(BF16), wo_ref[...], preferred_element_type=F32) + bo_ref[...]
    z = alpha * x_ref[...] + mod_ref[2:3, :] * y
    xn = _ln(z) * lg_ref[...] + lb_ref[...]
    xo_ref[...] = xn
    h = (_ln(xn) * (1.0 + mod_ref[4:5, :]) + mod_ref[3:4, :]).astype(BF16)
    h_ref[...] = h
    lo_ref[...] = jnp.dot(h, wr_ref[...], preferred_element_type=F32) + br_ref[...]


def _merge(om_lat, om_ctx, of_lat, of_ctx, on_lat, on_ctx, u, xall, mod, mod_row, wm, wf, wn, wo, bo, lg, lb, wr, br,
           alpha, B, S, NC):
    TA, D = xall.shape
    TC = B * NC
    nct = TC // TM
    npb = S // TM
    const = lambda i: (0, 0)
    row = lambda i: (i, 0)
    gcol = U_G // D

    lat = lambda i: jnp.maximum(i - nct, 0)
    lat_row = lambda i: (lat(i), 0)
    ctx_row = lambda i: (jnp.minimum(i, nct - 1), 0)
    outs = pl.pallas_call(
        functools.partial(_merge_kernel, alpha=alpha, nct=nct),
        grid=(TA // TM,),
        in_specs=[pl.BlockSpec((TM, MLA_HEADS * MLA_V), lat_row),
                  pl.BlockSpec((TM, MLA_HEADS * MLA_V), ctx_row),
                  pl.BlockSpec((None, FNET_GROUPS, FFT_N2 // FFT_NB, (TM // FFT_N2) * FFT_NB, FNET_GROUP_DIM),
                               lambda i: (lat(i) // npb, 0, 0, lat(i) % npb, 0)),
                  pl.BlockSpec((TM, FNET_WIDTH), ctx_row),
                  pl.BlockSpec((TM, NAT_WIDTH), lat_row),
                  pl.BlockSpec((TM, NAT_WIDTH), ctx_row),
                  pl.BlockSpec((TM, D), lambda i: (i, gcol)),
                  pl.BlockSpec((TM, D), lambda i: (i, gcol + 1)),
                  pl.BlockSpec((TM, D), lambda i: (i, gcol + 2)),
                  pl.BlockSpec((TM, D), row),
                  pl.BlockSpec((None, 8, D), lambda i: (mod_row(i), 0, 0)),
                  pl.BlockSpec((MLA_HEADS * MLA_V, D), const),
                  pl.BlockSpec((FNET_WIDTH, D), const),
                  pl.BlockSpec((NAT_WIDTH, D), const),
                  pl.BlockSpec((D, D), const),
                  pl.BlockSpec((1, D), const),
                  pl.BlockSpec((1, D), const),
                  pl.BlockSpec((1, D), const),
                  pl.BlockSpec((D, LANES), const),
                  pl.BlockSpec((1, LANES), const)],
        out_specs=[pl.BlockSpec((TM, D), row), pl.BlockSpec((TM, D), row), pl.BlockSpec((TM, LANES), row)],
        out_shape=[jax.ShapeDtypeStruct((TA, D), F32), jax.ShapeDtypeStruct((TA, D), BF16),
                   jax.ShapeDtypeStruct((TA, LANES), F32)],
        compiler_params=_cparams(("arbitrary",)),
        name="merge",
    )(om_lat, om_ctx, of_lat, of_ctx, on_lat, on_ctx, u, u, u, xall, mod, wm, wf, wn, wo, bo, lg, lb, wr, br)
    return outs


def _router_kernel(lg_ref, e_ref, w_ref, r_ref, cnt_ref, carry_ref):
    lg = lg_ref[...]
    lane = lax.broadcasted_iota(jnp.int32, lg.shape, 1)
    big = jnp.int32(1 << 20)
    is_g = lane < N_GROUPS
    gl = jnp.where(is_g, lg, -jnp.inf)
    ge = jnp.exp(gl - jnp.max(gl, axis=1, keepdims=True))
    p = ge / jnp.sum(ge, axis=1, keepdims=True)
    p_top = jnp.max(p, axis=1, keepdims=True)
    g_idx = jnp.min(jnp.where(is_g & (p == p_top), lane, big), axis=1, keepdims=True)
    lo = N_GROUPS + g_idx * EXPERTS_PER_GROUP
    in_grp = (lane >= lo) & (lane < lo + EXPERTS_PER_GROUP)
    el = jnp.where(in_grp, lg, -jnp.inf)
    m1 = jnp.max(el, axis=1, keepdims=True)
    i1 = jnp.min(jnp.where(in_grp & (el == m1), lane, big), axis=1, keepdims=True)
    rest = in_grp & (lane != i1)
    el2 = jnp.where(rest, lg, -jnp.inf)
    m2 = jnp.max(el2, axis=1, keepdims=True)
    i2 = jnp.min(jnp.where(rest & (el2 == m2), lane, big), axis=1, keepdims=True)
    d = jnp.exp(m2 - m1)
    w1 = p_top * (1.0 / (1.0 + d))
    w2 = p_top * (d / (1.0 + d))
    e_ref[...] = jnp.where(lane == 0, i1 - N_GROUPS, jnp.where(lane == 1, i2 - N_GROUPS, 0))
    w_ref[...] = jnp.where(lane == 0, w1, jnp.where(lane == 1, w2, 0.0))

    @pl.when(pl.program_id(0) == 0)
    def _():
        carry_ref[...] = jnp.zeros(carry_ref.shape, F32)

    tm = lg.shape[0]
    oh1 = lane == i1
    oh2 = lane == i2
    oh = jnp.where(oh1, 1.0, jnp.where(oh2, 1.0, 0.0))
    rr = lax.broadcasted_iota(jnp.int32, (tm, tm), 0)
    cc = lax.broadcasted_iota(jnp.int32, (tm, tm), 1)
    lower = jnp.where(cc < rr, 1.0, 0.0).astype(BF16)
    before = jnp.dot(lower, oh.astype(BF16), preferred_element_type=F32) + carry_ref[0:1, :]
    r1 = jnp.sum(jnp.where(oh1, before, 0.0), axis=1, keepdims=True)
    r2 = jnp.sum(jnp.where(oh2, before, 0.0), axis=1, keepdims=True)
    r_ref[...] = jnp.where(lane == 0, r1, jnp.where(lane == 1, r2, 0.0)).astype(jnp.int32)
    carry_ref[...] = carry_ref[...] + jnp.sum(oh, axis=0, keepdims=True)
    cnt_ref[...] = carry_ref[...].astype(jnp.int32)


def _router(logits):
    TA = logits.shape[0]
    row = lambda i: (i, 0)
    return pl.pallas_call(
        _router_kernel,
        grid=(TA // TM,),
        in_specs=[pl.BlockSpec((TM, LANES), row)],
        out_specs=[pl.BlockSpec((TM, LANES), row)] * 3 + [pl.BlockSpec((8, LANES), lambda i: (0, 0))],
        out_shape=[jax.ShapeDtypeStruct((TA, LANES), jnp.int32), jax.ShapeDtypeStruct((TA, LANES), F32),
                   jax.ShapeDtypeStruct((TA, LANES), jnp.int32), jax.ShapeDtypeStruct((8, LANES), jnp.int32)],
        scratch_shapes=[pltpu.VMEM((8, LANES), F32)],
        compiler_params=_cparams(("arbitrary",)),
        name="router",
    )(logits)


def _moe_kernel(be_ref, nu_ref, x_ref, wg_ref, wu_ref, wd_ref, o_ref, wgb_ref, wub_ref, wdb_ref):
    i = pl.program_id(0)
    changed = jnp.logical_or(i == 0, be_ref[i] != be_ref[jnp.maximum(i - 1, 0)])

    @pl.when(changed)
    def _():
        wgb_ref[...] = wg_ref[...].astype(BF16)
        wub_ref[...] = wu_ref[...].astype(BF16)
        wdb_ref[...] = wd_ref[...].astype(BF16)

    @pl.when(i < nu_ref[0])
    def _():
        x = x_ref[...]
        a = jnp.dot(x, wgb_ref[...], preferred_element_type=F32)
        b = jnp.dot(x, wub_ref[...], preferred_element_type=F32)
        hmid = (a * (1.0 / (1.0 + jnp.exp(-a))) * b).astype(BF16)
        o_ref[...] = jnp.dot(hmid, wdb_ref[...], preferred_element_type=F32).astype(o_ref.dtype)

    @pl.when(i >= nu_ref[0])
    def _():
        o_ref[...] = jnp.zeros(o_ref.shape, o_ref.dtype)


def _moe_experts(xb, block_e, n_used, w_gate, w_up, w_down, layer):
    n_rows, D = xb.shape
    nb = n_rows // MOE_BM
    DE = w_gate.shape[-1]
    xrow = lambda i, be, nu: (jnp.minimum(i, nu[0] - 1), 0)
    wmap = lambda i, be, nu: (layer, be[i], 0, 0)
    grid_spec = pltpu.PrefetchScalarGridSpec(
        num_scalar_prefetch=2,
        grid=(nb,),
        in_specs=[pl.BlockSpec((MOE_BM, D), xrow),
                  pl.BlockSpec((None, None, D, DE), wmap),
                  pl.BlockSpec((None, None, D, DE), wmap),
                  pl.BlockSpec((None, None, DE, D), wmap)],
        out_specs=pl.BlockSpec((MOE_BM, D), lambda i, be, nu: (i, 0)),
        scratch_shapes=[pltpu.VMEM((D, DE), BF16), pltpu.VMEM((D, DE), BF16), pltpu.VMEM((DE, D), BF16)],
    )
    return pl.pallas_call(
        _moe_kernel,
        grid_spec=grid_spec,
        out_shape=jax.ShapeDtypeStruct((n_rows, D), BF16),
        compiler_params=_cparams(("arbitrary",)),
        name="moe_experts",
    )(block_e, n_used, xb, w_gate, w_up, w_down)


def _ffn_out_kernel(x_ref, y0_ref, y1_ref, w_ref, mod_ref, lg_ref, lb_ref, o_ref, *, alpha):
    w = w_ref[...]
    f = y0_ref[...].astype(F32) * w[:, 0:1] + y1_ref[...].astype(F32) * w[:, 1:2]
    z = alpha * x_ref[...] + mod_ref[5:6, :] * f
    o_ref[...] = _ln(z) * lg_ref[...] + lb_ref[...]


def _ffn_out(xall, y2, w_tok, mod, mod_row, lg, lb, alpha):
    TA, D = xall.shape
    nt = TA // TM
    row = lambda i: (i, 0)
    const = lambda i: (0, 0)
    return pl.pallas_call(
        functools.partial(_ffn_out_kernel, alpha=alpha),
        grid=(nt,),
        in_specs=[pl.BlockSpec((TM, D), row), pl.BlockSpec((TM, D), row), pl.BlockSpec((TM, D), lambda i: (nt + i, 0)),
                  pl.BlockSpec((TM, LANES), row),
                  pl.BlockSpec((None, 8, D), lambda i: (mod_row(i), 0, 0)),
                  pl.BlockSpec((1, D), const), pl.BlockSpec((1, D), const)],
        out_specs=pl.BlockSpec((TM, D), row),
        out_shape=jax.ShapeDtypeStruct((TA, D), F32),
        compiler_params=_cparams(("arbitrary",)),
        name="ffn_out",
    )(xall, y2, y2, w_tok, mod, lg, lb)


def _reorder_w_in(w_in, b_in):
    D = w_in.shape[0]
    o_cq, o_ckv, o_kr = 0, MLA_Q_LORA, MLA_Q_LORA + MLA_KV_LORA
    o_f = o_kr + MLA_ROPE
    o_nq = o_f + FNET_WIDTH
    o_g = o_nq + 3 * NAT_WIDTH
    rot_src, rot_sign = _rope_partner()

    def build(a):
        kr = a[..., o_kr:o_kr + MLA_ROPE]
        kr_rot = kr[..., rot_src] * rot_sign
        pad = jnp.zeros(a.shape[:-1] + (LANES - 2 * MLA_ROPE,), a.dtype)
        return jnp.concatenate([a[..., o_f:o_g], a[..., o_g:], a[..., o_cq:o_kr], kr, kr_rot, pad], axis=-1)

    return build(w_in).astype(BF16), build(b_in[None, :])


def _rope_partner():
    q = MLA_ROPE // 4
    src = np.zeros(MLA_ROPE, np.int32)
    sign = np.zeros(MLA_ROPE, np.float32)
    for half in range(2):
        o = half * 2 * q
        for i in range(q):
            src[o + i], sign[o + i] = o + q + i, -1.0
            src[o + q + i], sign[o + q + i] = o + i, 1.0
    return src, sign


def _pad_heads(w, width, offset=0):
    K = w.shape[0]
    w = w.reshape(K, MLA_HEADS, width)
    out = jnp.zeros((K, MLA_HEADS, HEAD_PAD), w.dtype)
    return out.at[:, :, offset:offset + width].set(w).reshape(K, MLA_HEADS * HEAD_PAD)


def _mla_weights(w_uq, w_uk, w_uv):
    rot_src, rot_sign = _rope_partner()
    dq = MLA_NOPE + MLA_ROPE
    wq = w_uq.reshape(-1, MLA_HEADS, dq)
    wq_pe_rot = wq[:, :, MLA_NOPE:][:, :, rot_src] * rot_sign
    wqa = _pad_heads(wq.reshape(-1, MLA_HEADS * dq), dq)
    wqb = _pad_heads(wq_pe_rot.reshape(-1, MLA_HEADS * MLA_ROPE), MLA_ROPE, MLA_NOPE)
    wk = _pad_heads(w_uk, MLA_NOPE)
    K = w_uv.shape[0]
    wv3 = w_uv.reshape(K, MLA_HEADS // 2, 2, MLA_V)
    wv = jnp.zeros((K, MLA_HEADS // 2, 2, HEAD_PAD), w_uv.dtype)
    wv = wv.at[:, :, 0, :MLA_V].set(wv3[:, :, 0]).at[:, :, 1, MLA_V:].set(wv3[:, :, 1])
    wv = wv.reshape(K, MLA_HEADS * HEAD_PAD)
    vone = np.zeros((MLA_HEADS // 2, 2, HEAD_PAD), np.float32)
    vone[:, 0, MLA_V] = 1.0
    vone[:, 1, 0] = 1.0
    pa = np.zeros((LANES, MLA_HEADS, HEAD_PAD), np.float32)
    pb = np.zeros((LANES, MLA_HEADS, HEAD_PAD), np.float32)
    for i in range(MLA_ROPE):
        pa[i, :, MLA_NOPE + i] = 1.0
        pb[MLA_ROPE + i, :, MLA_NOPE + i] = 1.0
    W = MLA_HEADS * HEAD_PAD
    return (wqa.astype(BF16), wqb.astype(BF16), wk.astype(BF16), wv.astype(BF16),
            jnp.asarray(pa.reshape(LANES, W), BF16), jnp.asarray(pb.reshape(LANES, W), BF16),
            jnp.asarray(vone.reshape(1, W)))


def _rope_lane_tables(S):
    t = jnp.arange(S)
    rows = (t // GRID_W).astype(F32)
    cols = (t % GRID_W).astype(F32)
    n_freq = MLA_ROPE // 4
    inv = jnp.power(ROPE_BASE, -jnp.arange(n_freq, dtype=F32) / n_freq)
    ar = rows[:, None] * inv[None, :]
    ac = cols[:, None] * inv[None, :]
    ones = jnp.ones((S, MLA_NOPE), F32)
    zpad = jnp.zeros((S, HEAD_PAD - MLA_NOPE - MLA_ROPE), F32)
    cos = jnp.concatenate([ones, jnp.cos(ar), jnp.cos(ar), jnp.cos(ac), jnp.cos(ac), zpad], axis=1)
    sin = jnp.concatenate([0 * ones, jnp.sin(ar), jnp.sin(ar), jnp.sin(ac), jnp.sin(ac), zpad], axis=1)
    idc = jnp.concatenate([jnp.ones((TM, MLA_NOPE + MLA_ROPE), F32), jnp.zeros((TM, HEAD_PAD - MLA_NOPE - MLA_ROPE), F32)], axis=1)
    return jnp.concatenate([idc, cos], axis=0), jnp.concatenate([jnp.zeros((TM, HEAD_PAD), F32), sin], axis=0)


def _moe_plan(e_ids, ranks, counts, n_blocks):
    t = e_ids.shape[0]
    a = 2 * t
    padded = (counts + MOE_BM - 1) // MOE_BM * MOE_BM
    pends = jnp.cumsum(padded)
    pstarts = pends - padded
    eo = jnp.arange(N_EXPERTS, dtype=jnp.int32)
    pos = ranks + jnp.sum(jnp.where(e_ids[:, :, None] == eo, pstarts, 0), axis=-1)
    blk0 = jnp.arange(n_blocks, dtype=jnp.int32) * MOE_BM
    block_e = jnp.minimum(jnp.sum((pends[None, :] <= blk0[:, None]).astype(jnp.int32), axis=1), N_EXPERTS - 1)
    n_used = (pends[-1] // MOE_BM).astype(jnp.int32).reshape(1)
    nd = n_blocks * MOE_BM - a
    dcum = jnp.cumsum(padded - counts)
    d_e = jnp.sum((dcum[None, :] <= jnp.arange(nd, dtype=jnp.int32)[:, None]).astype(jnp.int32), axis=1)
    keys = jnp.concatenate([e_ids.reshape(a) * 2, d_e * 2 + 1])
    payload = jnp.concatenate([jnp.arange(a, dtype=jnp.int32), jnp.zeros((nd,), jnp.int32)])
    _, row_asg = lax.sort((keys, payload), num_keys=1, is_stable=True)
    return row_asg // 2, pos, block_e, n_used


def kernel(x, c, ctx, c_ctx, w_ada, b_ada, w_in, b_in, q_norm_g, kv_norm_g, w_uq, w_uk, w_uv, rpb, w_o_mla, w_o_fnet, w_o_nat, w_out, b_out, ln1_g, ln1_b, w_rg, b_rg, w_re, b_re, w_gate_e, w_up_e, w_down_e, ln2_g, ln2_b):
    B, S, D = x.shape
    NC = ctx.shape[1]
    L = w_ada.shape[0]
    TC, TL = B * NC, B * S
    TA = TC + TL
    assert TC % TM == 0 and S % TM == 0 and NC % 128 == 0 and B < 8
    alpha = float((2 * L) ** 0.25)
    nct, npb = TC // TM, S // TM

    def mod_row(i):
        return jnp.where(i < nct, B, jnp.maximum(i - nct, 0) // npb)

    def tab_row(i):
        return jnp.where(i < nct, 0, 1 + jnp.maximum(i - nct, 0) % npb)

    cc = jnp.concatenate([c, c_ctx[None, :], jnp.zeros((8 - B - 1, D), F32)], axis=0)
    mod_all = _ada_all(cc, w_ada, b_ada).reshape(L, 8, 6, D)
    mod_all = jnp.concatenate([mod_all, jnp.zeros((L, 8, 2, D), F32)], axis=2)

    tabc, tabs = _rope_lane_tables(S)
    chan_scale = FNET_GROUP_DIM ** -0.5
    cch, sch = _dft_tables(FNET_GROUP_DIM, chan_scale)
    w_chan = jnp.concatenate([cch, sch], axis=1).astype(BF16)
    assert S % (FFT_N2 * FFT_NB) == 0
    m1, w2 = _fft_tables(S)
    a_ctx = _seq_dft_matrix(NC)
    nat_bias = _nat_bias_table(rpb, S // GRID_W)
    qscale = float((MLA_NOPE + MLA_ROPE) ** -0.5) * LOG2E
    rows_n = S // GRID_W
    n_blocks = (2 * TA) // MOE_BM + N_EXPERTS
    tq = math.gcd(1024, math.gcd(S, TC))
    tk = tq

    xall = jnp.concatenate([ctx.reshape(TC, D), x.reshape(TL, D)], axis=0)
    for l in range(L):
        mod = mod_all[l]
        w_in_r, b_in_r = _reorder_w_in(w_in[l], b_in[l])
        u = _in_proj(xall, mod, w_in_r, b_in_r, mod_row)

        wqa, wqb, wk, wv, pa, pb, vone = _mla_weights(w_uq[l], w_uk[l], w_uv[l])
        q, k, v = _qkv(u, tabc, tabs, tab_row, q_norm_g[l][None, :], kv_norm_g[l][None, :],
                       wqa, wqb, wk, wv, pa, pb, vone, qscale)
        om_ctx, om_lat = _mla_attention(q, k, v, B, S, NC, tq, tk)

        on_ctx, on_lat = _nat_attention(u, nat_bias, l, B, S, NC)

        n1 = S // FFT_N2
        f_lat = u[TC:, U_F:U_F + FNET_WIDTH].astype(F32).reshape(B, FFT_N2, n1 // FFT_NB, FFT_NB, FNET_GROUPS,
                                                                   FNET_GROUP_DIM)
        f_lat = f_lat.transpose(0, 4, 2, 1, 3, 5).reshape(B, FNET_GROUPS, n1 // FFT_NB, FFT_N2 * FFT_NB, FNET_GROUP_DIM)
        of_lat = _fourier_mix_long(f_lat, m1, w2, w_chan, B, S)
        of_ctx = _fourier_mix(u, 0, B, NC, a_ctx, w_chan, ts=NC, tm=NC, tk=2 * NC)
        of_ctx = of_ctx.reshape(NC, B, FNET_WIDTH).transpose(1, 0, 2).reshape(TC, FNET_WIDTH)

        w_r = jnp.concatenate([w_rg[l], w_re[l], jnp.zeros((D, LANES - N_GROUPS - N_EXPERTS), F32)], axis=1).astype(BF16)
        b_r = jnp.concatenate([b_rg[l], b_re[l], jnp.zeros((LANES - N_GROUPS - N_EXPERTS,), F32)])[None, :]
        xall, h_f, logits = _merge(om_lat, om_ctx, of_lat, of_ctx, on_lat, on_ctx, u, xall, mod, mod_row,
                                   w_o_mla[l].astype(BF16), w_o_fnet[l].astype(BF16), w_o_nat[l].astype(BF16),
                                   w_out[l].astype(BF16), b_out[l][None, :], ln1_g[l][None, :], ln1_b[l][None, :],
                                   w_r, b_r, alpha, B, S, NC)

        e_out, w_tok, r_out, cnt = _router(logits)
        counts = cnt[0, N_GROUPS:N_GROUPS + N_EXPERTS]
        row_tok, pos, block_e, n_used = _moe_plan(e_out[:, :2], r_out[:, :2], counts, n_blocks)
        xb = jnp.take(h_f, row_tok, axis=0, mode="clip")
        yb = _moe_experts(xb, block_e, n_used, w_gate_e, w_up_e, w_down_e, l)
        y2 = jnp.take(yb, jnp.concatenate([pos[:, 0], pos[:, 1]]), axis=0, mode="clip")
        xall = _ffn_out(xall, y2, w_tok, mod, mod_row, ln2_g[l][None, :], ln2_b[l][None, :], alpha)
    return xall[TC:].reshape(B, S, D)
```

```python
import functools
import math

import numpy as np
import jax
import jax.numpy as jnp
from jax import lax
from jax.experimental import pallas as pl
from jax.experimental.pallas import tpu as pltpu

F32 = jnp.float32
BF16 = jnp.bfloat16

GRID_W = 64
MLA_HEADS = 8
MLA_Q_LORA = 256
MLA_KV_LORA = 128
MLA_NOPE = 64
MLA_ROPE = 32
MLA_V = 64
ROPE_BASE = 10000.0
FNET_GROUPS = 4
FNET_GROUP_DIM = 128
FNET_WIDTH = FNET_GROUPS * FNET_GROUP_DIM
NAT_HEADS = 8
NAT_HEAD_DIM = 64
NAT_WIDTH = NAT_HEADS * NAT_HEAD_DIM
NAT_WIN_H = 8
NAT_WIN_W = 16
N_GROUPS = 4
EXPERTS_PER_GROUP = 8
N_EXPERTS = N_GROUPS * EXPERTS_PER_GROUP
D_EXPERT = 512
LN_EPS = 1e-6

LANES = 128
HEAD_PAD = 128
LOG2E = 1.4426950408889634
NEG_BIG = -1e30
VMEM_LIMIT = 56 * 1024 * 1024

U_F, U_NQ, U_NK, U_NV, U_G, U_CQ, U_CKV, U_KR, U_W = 0, 512, 1024, 1536, 2048, 5120, 5376, 5504, 5632

TM = 512
NAT_ROWS = 4
NAT_KROWS = 12
MOE_BM = 512


def _cparams(sem):
    return pltpu.CompilerParams(dimension_semantics=sem, vmem_limit_bytes=VMEM_LIMIT)


def _ln(x):
    mu = jnp.mean(x, axis=-1, keepdims=True)
    xc = x - mu
    var = jnp.mean(xc * xc, axis=-1, keepdims=True)
    return xc * lax.rsqrt(var + LN_EPS)


def _ada_kernel(c_ref, w_ref, b_ref, o_ref):
    c = c_ref[...]
    a = c * (1.0 / (1.0 + jnp.exp(-c)))
    o_ref[...] = jnp.dot(a, w_ref[...], preferred_element_type=F32,
                         precision=lax.Precision.HIGHEST) + b_ref[...]


def _ada_all(cc, w_ada, b_ada):
    L, D, N = w_ada.shape
    tn = 1536
    return pl.pallas_call(
        _ada_kernel,
        grid=(L, N // tn),
        in_specs=[pl.BlockSpec((8, D), lambda l, j: (0, 0)),
                  pl.BlockSpec((None, D, tn), lambda l, j: (l, 0, j)),
                  pl.BlockSpec((None, 1, tn), lambda l, j: (l, 0, j))],
        out_specs=pl.BlockSpec((None, 8, tn), lambda l, j: (l, 0, j)),
        out_shape=jax.ShapeDtypeStruct((L, 8, N), F32),
        compiler_params=_cparams(("arbitrary", "arbitrary")),
        name="ada_mod",
    )(cc, w_ada, b_ada.reshape(L, 1, N))


def _in_proj_kernel(x_ref, mod_ref, w_ref, b_ref, o_ref, *, cw):
    y = _ln(x_ref[...])
    h = (y * (1.0 + mod_ref[1:2, :]) + mod_ref[0:1, :]).astype(BF16)
    for j in range(o_ref.shape[1] // cw):
        sl = slice(j * cw, (j + 1) * cw)
        acc = jnp.dot(h, w_ref[:, sl], preferred_element_type=F32)
        o_ref[:, sl] = (acc + b_ref[:, sl]).astype(o_ref.dtype)


def _in_proj(xall, mod, w_in_r, b_in_r, mod_row):
    TA, D = xall.shape
    N = w_in_r.shape[1]
    return pl.pallas_call(
        functools.partial(_in_proj_kernel, cw=512),
        grid=(TA // TM,),
        in_specs=[pl.BlockSpec((TM, D), lambda i: (i, 0)),
                  pl.BlockSpec((None, 8, D), lambda i: (mod_row(i), 0, 0)),
                  pl.BlockSpec((D, N), lambda i: (0, 0)),
                  pl.BlockSpec((1, N), lambda i: (0, 0))],
        out_specs=pl.BlockSpec((TM, N), lambda i: (i, 0)),
        out_shape=jax.ShapeDtypeStruct((TA, N), BF16),
        compiler_params=_cparams(("arbitrary",)),
        name="in_proj",
    )(xall, mod, w_in_r, b_in_r)


def _rms(x, g):
    return x * lax.rsqrt(jnp.mean(x * x, axis=-1, keepdims=True) + LN_EPS) * g


def _qkv_kernel(cq_ref, ckv_ref, kr_ref, tc_ref, ts_ref, qg_ref, kvg_ref, wqa_ref, wqb_ref,
                wk_ref, wv_ref, pa_ref, pb_ref, vone_ref, q_ref, k_ref, v_ref, *, qscale):
    cqn = _rms(cq_ref[...].astype(F32), qg_ref[...]).astype(BF16)
    ckvn = _rms(ckv_ref[...].astype(F32), kvg_ref[...]).astype(BF16)
    cos = jnp.concatenate([tc_ref[...]] * MLA_HEADS, axis=1)
    sin = jnp.concatenate([ts_ref[...]] * MLA_HEADS, axis=1)
    qa = jnp.dot(cqn, wqa_ref[...], preferred_element_type=F32)
    qb = jnp.dot(cqn, wqb_ref[...], preferred_element_type=F32)
    q_ref[...] = ((qa * cos + qb * sin) * qscale).astype(q_ref.dtype)
    kr = kr_ref[...]
    ka = jnp.dot(kr, pa_ref[...], preferred_element_type=F32)
    kb = jnp.dot(kr, pb_ref[...], preferred_element_type=F32)
    kn = jnp.dot(ckvn, wk_ref[...], preferred_element_type=F32)
    k_ref[...] = (kn + ka * cos + kb * sin).astype(k_ref.dtype)
    v = jnp.dot(ckvn, wv_ref[...], preferred_element_type=F32) + vone_ref[...]
    v_ref[...] = v.astype(v_ref.dtype)


def _qkv(u, tabc, tabs, tab_row, qg, kvg, wqa, wqb, wk, wv, pa, pb, vone, qscale):
    TA = u.shape[0]
    W = MLA_HEADS * HEAD_PAD
    const = lambda i: (0, 0)
    out = jax.ShapeDtypeStruct((TA, W), BF16)
    return pl.pallas_call(
        functools.partial(_qkv_kernel, qscale=qscale),
        grid=(TA // TM,),
        in_specs=[pl.BlockSpec((TM, MLA_Q_LORA), lambda i: (i, U_CQ // MLA_Q_LORA)),
                  pl.BlockSpec((TM, MLA_KV_LORA), lambda i: (i, U_CKV // MLA_KV_LORA)),
                  pl.BlockSpec((TM, LANES), lambda i: (i, U_KR // LANES)),
                  pl.BlockSpec((TM, LANES), lambda i: (tab_row(i), 0)),
                  pl.BlockSpec((TM, LANES), lambda i: (tab_row(i), 0)),
                  pl.BlockSpec((1, MLA_Q_LORA), const),
                  pl.BlockSpec((1, MLA_KV_LORA), const),
                  pl.BlockSpec((MLA_Q_LORA, W), const),
                  pl.BlockSpec((MLA_Q_LORA, W), const),
                  pl.BlockSpec((MLA_KV_LORA, W), const),
                  pl.BlockSpec((MLA_KV_LORA, W), const),
                  pl.BlockSpec((LANES, W), const),
                  pl.BlockSpec((LANES, W), const),
                  pl.BlockSpec((1, W), const)],
        out_specs=[pl.BlockSpec((TM, W), lambda i: (i, 0))] * 3,
        out_shape=[out, out, out],
        compiler_params=_cparams(("arbitrary",)),
        name="mla_qkv",
    )(u, u, u, tabc, tabs, qg, kvg, wqa, wqb, wk, wv, pa, pb, vone)


def _mla_attend(q_ref, k_ref, v_ref, acc_ref, m_ref):
    tk = k_ref.shape[0]
    for h in range(MLA_HEADS):
        hs = slice(h * HEAD_PAD, (h + 1) * HEAD_PAD)
        s = lax.dot_general(q_ref[:, hs], k_ref[:, hs], (((1,), (1,)), ((), ())),
                            preferred_element_type=F32)
        m_prev = m_ref[h]
        m_new = jnp.maximum(m_prev, jnp.max(s, axis=1, keepdims=True))
        alpha = jnp.exp2(m_prev - m_new)
        p = jnp.exp2((s - jnp.concatenate([m_new] * (tk // LANES), axis=1)).astype(BF16))
        pv = jnp.dot(p, v_ref[:, hs], preferred_element_type=F32)
        acc_ref[h] = acc_ref[h] * alpha + pv
        m_ref[h] = m_new


def _mla_finish(o_ref, acc_ref):
    lane = lax.broadcasted_iota(jnp.int32, acc_ref.shape[1:], 1)
    for hp in range(MLA_HEADS // 2):
        a0 = acc_ref[2 * hp]
        a1 = acc_ref[2 * hp + 1]
        l0 = jnp.sum(jnp.where(lane == MLA_V, a0, 0.0), axis=1, keepdims=True)
        l1 = jnp.sum(jnp.where(lane == 0, a1, 0.0), axis=1, keepdims=True)
        o = jnp.where(lane < MLA_V, a0 / l0, a1 / l1)
        o_ref[:, hp * LANES:(hp + 1) * LANES] = o.astype(o_ref.dtype)


def _mla_lat_kernel(q_ref, kc_ref, vc_ref, kl_ref, vl_ref, o_ref, acc_ref, m_ref):
    j = pl.program_id(2)

    @pl.when(j == 0)
    def _():
        m_ref[...] = jnp.full(m_ref.shape, NEG_BIG, F32)
        acc_ref[...] = jnp.zeros(acc_ref.shape, F32)
        _mla_attend(q_ref, kc_ref, vc_ref, acc_ref, m_ref)

    _mla_attend(q_ref, kl_ref, vl_ref, acc_ref, m_ref)

    @pl.when(j == pl.num_programs(2) - 1)
    def _():
        _mla_finish(o_ref, acc_ref)


def _mla_ctx_kernel(q_ref, kc_ref, vc_ref, o_ref, acc_ref, m_ref):
    m_ref[...] = jnp.full(m_ref.shape, NEG_BIG, F32)
    acc_ref[...] = jnp.zeros(acc_ref.shape, F32)
    _mla_attend(q_ref, kc_ref, vc_ref, acc_ref, m_ref)
    _mla_finish(o_ref, acc_ref)


def _mla_attention(q, k, v, B, S, NC, tq, tk):
    TA, W = q.shape
    TC = B * NC
    OW = MLA_HEADS * MLA_V
    nq, nk = S // tq, S // tk
    scratch = lambda t: [pltpu.VMEM((MLA_HEADS, t, HEAD_PAD), F32), pltpu.VMEM((MLA_HEADS, t, HEAD_PAD), F32)]
    o_lat = pl.pallas_call(
        _mla_lat_kernel,
        grid=(B, nq, nk),
        in_specs=[pl.BlockSpec((tq, W), lambda b, i, j: (TC // tq + b * nq + i, 0)),
                  pl.BlockSpec((NC, W), lambda b, i, j: (b, 0)),
                  pl.BlockSpec((NC, W), lambda b, i, j: (b, 0)),
                  pl.BlockSpec((tk, W), lambda b, i, j: (TC // tk + b * nk + j, 0)),
                  pl.BlockSpec((tk, W), lambda b, i, j: (TC // tk + b * nk + j, 0))],
        out_specs=pl.BlockSpec((tq, OW), lambda b, i, j: (b * nq + i, 0)),
        out_shape=jax.ShapeDtypeStruct((B * S, OW), BF16),
        scratch_shapes=scratch(tq),
        compiler_params=_cparams(("arbitrary", "arbitrary", "arbitrary")),
        name="mla_attn_lat",
    )(q, k, v, k, v)
    o_ctx = pl.pallas_call(
        _mla_ctx_kernel,
        grid=(B,),
        in_specs=[pl.BlockSpec((NC, W), lambda b: (b, 0))] * 3,
        out_specs=pl.BlockSpec((NC, OW), lambda b: (b, 0)),
        out_shape=jax.ShapeDtypeStruct((TC, OW), BF16),
        scratch_shapes=scratch(NC),
        compiler_params=_cparams(("arbitrary",)),
        name="mla_attn_ctx",
    )(q, k, v)
    return o_ctx, o_lat


def _nat_softmax_pv(scores, values):
    m = scores[0].max(axis=1, keepdims=True)
    for s in scores[1:]:
        m = jnp.maximum(m, s.max(axis=1, keepdims=True))
    l = None
    o = None
    for s, v in zip(scores, values):
        p = jnp.exp2(s - m)
        ls = jnp.sum(p, axis=1, keepdims=True)
        pv = jnp.dot(p.astype(BF16), v, preferred_element_type=F32)
        l = ls if l is None else l + ls
        o = pv if o is None else o + pv
    return o / l


def _nat_lat_kernel(q_ref, k0_ref, k1_ref, k2_ref, v0_ref, v1_ref, v2_ref, kc_ref, vc_ref, bias_ref, o_ref,
                    *, qscale):
    tq = q_ref.shape[0]
    ck = k0_ref.shape[0]
    lane = lax.broadcasted_iota(jnp.int32, (tq, LANES), 1)
    dn = (((1,), (1,)), ((), ()))
    for hp in range(NAT_HEADS // 2):
        ls = slice(hp * LANES, (hp + 1) * LANES)
        qp = q_ref[:, ls].astype(F32) * qscale
        keys = [r[:, ls] for r in (k0_ref, k1_ref, k2_ref, kc_ref)]
        vals = [r[:, ls] for r in (v0_ref, v1_ref, v2_ref, vc_ref)]
        outs = []
        for hh in range(2):
            mask = (lane < NAT_HEAD_DIM) if hh == 0 else (lane >= NAT_HEAD_DIM)
            qm = jnp.where(mask, qp, 0.0).astype(BF16)
            scores = []
            for c, kk in enumerate(keys):
                s = lax.dot_general(qm, kk, dn, preferred_element_type=F32)
                if c < 3:
                    s = s + bias_ref[2 * hp + hh, :, c * ck:(c + 1) * ck].astype(F32)
                scores.append(s)
            outs.append(_nat_softmax_pv(scores, vals))
        o_ref[:, ls] = jnp.where(lane < NAT_HEAD_DIM, outs[0], outs[1]).astype(o_ref.dtype)


def _nat_ctx_kernel(q_ref, kc_ref, vc_ref, o_ref, *, qscale):
    tq = q_ref.shape[0]
    lane = lax.broadcasted_iota(jnp.int32, (tq, LANES), 1)
    dn = (((1,), (1,)), ((), ()))
    for hp in range(NAT_HEADS // 2):
        ls = slice(hp * LANES, (hp + 1) * LANES)
        qp = q_ref[:, ls].astype(F32) * qscale
        outs = []
        for hh in range(2):
            mask = (lane < NAT_HEAD_DIM) if hh == 0 else (lane >= NAT_HEAD_DIM)
            qm = jnp.where(mask, qp, 0.0).astype(BF16)
            s = lax.dot_general(qm, kc_ref[:, ls], dn, preferred_element_type=F32)
            outs.append(_nat_softmax_pv([s], [vc_ref[:, ls]]))
        o_ref[:, ls] = jnp.where(lane < NAT_HEAD_DIM, outs[0], outs[1]).astype(o_ref.dtype)


def _nat_bias_table(rpb, rows_n):
    L = rpb.shape[0]
    qc = np.arange(GRID_W)
    cs = np.clip(qc - NAT_WIN_W // 2, 0, GRID_W - NAT_WIN_W)
    kc = np.arange(GRID_W)
    colvalid = (kc[None, :] >= cs[:, None]) & (kc[None, :] < cs[:, None] + NAT_WIN_W)
    dc = np.clip(kc[None, :] - qc[:, None] + NAT_WIN_W - 1, 0, 2 * NAT_WIN_W - 2)
    bt = jnp.where(colvalid, rpb[:, :, :, dc] * LOG2E, NEG_BIG)
    n_dr = 2 * NAT_WIN_H - 1
    bt = jnp.concatenate([bt, jnp.full((L, NAT_HEADS, 1, GRID_W, GRID_W), NEG_BIG, F32)], axis=2).astype(BF16)
    kh = NAT_WIN_H
    dr = np.full((3, NAT_ROWS, NAT_KROWS), n_dr, np.int32)
    for v, r0 in enumerate((0, 2 * NAT_ROWS, rows_n - NAT_ROWS)):
        kstart = int(np.clip(r0 - kh // 2, 0, rows_n - NAT_KROWS))
        for i in range(NAT_ROWS):
            r = r0 + i
            rs = int(np.clip(r - kh // 2, 0, rows_n - kh))
            for jj in range(NAT_KROWS):
                kr = kstart + jj
                if rs <= kr < rs + kh:
                    dr[v, i, jj] = kr - r + NAT_WIN_H - 1
    t = bt[:, :, dr]
    t = t.transpose(0, 2, 1, 3, 5, 4, 6)
    return t.reshape(L, 3, NAT_HEADS, NAT_ROWS * GRID_W, NAT_KROWS * GRID_W)


def _nat_attention(u, bias, layer, B, S, NC):
    TA = u.shape[0]
    TC = B * NC
    rows_n = S // GRID_W
    nblk = rows_n // NAT_ROWS
    tq = NAT_ROWS * GRID_W
    ck = tq
    qscale = float(NAT_HEAD_DIM ** -0.5) * LOG2E
    cq, ckk, cv = U_NQ // NAT_WIDTH, U_NK // NAT_WIDTH, U_NV // NAT_WIDTH
    base = lambda b: TC // tq + b * nblk

    def kstart_blk(i):
        return jnp.clip(i * NAT_ROWS - NAT_WIN_H // 2, 0, rows_n - NAT_KROWS) // NAT_ROWS

    def kv_spec(col, c):
        return pl.BlockSpec((ck, NAT_WIDTH), lambda b, i: (base(b) + kstart_blk(i) + c, col))

    def variant(i):
        return jnp.where(i == 0, 0, jnp.where(i == nblk - 1, 2, 1))

    o_lat = pl.pallas_call(
        functools.partial(_nat_lat_kernel, qscale=qscale),
        grid=(B, nblk),
        in_specs=[pl.BlockSpec((tq, NAT_WIDTH), lambda b, i: (base(b) + i, cq)),
                  kv_spec(ckk, 0), kv_spec(ckk, 1), kv_spec(ckk, 2),
                  kv_spec(cv, 0), kv_spec(cv, 1), kv_spec(cv, 2),
                  pl.BlockSpec((NC, NAT_WIDTH), lambda b, i: (b, ckk)),
                  pl.BlockSpec((NC, NAT_WIDTH), lambda b, i: (b, cv)),
                  pl.BlockSpec((None, None, NAT_HEADS, tq, NAT_KROWS * GRID_W),
                               lambda b, i: (layer, variant(i), 0, 0, 0))],
        out_specs=pl.BlockSpec((tq, NAT_WIDTH), lambda b, i: (b * nblk + i, 0)),
        out_shape=jax.ShapeDtypeStruct((B * S, NAT_WIDTH), BF16),
        compiler_params=_cparams(("arbitrary", "arbitrary")),
        name="nat_attn_lat",
    )(u, u, u, u, u, u, u, u, u, bias)
    o_ctx = pl.pallas_call(
        functools.partial(_nat_ctx_kernel, qscale=qscale),
        grid=(B,),
        in_specs=[pl.BlockSpec((NC, NAT_WIDTH), lambda b: (b, cq)),
                  pl.BlockSpec((NC, NAT_WIDTH), lambda b: (b, ckk)),
                  pl.BlockSpec((NC, NAT_WIDTH), lambda b: (b, cv))],
        out_specs=pl.BlockSpec((NC, NAT_WIDTH), lambda b: (b, 0)),
        out_shape=jax.ShapeDtypeStruct((TC, NAT_WIDTH), BF16),
        compiler_params=_cparams(("arbitrary",)),
        name="nat_attn_ctx",
    )(u, u, u)
    return o_ctx, o_lat


def _fnet_chan_kernel(f_ref, w_ref, o_ref):
    for g in range(FNET_GROUPS):
        gs = slice(g * FNET_GROUP_DIM, (g + 1) * FNET_GROUP_DIM)
        o_ref[:, gs] = jnp.dot(f_ref[:, gs], w_ref[...], preferred_element_type=F32).astype(o_ref.dtype)


def _fnet_seq_kernel(a_ref, r_ref, o_ref, acc_ref):
    k = pl.program_id(2)

    @pl.when(k == 0)
    def _():
        acc_ref[...] = jnp.zeros(acc_ref.shape, F32)

    acc_ref[...] += jnp.dot(a_ref[...], r_ref[...], preferred_element_type=F32)

    @pl.when(k == pl.num_programs(2) - 1)
    def _():
        o_ref[...] = acc_ref[...].astype(o_ref.dtype)


def _dft_tables(n, scale):
    j = jnp.arange(n, dtype=jnp.int32)
    idx = (j[:, None] * j[None, :]) & (n - 1)
    ang = idx.astype(F32) * (2.0 * math.pi / n)
    return jnp.cos(ang) * scale, jnp.sin(ang) * scale


def _seq_dft_matrix(n):
    c, s = _dft_tables(n, n ** -0.5)
    return jnp.concatenate([c, -s], axis=1).astype(BF16)


FFT_N2 = 128
FFT_NB = 16


def _fft_tables(n):
    n2 = FFT_N2
    n1 = n // n2
    j1 = jnp.arange(n1, dtype=jnp.int32)[:, None, None]
    k2 = jnp.arange(n2, dtype=jnp.int32)[None, :, None]
    j2 = jnp.arange(n2, dtype=jnp.int32)[None, None, :]
    ang = ((k2 * (j1 + n1 * j2)) & (n - 1)).astype(F32) * (2.0 * math.pi / n)
    s1 = n2 ** -0.5
    m1 = jnp.concatenate([jnp.cos(ang) * s1, -jnp.sin(ang) * s1], axis=1).astype(BF16)
    c1, sn1 = _dft_tables(n1, n1 ** -0.5)
    top = jnp.stack([c1, sn1], axis=2).reshape(n1, 2 * n1)
    bot = jnp.stack([-sn1, c1], axis=2).reshape(n1, 2 * n1)
    w2 = jnp.concatenate([top, bot], axis=0).astype(BF16)
    return m1, w2


def _load_slab(ref, i):
    rows = ref.shape[1] // FFT_NB
    parts = [ref[g, pl.ds(i, rows, stride=FFT_NB), :] for g in range(FNET_GROUPS)]
    return jnp.concatenate(parts, axis=1)


def _fft1_kernel(f_ref, m_ref, o_ref):
    n2 = m_ref.shape[2]
    for i in range(FFT_NB):
        r = jnp.dot(m_ref[i], _load_slab(f_ref, i).astype(BF16), preferred_element_type=F32)
        for g in range(FNET_GROUPS):
            gs = slice(g * FNET_GROUP_DIM, (g + 1) * FNET_GROUP_DIM)
            for p in range(2):
                for kb in range(n2 // FFT_NB):
                    src = p * n2 + kb * FFT_NB
                    dst = (2 * i + p) * FFT_NB
                    o_ref[g, kb, dst:dst + FFT_NB, :] = r[src:src + FFT_NB, gs]


def _fft2_kernel(b_ref, w_ref, cc_ref, cs_ref, o_ref):
    n1 = o_ref.shape[1] // FFT_NB
    for i in range(FFT_NB):
        x = jnp.dot(w_ref[...], _load_slab(b_ref, i).astype(BF16), preferred_element_type=F32).astype(BF16)
        for g in range(FNET_GROUPS):
            gs = slice(g * FNET_GROUP_DIM, (g + 1) * FNET_GROUP_DIM)
            y = (jnp.dot(x[:n1, gs], cc_ref[...], preferred_element_type=F32)
                 + jnp.dot(x[n1:, gs], cs_ref[...], preferred_element_type=F32))
            o_ref[g, pl.ds(i, n1, stride=FFT_NB), :] = y


def _fourier_mix_long(f_lat, m1, w2, w_chan, B, n):
    n2 = FFT_N2
    n1 = n // n2
    G, GD, NB = FNET_GROUPS, FNET_GROUP_DIM, FFT_NB
    bmid = pl.pallas_call(
        _fft1_kernel,
        grid=(B, n1 // NB),
        in_specs=[pl.BlockSpec((None, G, None, n2 * NB, GD), lambda b, j: (b, 0, j, 0, 0)),
                  pl.BlockSpec((NB, 2 * n2, n2), lambda b, j: (j, 0, 0))],
        out_specs=pl.BlockSpec((None, G, n2 // NB, 2 * NB * NB, GD), lambda b, j: (b, 0, 0, j, 0)),
        out_shape=jax.ShapeDtypeStruct((B, G, n2 // NB, 2 * n1 * NB, GD), F32),
        compiler_params=_cparams(("arbitrary", "arbitrary")),
        name="fnet_fft1",
    )(f_lat, m1)
    return pl.pallas_call(
        _fft2_kernel,
        grid=(B, n2 // NB),
        in_specs=[pl.BlockSpec((None, G, None, 2 * n1 * NB, GD), lambda b, j: (b, 0, j, 0, 0)),
                  pl.BlockSpec((2 * n1, 2 * n1), lambda b, j: (0, 0)),
                  pl.BlockSpec((GD, GD), lambda b, j: (0, 0)),
                  pl.BlockSpec((GD, GD), lambda b, j: (0, 1))],
        out_specs=pl.BlockSpec((None, G, None, n1 * NB, GD), lambda b, j: (b, 0, j, 0, 0)),
        out_shape=jax.ShapeDtypeStruct((B, G, n2 // NB, n1 * NB, GD), F32),
        compiler_params=_cparams(("arbitrary", "arbitrary")),
        name="fnet_fft2",
    )(bmid, w2, w_chan, w_chan)


def _fourier_mix(u, row0, B, n, a_mat, w_chan, ts, tm, tk):
    nt = n // ts
    rr = pl.pallas_call(
        _fnet_chan_kernel,
        grid=(B, nt, 2),
        in_specs=[pl.BlockSpec((ts, FNET_WIDTH), lambda b, i, p: (row0 // ts + b * nt + i, U_F // FNET_WIDTH)),
                  pl.BlockSpec((FNET_GROUP_DIM, FNET_GROUP_DIM), lambda b, i, p: (0, p))],
        out_specs=pl.BlockSpec((ts, FNET_WIDTH), lambda b, i, p: (p * nt + i, b)),
        out_shape=jax.ShapeDtypeStruct((2 * n, B * FNET_WIDTH), BF16),
        compiler_params=_cparams(("arbitrary", "arbitrary", "arbitrary")),
        name="fnet_chan",
    )(u, w_chan)
    tn = min(B * FNET_WIDTH, 2048)
    return pl.pallas_call(
        _fnet_seq_kernel,
        grid=(n // tm, (B * FNET_WIDTH) // tn, (2 * n) // tk),
        in_specs=[pl.BlockSpec((tm, tk), lambda i, j, k: (i, k)),
                  pl.BlockSpec((tk, tn), lambda i, j, k: (k, j))],
        out_specs=pl.BlockSpec((tm, tn), lambda i, j, k: (i, j)),
        out_shape=jax.ShapeDtypeStruct((n, B * FNET_WIDTH), BF16),
        scratch_shapes=[pltpu.VMEM((tm, tn), F32)],
        compiler_params=_cparams(("arbitrary", "arbitrary", "arbitrary")),
        name="fnet_seq",
    )(a_mat, rr)


def _merge_kernel(oml_ref, omc_ref, ofl_ref, ofc_ref, onl_ref, onc_ref, gm_ref, gf_ref, gn_ref, x_ref, mod_ref,
                  wm_ref, wf_ref, wn_ref, wo_ref, bo_ref, lg_ref, lb_ref, wr_ref, br_ref, xo_ref, h_ref, lo_ref,
                  *, alpha, nct):
    def gate(g_ref):
        return 1.0 / (1.0 + jnp.exp(-g_ref[...].astype(F32)))

    is_ctx = pl.program_id(0) < nct
    nkb, rows = ofl_ref.shape[1], ofl_ref.shape[2]
    ofl = jnp.concatenate(
        [jnp.concatenate([ofl_ref[g, kb, k1 * FFT_NB:(k1 + 1) * FFT_NB, :]
                          for k1 in range(rows // FFT_NB) for kb in range(nkb)], axis=0)
         for g in range(FNET_GROUPS)], axis=1)
    of = jnp.where(is_ctx, ofc_ref[...].astype(F32), ofl).astype(BF16)
    om = jnp.where(is_ctx, omc_ref[...], oml_ref[...])
    on = jnp.where(is_ctx, onc_ref[...], onl_ref[...])
    m = gate(gm_ref) * jnp.dot(om, wm_ref[...], preferred_element_type=F32)
    m = m + gate(gf_ref) * jnp.dot(of, wf_ref[...], preferred_element_type=F32)
    m = m + gate(gn_ref) * jnp.dot(on, wn_ref[...], preferred_element_type=F32)
    y = jnp.dot(m.astype(BF16), wo_ref[...], preferred_element_type=F32) + bo_ref[...]
    z = alpha * x_ref[...] + mod_ref[2:3, :] * y
    xn = _ln(z) * lg_ref[...] + lb_ref[...]
    xo_ref[...] = xn
    h = (_ln(xn) * (1.0 + mod_ref[4:5, :]) + mod_ref[3:4, :]).astype(BF16)
    h_ref[0] = h
    h_ref[1] = h
    lo_ref[...] = jnp.dot(h, wr_ref[...], preferred_element_type=F32) + br_ref[...]


def _merge(om_lat, om_ctx, of_lat, of_ctx, on_lat, on_ctx, u, xall, mod, mod_row, wm, wf, wn, wo, bo, lg, lb, wr, br,
           alpha, B, S, NC):
    TA, D = xall.shape
    TC = B * NC
    nct = TC // TM
    npb = S // TM
    const = lambda i: (0, 0)
    row = lambda i: (i, 0)
    gcol = U_G // D

    lat = lambda i: jnp.maximum(i - nct, 0)
    lat_row = lambda i: (lat(i), 0)
    ctx_row = lambda i: (jnp.minimum(i, nct - 1), 0)
    outs = pl.pallas_call(
        functools.partial(_merge_kernel, alpha=alpha, nct=nct),
        grid=(TA // TM,),
        in_specs=[pl.BlockSpec((TM, MLA_HEADS * MLA_V), lat_row),
                  pl.BlockSpec((TM, MLA_HEADS * MLA_V), ctx_row),
                  pl.BlockSpec((None, FNET_GROUPS, FFT_N2 // FFT_NB, (TM // FFT_N2) * FFT_NB, FNET_GROUP_DIM),
                               lambda i: (lat(i) // npb, 0, 0, lat(i) % npb, 0)),
                  pl.BlockSpec((TM, FNET_WIDTH), ctx_row),
                  pl.BlockSpec((TM, NAT_WIDTH), lat_row),
                  pl.BlockSpec((TM, NAT_WIDTH), ctx_row),
                  pl.BlockSpec((TM, D), lambda i: (i, gcol)),
                  pl.BlockSpec((TM, D), lambda i: (i, gcol + 1)),
                  pl.BlockSpec((TM, D), lambda i: (i, gcol + 2)),
                  pl.BlockSpec((TM, D), row),
                  pl.BlockSpec((None, 8, D), lambda i: (mod_row(i), 0, 0)),
                  pl.BlockSpec((MLA_HEADS * MLA_V, D), const),
                  pl.BlockSpec((FNET_WIDTH, D), const),
                  pl.BlockSpec((NAT_WIDTH, D), const),
                  pl.BlockSpec((D, D), const),
                  pl.BlockSpec((1, D), const),
                  pl.BlockSpec((1, D), const),
                  pl.BlockSpec((1, D), const),
                  pl.BlockSpec((D, LANES), const),
                  pl.BlockSpec((1, LANES), const)],
        out_specs=[pl.BlockSpec((TM, D), row), pl.BlockSpec((2, TM, D), lambda i: (0, i, 0)),
                   pl.BlockSpec((TM, LANES), row)],
        out_shape=[jax.ShapeDtypeStruct((TA, D), F32), jax.ShapeDtypeStruct((2, TA, D), BF16),
                   jax.ShapeDtypeStruct((TA, LANES), F32)],
        compiler_params=_cparams(("arbitrary",)),
        name="merge",
    )(om_lat, om_ctx, of_lat, of_ctx, on_lat, on_ctx, u, u, u, xall, mod, wm, wf, wn, wo, bo, lg, lb, wr, br)
    return outs


def _router_kernel(lg_ref, e_ref, w_ref, r_ref, cnt_ref, carry_ref):
    lg = lg_ref[...]
    lane = lax.broadcasted_iota(jnp.int32, lg.shape, 1)
    big = jnp.int32(1 << 20)
    is_g = lane < N_GROUPS
    gl = jnp.where(is_g, lg, -jnp.inf)
    ge = jnp.exp(gl - jnp.max(gl, axis=1, keepdims=True))
    p = ge / jnp.sum(ge, axis=1, keepdims=True)
    p_top = jnp.max(p, axis=1, keepdims=True)
    g_idx = jnp.min(jnp.where(is_g & (p == p_top), lane, big), axis=1, keepdims=True)
    lo = N_GROUPS + g_idx * EXPERTS_PER_GROUP
    in_grp = (lane >= lo) & (lane < lo + EXPERTS_PER_GROUP)
    el = jnp.where(in_grp, lg, -jnp.inf)
    m1 = jnp.max(el, axis=1, keepdims=True)
    i1 = jnp.min(jnp.where(in_grp & (el == m1), lane, big), axis=1, keepdims=True)
    rest = in_grp & (lane != i1)
    el2 = jnp.where(rest, lg, -jnp.inf)
    m2 = jnp.max(el2, axis=1, keepdims=True)
    i2 = jnp.min(jnp.where(rest & (el2 == m2), lane, big), axis=1, keepdims=True)
    d = jnp.exp(m2 - m1)
    w1 = p_top * (1.0 / (1.0 + d))
    w2 = p_top * (d / (1.0 + d))
    e_ref[...] = jnp.where(lane == 0, i1 - N_GROUPS, jnp.where(lane == 1, i2 - N_GROUPS, 0))
    w_ref[...] = jnp.where(lane == 0, w1, jnp.where(lane == 1, w2, 0.0))

    @pl.when(pl.program_id(0) == 0)
    def _():
        carry_ref[...] = jnp.zeros(carry_ref.shape, F32)

    tm = lg.shape[0]
    oh1 = lane == i1
    oh2 = lane == i2
    oh = jnp.where(oh1, 1.0, jnp.where(oh2, 1.0, 0.0))
    rr = lax.broadcasted_iota(jnp.int32, (tm, tm), 0)
    cc = lax.broadcasted_iota(jnp.int32, (tm, tm), 1)
    lower = jnp.where(cc < rr, 1.0, 0.0).astype(BF16)
    before = jnp.dot(lower, oh.astype(BF16), preferred_element_type=F32) + carry_ref[0:1, :]
    r1 = jnp.sum(jnp.where(oh1, before, 0.0), axis=1, keepdims=True)
    r2 = jnp.sum(jnp.where(oh2, before, 0.0), axis=1, keepdims=True)
    r_ref[...] = jnp.where(lane == 0, r1, jnp.where(lane == 1, r2, 0.0)).astype(jnp.int32)
    carry_ref[...] = carry_ref[...] + jnp.sum(oh, axis=0, keepdims=True)
    cnt_ref[...] = carry_ref[...].astype(jnp.int32)


def _router(logits):
    TA = logits.shape[0]
    row = lambda i: (i, 0)
    return pl.pallas_call(
        _router_kernel,
        grid=(TA // TM,),
        in_specs=[pl.BlockSpec((TM, LANES), row)],
        out_specs=[pl.BlockSpec((TM, LANES), row)] * 3 + [pl.BlockSpec((8, LANES), lambda i: (0, 0))],
        out_shape=[jax.ShapeDtypeStruct((TA, LANES), jnp.int32), jax.ShapeDtypeStruct((TA, LANES), F32),
                   jax.ShapeDtypeStruct((TA, LANES), jnp.int32), jax.ShapeDtypeStruct((8, LANES), jnp.int32)],
        scratch_shapes=[pltpu.VMEM((8, LANES), F32)],
        compiler_params=_cparams(("arbitrary",)),
        name="router",
    )(logits)


def _moe_kernel(be_ref, nu_ref, x_ref, wg_ref, wu_ref, wd_ref, o_ref, wgb_ref, wub_ref, wdb_ref):
    i = pl.program_id(0)
    changed = jnp.logical_or(i == 0, be_ref[i] != be_ref[jnp.maximum(i - 1, 0)])

    @pl.when(changed)
    def _():
        wgb_ref[...] = wg_ref[...].astype(BF16)
        wub_ref[...] = wu_ref[...].astype(BF16)
        wdb_ref[...] = wd_ref[...].astype(BF16)

    @pl.when(i < nu_ref[0])
    def _():
        x = x_ref[...]
        a = jnp.dot(x, wgb_ref[...], preferred_element_type=F32)
        b = jnp.dot(x, wub_ref[...], preferred_element_type=F32)
        hmid = (a * (1.0 / (1.0 + jnp.exp(-a))) * b).astype(BF16)
        o_ref[...] = jnp.dot(hmid, wdb_ref[...], preferred_element_type=F32).astype(o_ref.dtype)

    @pl.when(i >= nu_ref[0])
    def _():
        o_ref[...] = jnp.zeros(o_ref.shape, o_ref.dtype)


def _moe_experts(xb, block_e, n_used, w_gate, w_up, w_down, layer):
    n_rows, D = xb.shape
    nb = n_rows // MOE_BM
    DE = w_gate.shape[-1]
    xrow = lambda i, be, nu: (jnp.minimum(i, nu[0] - 1), 0)
    wmap = lambda i, be, nu: (layer, be[i], 0, 0)
    grid_spec = pltpu.PrefetchScalarGridSpec(
        num_scalar_prefetch=2,
        grid=(nb,),
        in_specs=[pl.BlockSpec((MOE_BM, D), xrow),
                  pl.BlockSpec((None, None, D, DE), wmap),
                  pl.BlockSpec((None, None, D, DE), wmap),
                  pl.BlockSpec((None, None, DE, D), wmap)],
        out_specs=pl.BlockSpec((MOE_BM, D), lambda i, be, nu: (i, 0)),
        scratch_shapes=[pltpu.VMEM((D, DE), BF16), pltpu.VMEM((D, DE), BF16), pltpu.VMEM((DE, D), BF16)],
    )
    return pl.pallas_call(
        _moe_kernel,
        grid_spec=grid_spec,
        out_shape=jax.ShapeDtypeStruct((n_rows, D), BF16),
        compiler_params=_cparams(("arbitrary",)),
        name="moe_experts",
    )(block_e, n_used, xb, w_gate, w_up, w_down)


def _ffn_out_kernel(x_ref, y0_ref, y1_ref, w_ref, mod_ref, lg_ref, lb_ref, o_ref, *, alpha):
    w = w_ref[...]
    f = y0_ref[...].astype(F32) * w[:, 0:1] + y1_ref[...].astype(F32) * w[:, 1:2]
    z = alpha * x_ref[...] + mod_ref[5:6, :] * f
    o_ref[...] = _ln(z) * lg_ref[...] + lb_ref[...]


def _ffn_out(xall, y2, w_tok, mod, mod_row, lg, lb, alpha, first_tile=0):
    TA, D = xall.shape
    nt = TA // TM
    row = lambda i: (i + first_tile, 0)
    const = lambda i: (0, 0)
    return pl.pallas_call(
        functools.partial(_ffn_out_kernel, alpha=alpha),
        grid=(nt - first_tile,),
        in_specs=[pl.BlockSpec((TM, D), row), pl.BlockSpec((TM, D), row),
                  pl.BlockSpec((TM, D), lambda i: (nt + i + first_tile, 0)),
                  pl.BlockSpec((TM, LANES), row),
                  pl.BlockSpec((None, 8, D), lambda i: (mod_row(i + first_tile), 0, 0)),
                  pl.BlockSpec((1, D), const), pl.BlockSpec((1, D), const)],
        out_specs=pl.BlockSpec((TM, D), lambda i: (i, 0)),
        out_shape=jax.ShapeDtypeStruct((TA - first_tile * TM, D), F32),
        compiler_params=_cparams(("arbitrary",)),
        name="ffn_out",
    )(xall, y2, y2, w_tok, mod, lg, lb)


def _reorder_w_in(w_in, b_in):
    D = w_in.shape[0]
    o_cq, o_ckv, o_kr = 0, MLA_Q_LORA, MLA_Q_LORA + MLA_KV_LORA
    o_f = o_kr + MLA_ROPE
    o_nq = o_f + FNET_WIDTH
    o_g = o_nq + 3 * NAT_WIDTH
    rot_src, rot_sign = _rope_partner()

    def build(a):
        kr = a[..., o_kr:o_kr + MLA_ROPE]
        kr_rot = kr[..., rot_src] * rot_sign
        pad = jnp.zeros(a.shape[:-1] + (LANES - 2 * MLA_ROPE,), a.dtype)
        return jnp.concatenate([a[..., o_f:o_g], a[..., o_g:], a[..., o_cq:o_kr], kr, kr_rot, pad], axis=-1)

    return build(w_in).astype(BF16), build(b_in[None, :])


def _rope_partner():
    q = MLA_ROPE // 4
    src = np.zeros(MLA_ROPE, np.int32)
    sign = np.zeros(MLA_ROPE, np.float32)
    for half in range(2):
        o = half * 2 * q
        for i in range(q):
            src[o + i], sign[o + i] = o + q + i, -1.0
            src[o + q + i], sign[o + q + i] = o + i, 1.0
    return src, sign


def _pad_heads(w, width, offset=0):
    K = w.shape[0]
    w = w.reshape(K, MLA_HEADS, width)
    out = jnp.zeros((K, MLA_HEADS, HEAD_PAD), w.dtype)
    return out.at[:, :, offset:offset + width].set(w).reshape(K, MLA_HEADS * HEAD_PAD)


def _mla_weights(w_uq, w_uk, w_uv):
    rot_src, rot_sign = _rope_partner()
    dq = MLA_NOPE + MLA_ROPE
    wq = w_uq.reshape(-1, MLA_HEADS, dq)
    wq_pe_rot = wq[:, :, MLA_NOPE:][:, :, rot_src] * rot_sign
    wqa = _pad_heads(wq.reshape(-1, MLA_HEADS * dq), dq)
    wqb = _pad_heads(wq_pe_rot.reshape(-1, MLA_HEADS * MLA_ROPE), MLA_ROPE, MLA_NOPE)
    wk = _pad_heads(w_uk, MLA_NOPE)
    K = w_uv.shape[0]
    wv3 = w_uv.reshape(K, MLA_HEADS // 2, 2, MLA_V)
    wv = jnp.zeros((K, MLA_HEADS // 2, 2, HEAD_PAD), w_uv.dtype)
    wv = wv.at[:, :, 0, :MLA_V].set(wv3[:, :, 0]).at[:, :, 1, MLA_V:].set(wv3[:, :, 1])
    wv = wv.reshape(K, MLA_HEADS * HEAD_PAD)
    vone = np.zeros((MLA_HEADS // 2, 2, HEAD_PAD), np.float32)
    vone[:, 0, MLA_V] = 1.0
    vone[:, 1, 0] = 1.0
    pa = np.zeros((LANES, MLA_HEADS, HEAD_PAD), np.float32)
    pb = np.zeros((LANES, MLA_HEADS, HEAD_PAD), np.float32)
    for i in range(MLA_ROPE):
        pa[i, :, MLA_NOPE + i] = 1.0
        pb[MLA_ROPE + i, :, MLA_NOPE + i] = 1.0
    W = MLA_HEADS * HEAD_PAD
    return (wqa.astype(BF16), wqb.astype(BF16), wk.astype(BF16), wv.astype(BF16),
            jnp.asarray(pa.reshape(LANES, W), BF16), jnp.asarray(pb.reshape(LANES, W), BF16),
            jnp.asarray(vone.reshape(1, W)))


def _rope_lane_tables(S):
    t = jnp.arange(S)
    rows = (t // GRID_W).astype(F32)
    cols = (t % GRID_W).astype(F32)
    n_freq = MLA_ROPE // 4
    inv = jnp.power(ROPE_BASE, -jnp.arange(n_freq, dtype=F32) / n_freq)
    ar = rows[:, None] * inv[None, :]
    ac = cols[:, None] * inv[None, :]
    ones = jnp.ones((S, MLA_NOPE), F32)
    zpad = jnp.zeros((S, HEAD_PAD - MLA_NOPE - MLA_ROPE), F32)
    cos = jnp.concatenate([ones, jnp.cos(ar), jnp.cos(ar), jnp.cos(ac), jnp.cos(ac), zpad], axis=1)
    sin = jnp.concatenate([0 * ones, jnp.sin(ar), jnp.sin(ar), jnp.sin(ac), jnp.sin(ac), zpad], axis=1)
    idc = jnp.concatenate([jnp.ones((TM, MLA_NOPE + MLA_ROPE), F32), jnp.zeros((TM, HEAD_PAD - MLA_NOPE - MLA_ROPE), F32)], axis=1)
    return jnp.concatenate([idc, cos], axis=0), jnp.concatenate([jnp.zeros((TM, HEAD_PAD), F32), sin], axis=0)


def _moe_plan(e_ids, ranks, counts, n_blocks):
    t = e_ids.shape[0]
    a = 2 * t
    padded = (counts + MOE_BM - 1) // MOE_BM * MOE_BM
    pends = jnp.cumsum(padded)
    pstarts = pends - padded
    eo = jnp.arange(N_EXPERTS, dtype=jnp.int32)
    pos = ranks + jnp.sum(jnp.where(e_ids[:, :, None] == eo, pstarts, 0), axis=-1)
    blk0 = jnp.arange(n_blocks, dtype=jnp.int32) * MOE_BM
    block_e = jnp.minimum(jnp.sum((pends[None, :] <= blk0[:, None]).astype(jnp.int32), axis=1), N_EXPERTS - 1)
    n_used = (pends[-1] // MOE_BM).astype(jnp.int32).reshape(1)
    nd = n_blocks * MOE_BM - a
    dcum = jnp.cumsum(padded - counts)
    d_e = jnp.sum((dcum[None, :] <= jnp.arange(nd, dtype=jnp.int32)[:, None]).astype(jnp.int32), axis=1)
    keys = jnp.concatenate([e_ids.reshape(a) * 2, d_e * 2 + 1])
    payload = jnp.concatenate([jnp.arange(a, dtype=jnp.int32), (jnp.arange(nd, dtype=jnp.int32) * 2) % a])
    _, row_asg = lax.sort((keys, payload), num_keys=1, is_stable=True)
    row_src = (row_asg % 2) * t + row_asg // 2
    return row_src, pos, block_e, n_used


def kernel(x, c, ctx, c_ctx, w_ada, b_ada, w_in, b_in, q_norm_g, kv_norm_g, w_uq, w_uk, w_uv, rpb, w_o_mla, w_o_fnet, w_o_nat, w_out, b_out, ln1_g, ln1_b, w_rg, b_rg, w_re, b_re, w_gate_e, w_up_e, w_down_e, ln2_g, ln2_b):
    B, S, D = x.shape
    NC = ctx.shape[1]
    L = w_ada.shape[0]
    TC, TL = B * NC, B * S
    TA = TC + TL
    assert TC % TM == 0 and S % TM == 0 and NC % 128 == 0 and B < 8
    alpha = float((2 * L) ** 0.25)
    nct, npb = TC // TM, S // TM

    def mod_row(i):
        return jnp.where(i < nct, B, jnp.maximum(i - nct, 0) // npb)

    def tab_row(i):
        return jnp.where(i < nct, 0, 1 + jnp.maximum(i - nct, 0) % npb)

    cc = jnp.concatenate([c, c_ctx[None, :], jnp.zeros((8 - B - 1, D), F32)], axis=0)
    mod_all = _ada_all(cc, w_ada, b_ada).reshape(L, 8, 6, D)
    mod_all = jnp.concatenate([mod_all, jnp.zeros((L, 8, 2, D), F32)], axis=2)

    tabc, tabs = _rope_lane_tables(S)
    chan_scale = FNET_GROUP_DIM ** -0.5
    cch, sch = _dft_tables(FNET_GROUP_DIM, chan_scale)
    w_chan = jnp.concatenate([cch, sch], axis=1).astype(BF16)
    assert S % (FFT_N2 * FFT_NB) == 0
    m1, w2 = _fft_tables(S)
    a_ctx = _seq_dft_matrix(NC)
    nat_bias = _nat_bias_table(rpb, S // GRID_W)
    qscale = float((MLA_NOPE + MLA_ROPE) ** -0.5) * LOG2E
    rows_n = S // GRID_W
    n_blocks = (2 * TA) // MOE_BM + N_EXPERTS
    tq = math.gcd(1024, math.gcd(S, TC))
    tk = tq

    xall = jnp.concatenate([ctx.reshape(TC, D), x.reshape(TL, D)], axis=0)
    for l in range(L):
        mod = mod_all[l]
        w_in_r, b_in_r = _reorder_w_in(w_in[l], b_in[l])
        u = _in_proj(xall, mod, w_in_r, b_in_r, mod_row)

        wqa, wqb, wk, wv, pa, pb, vone = _mla_weights(w_uq[l], w_uk[l], w_uv[l])
        q, k, v = _qkv(u, tabc, tabs, tab_row, q_norm_g[l][None, :], kv_norm_g[l][None, :],
                       wqa, wqb, wk, wv, pa, pb, vone, qscale)
        om_ctx, om_lat = _mla_attention(q, k, v, B, S, NC, tq, tk)

        on_ctx, on_lat = _nat_attention(u, nat_bias, l, B, S, NC)

        n1 = S // FFT_N2
        f_lat = u[TC:, U_F:U_F + FNET_WIDTH].astype(F32).reshape(B, FFT_N2, n1 // FFT_NB, FFT_NB, FNET_GROUPS,
                                                                   FNET_GROUP_DIM)
        f_lat = f_lat.transpose(0, 4, 2, 1, 3, 5).reshape(B, FNET_GROUPS, n1 // FFT_NB, FFT_N2 * FFT_NB, FNET_GROUP_DIM)
        of_lat = _fourier_mix_long(f_lat, m1, w2, w_chan, B, S)
        of_ctx = _fourier_mix(u, 0, B, NC, a_ctx, w_chan, ts=NC, tm=NC, tk=2 * NC)
        of_ctx = of_ctx.reshape(NC, B, FNET_WIDTH).transpose(1, 0, 2).reshape(TC, FNET_WIDTH)

        w_r = jnp.concatenate([w_rg[l], w_re[l], jnp.zeros((D, LANES - N_GROUPS - N_EXPERTS), F32)], axis=1).astype(BF16)
        b_r = jnp.concatenate([b_rg[l], b_re[l], jnp.zeros((LANES - N_GROUPS - N_EXPERTS,), F32)])[None, :]
        xall, h_f, logits = _merge(om_lat, om_ctx, of_lat, of_ctx, on_lat, on_ctx, u, xall, mod, mod_row,
                                   w_o_mla[l].astype(BF16), w_o_fnet[l].astype(BF16), w_o_nat[l].astype(BF16),
                                   w_out[l].astype(BF16), b_out[l][None, :], ln1_g[l][None, :], ln1_b[l][None, :],
                                   w_r, b_r, alpha, B, S, NC)

        e_out, w_tok, r_out, cnt = _router(logits)
        counts = cnt[0, N_GROUPS:N_GROUPS + N_EXPERTS]
        row_src, pos, block_e, n_used = _moe_plan(e_out[:, :2], r_out[:, :2], counts, n_blocks)
        xb = jnp.take(h_f.reshape(2 * TA, D), row_src, axis=0, mode="clip")
        yb = _moe_experts(xb, block_e, n_used, w_gate_e, w_up_e, w_down_e, l)
        y2 = jnp.take(yb, jnp.concatenate([pos[:, 0], pos[:, 1]]), axis=0, mode="clip")
        xall = _ffn_out(xall, y2, w_tok, mod, mod_row, ln2_g[l][None, :], ln2_b[l][None, :], alpha,
                        first_tile=nct if l == L - 1 else 0)
    return xall.reshape(B, S, D)
```

```python
import functools
import math

import numpy as np
import jax
import jax.numpy as jnp
from jax import lax
from jax.experimental import pallas as pl
from jax.experimental.pallas import tpu as pltpu

F32 = jnp.float32
BF16 = jnp.bfloat16

GRID_W = 64
MLA_HEADS = 8
MLA_Q_LORA = 256
MLA_KV_LORA = 128
MLA_NOPE = 64
MLA_ROPE = 32
MLA_V = 64
ROPE_BASE = 10000.0
FNET_GROUPS = 4
FNET_GROUP_DIM = 128
FNET_WIDTH = FNET_GROUPS * FNET_GROUP_DIM
NAT_HEADS = 8
NAT_HEAD_DIM = 64
NAT_WIDTH = NAT_HEADS * NAT_HEAD_DIM
NAT_WIN_H = 8
NAT_WIN_W = 16
N_GROUPS = 4
EXPERTS_PER_GROUP = 8
N_EXPERTS = N_GROUPS * EXPERTS_PER_GROUP
D_EXPERT = 512
LN_EPS = 1e-6

LANES = 128
HEAD_PAD = 128
LOG2E = 1.4426950408889634
NEG_BIG = -1e30
VMEM_LIMIT = 56 * 1024 * 1024

U_F, U_NQ, U_NK, U_NV, U_G, U_CQ, U_CKV, U_KR, U_W = 0, 512, 1024, 1536, 2048, 5120, 5376, 5504, 5632

TM = 512
NAT_ROWS = 4
NAT_KROWS = 12
MOE_BM = 512


def _cparams(sem):
    return pltpu.CompilerParams(dimension_semantics=sem, vmem_limit_bytes=VMEM_LIMIT)


def _ln(x):
    mu = jnp.mean(x, axis=-1, keepdims=True)
    xc = x - mu
    var = jnp.mean(xc * xc, axis=-1, keepdims=True)
    return xc * lax.rsqrt(var + LN_EPS)


def _ada_kernel(c_ref, w_ref, b_ref, o_ref):
    c = c_ref[...]
    a = c * (1.0 / (1.0 + jnp.exp(-c)))
    o_ref[...] = jnp.dot(a, w_ref[...], preferred_element_type=F32,
                         precision=lax.Precision.HIGHEST) + b_ref[...]


def _ada_all(cc, w_ada, b_ada):
    L, D, N = w_ada.shape
    tn = 1536
    return pl.pallas_call(
        _ada_kernel,
        grid=(L, N // tn),
        in_specs=[pl.BlockSpec((8, D), lambda l, j: (0, 0)),
                  pl.BlockSpec((None, D, tn), lambda l, j: (l, 0, j)),
                  pl.BlockSpec((None, 1, tn), lambda l, j: (l, 0, j))],
        out_specs=pl.BlockSpec((None, 8, tn), lambda l, j: (l, 0, j)),
        out_shape=jax.ShapeDtypeStruct((L, 8, N), F32),
        compiler_params=_cparams(("arbitrary", "arbitrary")),
        name="ada_mod",
    )(cc, w_ada, b_ada.reshape(L, 1, N))


def _in_proj_kernel(x_ref, mod_ref, w_ref, b_ref, o_ref, f_ref, *, cw):
    y = _ln(x_ref[...])
    h = (y * (1.0 + mod_ref[1:2, :]) + mod_ref[0:1, :]).astype(BF16)
    for j in range(o_ref.shape[1] // cw):
        sl = slice(j * cw, (j + 1) * cw)
        acc = jnp.dot(h, w_ref[:, sl], preferred_element_type=F32) + b_ref[:, sl]
        o_ref[:, sl] = acc.astype(o_ref.dtype)
        if j == U_F // cw:
            nbk = f_ref.shape[1]
            n1 = nbk * FFT_NB
            for g in range(FNET_GROUPS):
                for jb in range(nbk):
                    for jj in range(acc.shape[0] // n1):
                        src = jj * n1 + jb * FFT_NB
                        f_ref[g, jb, jj * FFT_NB:(jj + 1) * FFT_NB, :] = (
                            acc[src:src + FFT_NB, g * FNET_GROUP_DIM:(g + 1) * FNET_GROUP_DIM])


def _in_proj(xall, mod, w_in_r, b_in_r, mod_row, B, S, nct):
    TA, D = xall.shape
    N = w_in_r.shape[1]
    n1 = S // FFT_N2
    npb = S // TM
    lat = lambda i: jnp.maximum(i - nct, 0)
    f_spec = pl.BlockSpec((None, FNET_GROUPS, n1 // FFT_NB, (TM // n1) * FFT_NB, FNET_GROUP_DIM),
                          lambda i: (lat(i) // npb, 0, 0, lat(i) % npb, 0))
    return pl.pallas_call(
        functools.partial(_in_proj_kernel, cw=512),
        grid=(TA // TM,),
        in_specs=[pl.BlockSpec((TM, D), lambda i: (i, 0)),
                  pl.BlockSpec((None, 8, D), lambda i: (mod_row(i), 0, 0)),
                  pl.BlockSpec((D, N), lambda i: (0, 0)),
                  pl.BlockSpec((1, N), lambda i: (0, 0))],
        out_specs=[pl.BlockSpec((TM, N), lambda i: (i, 0)), f_spec],
        out_shape=[jax.ShapeDtypeStruct((TA, N), BF16),
                   jax.ShapeDtypeStruct((B, FNET_GROUPS, n1 // FFT_NB, FFT_N2 * FFT_NB, FNET_GROUP_DIM), F32)],
        compiler_params=_cparams(("arbitrary",)),
        name="in_proj",
    )(xall, mod, w_in_r, b_in_r)


def _rms(x, g):
    return x * lax.rsqrt(jnp.mean(x * x, axis=-1, keepdims=True) + LN_EPS) * g


def _qkv_kernel(cq_ref, ckv_ref, kr_ref, tc_ref, ts_ref, qg_ref, kvg_ref, wqa_ref, wqb_ref,
                wk_ref, wv_ref, pa_ref, pb_ref, vone_ref, q_ref, k_ref, v_ref, *, qscale):
    cqn = _rms(cq_ref[...].astype(F32), qg_ref[...]).astype(BF16)
    ckvn = _rms(ckv_ref[...].astype(F32), kvg_ref[...]).astype(BF16)
    cos = jnp.concatenate([tc_ref[...]] * MLA_HEADS, axis=1)
    sin = jnp.concatenate([ts_ref[...]] * MLA_HEADS, axis=1)
    qa = jnp.dot(cqn, wqa_ref[...], preferred_element_type=F32)
    qb = jnp.dot(cqn, wqb_ref[...], preferred_element_type=F32)
    q_ref[...] = ((qa * cos + qb * sin) * qscale).astype(q_ref.dtype)
    kr = kr_ref[...]
    ka = jnp.dot(kr, pa_ref[...], preferred_element_type=F32)
    kb = jnp.dot(kr, pb_ref[...], preferred_element_type=F32)
    kn = jnp.dot(ckvn, wk_ref[...], preferred_element_type=F32)
    k_ref[...] = (kn + ka * cos + kb * sin).astype(k_ref.dtype)
    v = jnp.dot(ckvn, wv_ref[...], preferred_element_type=F32) + vone_ref[...]
    v_ref[...] = v.astype(v_ref.dtype)


def _qkv(u, tabc, tabs, tab_row, qg, kvg, wqa, wqb, wk, wv, pa, pb, vone, qscale):
    TA = u.shape[0]
    W = MLA_HEADS * HEAD_PAD
    const = lambda i: (0, 0)
    out = jax.ShapeDtypeStruct((TA, W), BF16)
    return pl.pallas_call(
        functools.partial(_qkv_kernel, qscale=qscale),
        grid=(TA // TM,),
        in_specs=[pl.BlockSpec((TM, MLA_Q_LORA), lambda i: (i, U_CQ // MLA_Q_LORA)),
                  pl.BlockSpec((TM, MLA_KV_LORA), lambda i: (i, U_CKV // MLA_KV_LORA)),
                  pl.BlockSpec((TM, LANES), lambda i: (i, U_KR // LANES)),
                  pl.BlockSpec((TM, LANES), lambda i: (tab_row(i), 0)),
                  pl.BlockSpec((TM, LANES), lambda i: (tab_row(i), 0)),
                  pl.BlockSpec((1, MLA_Q_LORA), const),
                  pl.BlockSpec((1, MLA_KV_LORA), const),
                  pl.BlockSpec((MLA_Q_LORA, W), const),
                  pl.BlockSpec((MLA_Q_LORA, W), const),
                  pl.BlockSpec((MLA_KV_LORA, W), const),
                  pl.BlockSpec((MLA_KV_LORA, W), const),
                  pl.BlockSpec((LANES, W), const),
                  pl.BlockSpec((LANES, W), const),
                  pl.BlockSpec((1, W), const)],
        out_specs=[pl.BlockSpec((TM, W), lambda i: (i, 0))] * 3,
        out_shape=[out, out, out],
        compiler_params=_cparams(("arbitrary",)),
        name="mla_qkv",
    )(u, u, u, tabc, tabs, qg, kvg, wqa, wqb, wk, wv, pa, pb, vone)


def _mla_attend(q_ref, k_ref, v_ref, acc_ref, m_ref):
    tk = k_ref.shape[0]
    for h in range(MLA_HEADS):
        hs = slice(h * HEAD_PAD, (h + 1) * HEAD_PAD)
        s = lax.dot_general(q_ref[:, hs], k_ref[:, hs], (((1,), (1,)), ((), ())),
                            preferred_element_type=F32)
        m_prev = m_ref[h]
        m_new = jnp.maximum(m_prev, jnp.max(s, axis=1, keepdims=True))
        alpha = jnp.exp2(m_prev - m_new)
        p = jnp.exp2((s - jnp.concatenate([m_new] * (tk // LANES), axis=1)).astype(BF16))
        pv = jnp.dot(p, v_ref[:, hs], preferred_element_type=F32)
        acc_ref[h] = acc_ref[h] * alpha + pv
        m_ref[h] = m_new


def _mla_finish(o_ref, acc_ref):
    lane = lax.broadcasted_iota(jnp.int32, acc_ref.shape[1:], 1)
    for hp in range(MLA_HEADS // 2):
        a0 = acc_ref[2 * hp]
        a1 = acc_ref[2 * hp + 1]
        l0 = jnp.sum(jnp.where(lane == MLA_V, a0, 0.0), axis=1, keepdims=True)
        l1 = jnp.sum(jnp.where(lane == 0, a1, 0.0), axis=1, keepdims=True)
        o = jnp.where(lane < MLA_V, a0 / l0, a1 / l1)
        o_ref[:, hp * LANES:(hp + 1) * LANES] = o.astype(o_ref.dtype)


def _mla_lat_kernel(q_ref, kc_ref, vc_ref, kl_ref, vl_ref, o_ref, acc_ref, m_ref):
    j = pl.program_id(2)

    @pl.when(j == 0)
    def _():
        m_ref[...] = jnp.full(m_ref.shape, NEG_BIG, F32)
        acc_ref[...] = jnp.zeros(acc_ref.shape, F32)
        _mla_attend(q_ref, kc_ref, vc_ref, acc_ref, m_ref)

    _mla_attend(q_ref, kl_ref, vl_ref, acc_ref, m_ref)

    @pl.when(j == pl.num_programs(2) - 1)
    def _():
        _mla_finish(o_ref, acc_ref)


def _mla_ctx_kernel(q_ref, kc_ref, vc_ref, o_ref, acc_ref, m_ref):
    m_ref[...] = jnp.full(m_ref.shape, NEG_BIG, F32)
    acc_ref[...] = jnp.zeros(acc_ref.shape, F32)
    _mla_attend(q_ref, kc_ref, vc_ref, acc_ref, m_ref)
    _mla_finish(o_ref, acc_ref)


def _mla_attention(q, k, v, B, S, NC, tq, tk):
    TA, W = q.shape
    TC = B * NC
    OW = MLA_HEADS * MLA_V
    nq, nk = S // tq, S // tk
    scratch = lambda t: [pltpu.VMEM((MLA_HEADS, t, HEAD_PAD), F32), pltpu.VMEM((MLA_HEADS, t, HEAD_PAD), F32)]
    kv_align = math.gcd(TC, tk)
    kv_spec = pl.BlockSpec((pl.Element(tk), pl.Element(W)),
                           lambda b, i, j: (pl.multiple_of(TC + b * S + j * tk, kv_align), 0))
    o_lat = pl.pallas_call(
        _mla_lat_kernel,
        grid=(B, nq, nk),
        in_specs=[pl.BlockSpec((tq, W), lambda b, i, j: (TC // tq + b * nq + i, 0)),
                  pl.BlockSpec((NC, W), lambda b, i, j: (b, 0)),
                  pl.BlockSpec((NC, W), lambda b, i, j: (b, 0)),
                  kv_spec, kv_spec],
        out_specs=pl.BlockSpec((tq, OW), lambda b, i, j: (b * nq + i, 0)),
        out_shape=jax.ShapeDtypeStruct((B * S, OW), BF16),
        scratch_shapes=scratch(tq),
        compiler_params=_cparams(("arbitrary", "arbitrary", "arbitrary")),
        name="mla_attn_lat",
    )(q, k, v, k, v)
    o_ctx = pl.pallas_call(
        _mla_ctx_kernel,
        grid=(B,),
        in_specs=[pl.BlockSpec((NC, W), lambda b: (b, 0))] * 3,
        out_specs=pl.BlockSpec((NC, OW), lambda b: (b, 0)),
        out_shape=jax.ShapeDtypeStruct((TC, OW), BF16),
        scratch_shapes=scratch(NC),
        compiler_params=_cparams(("arbitrary",)),
        name="mla_attn_ctx",
    )(q, k, v)
    return o_ctx, o_lat


def _nat_softmax_pv(scores, values):
    m = scores[0].max(axis=1, keepdims=True)
    for s in scores[1:]:
        m = jnp.maximum(m, s.max(axis=1, keepdims=True))
    l = None
    o = None
    for s, v in zip(scores, values):
        p = jnp.exp2(s - m)
        ls = jnp.sum(p, axis=1, keepdims=True)
        pv = jnp.dot(p.astype(BF16), v, preferred_element_type=F32)
        l = ls if l is None else l + ls
        o = pv if o is None else o + pv
    return o / l


def _nat_lat_kernel(q_ref, k0_ref, k1_ref, k2_ref, v0_ref, v1_ref, v2_ref, kc_ref, vc_ref, bias_ref, o_ref,
                    *, qscale):
    tq = q_ref.shape[0]
    ck = k0_ref.shape[0]
    lane = lax.broadcasted_iota(jnp.int32, (tq, LANES), 1)
    dn = (((1,), (1,)), ((), ()))
    for hp in range(NAT_HEADS // 2):
        ls = slice(hp * LANES, (hp + 1) * LANES)
        qp = q_ref[:, ls].astype(F32) * qscale
        keys = [r[:, ls] for r in (k0_ref, k1_ref, k2_ref, kc_ref)]
        vals = [r[:, ls] for r in (v0_ref, v1_ref, v2_ref, vc_ref)]
        outs = []
        for hh in range(2):
            mask = (lane < NAT_HEAD_DIM) if hh == 0 else (lane >= NAT_HEAD_DIM)
            qm = jnp.where(mask, qp, 0.0).astype(BF16)
            scores = []
            for c, kk in enumerate(keys):
                s = lax.dot_general(qm, kk, dn, preferred_element_type=F32)
                if c < 3:
                    s = s + bias_ref[2 * hp + hh, :, c * ck:(c + 1) * ck].astype(F32)
                scores.append(s)
            outs.append(_nat_softmax_pv(scores, vals))
        o_ref[:, ls] = jnp.where(lane < NAT_HEAD_DIM, outs[0], outs[1]).astype(o_ref.dtype)


def _nat_ctx_kernel(q_ref, kc_ref, vc_ref, o_ref, *, qscale):
    tq = q_ref.shape[0]
    lane = lax.broadcasted_iota(jnp.int32, (tq, LANES), 1)
    dn = (((1,), (1,)), ((), ()))
    for hp in range(NAT_HEADS // 2):
        ls = slice(hp * LANES, (hp + 1) * LANES)
        qp = q_ref[:, ls].astype(F32) * qscale
        outs = []
        for hh in range(2):
            mask = (lane < NAT_HEAD_DIM) if hh == 0 else (lane >= NAT_HEAD_DIM)
            qm = jnp.where(mask, qp, 0.0).astype(BF16)
            s = lax.dot_general(qm, kc_ref[:, ls], dn, preferred_element_type=F32)
            outs.append(_nat_softmax_pv([s], [vc_ref[:, ls]]))
        o_ref[:, ls] = jnp.where(lane < NAT_HEAD_DIM, outs[0], outs[1]).astype(o_ref.dtype)


def _nat_bias_table(rpb, rows_n):
    L = rpb.shape[0]
    qc = np.arange(GRID_W)
    cs = np.clip(qc - NAT_WIN_W // 2, 0, GRID_W - NAT_WIN_W)
    kc = np.arange(GRID_W)
    colvalid = (kc[None, :] >= cs[:, None]) & (kc[None, :] < cs[:, None] + NAT_WIN_W)
    dc = np.clip(kc[None, :] - qc[:, None] + NAT_WIN_W - 1, 0, 2 * NAT_WIN_W - 2)
    bt = jnp.where(colvalid, rpb[:, :, :, dc] * LOG2E, NEG_BIG)
    n_dr = 2 * NAT_WIN_H - 1
    bt = jnp.concatenate([bt, jnp.full((L, NAT_HEADS, 1, GRID_W, GRID_W), NEG_BIG, F32)], axis=2).astype(BF16)
    kh = NAT_WIN_H
    dr = np.full((3, NAT_ROWS, NAT_KROWS), n_dr, np.int32)
    for v, r0 in enumerate((0, 2 * NAT_ROWS, rows_n - NAT_ROWS)):
        kstart = int(np.clip(r0 - kh // 2, 0, rows_n - NAT_KROWS))
        for i in range(NAT_ROWS):
            r = r0 + i
            rs = int(np.clip(r - kh // 2, 0, rows_n - kh))
            for jj in range(NAT_KROWS):
                kr = kstart + jj
                if rs <= kr < rs + kh:
                    dr[v, i, jj] = kr - r + NAT_WIN_H - 1
    t = bt[:, :, dr]
    t = t.transpose(0, 2, 1, 3, 5, 4, 6)
    return t.reshape(L, 3, NAT_HEADS, NAT_ROWS * GRID_W, NAT_KROWS * GRID_W)


def _nat_attention(u, bias, layer, B, S, NC):
    TA = u.shape[0]
    TC = B * NC
    rows_n = S // GRID_W
    nblk = rows_n // NAT_ROWS
    tq = NAT_ROWS * GRID_W
    ck = tq
    qscale = float(NAT_HEAD_DIM ** -0.5) * LOG2E
    cq, ckk, cv = U_NQ // NAT_WIDTH, U_NK // NAT_WIDTH, U_NV // NAT_WIDTH
    base = lambda b: TC // tq + b * nblk

    def kstart_blk(i):
        return jnp.clip(i * NAT_ROWS - NAT_WIN_H // 2, 0, rows_n - NAT_KROWS) // NAT_ROWS

    def kv_spec(col, c):
        return pl.BlockSpec((ck, NAT_WIDTH), lambda b, i: (base(b) + kstart_blk(i) + c, col))

    def variant(i):
        return jnp.where(i == 0, 0, jnp.where(i == nblk - 1, 2, 1))

    o_lat = pl.pallas_call(
        functools.partial(_nat_lat_kernel, qscale=qscale),
        grid=(B, nblk),
        in_specs=[pl.BlockSpec((tq, NAT_WIDTH), lambda b, i: (base(b) + i, cq)),
                  kv_spec(ckk, 0), kv_spec(ckk, 1), kv_spec(ckk, 2),
                  kv_spec(cv, 0), kv_spec(cv, 1), kv_spec(cv, 2),
                  pl.BlockSpec((NC, NAT_WIDTH), lambda b, i: (b, ckk)),
                  pl.BlockSpec((NC, NAT_WIDTH), lambda b, i: (b, cv)),
                  pl.BlockSpec((None, None, NAT_HEADS, tq, NAT_KROWS * GRID_W),
                               lambda b, i: (layer, variant(i), 0, 0, 0))],
        out_specs=pl.BlockSpec((tq, NAT_WIDTH), lambda b, i: (b * nblk + i, 0)),
        out_shape=jax.ShapeDtypeStruct((B * S, NAT_WIDTH), BF16),
        compiler_params=_cparams(("arbitrary", "arbitrary")),
        name="nat_attn_lat",
    )(u, u, u, u, u, u, u, u, u, bias)
    o_ctx = pl.pallas_call(
        functools.partial(_nat_ctx_kernel, qscale=qscale),
        grid=(B,),
        in_specs=[pl.BlockSpec((NC, NAT_WIDTH), lambda b: (b, cq)),
                  pl.BlockSpec((NC, NAT_WIDTH), lambda b: (b, ckk)),
                  pl.BlockSpec((NC, NAT_WIDTH), lambda b: (b, cv))],
        out_specs=pl.BlockSpec((NC, NAT_WIDTH), lambda b: (b, 0)),
        out_shape=jax.ShapeDtypeStruct((TC, NAT_WIDTH), BF16),
        compiler_params=_cparams(("arbitrary",)),
        name="nat_attn_ctx",
    )(u, u, u)
    return o_ctx, o_lat


def _fnet_chan_kernel(f_ref, w_ref, o_ref):
    for g in range(FNET_GROUPS):
        gs = slice(g * FNET_GROUP_DIM, (g + 1) * FNET_GROUP_DIM)
        o_ref[:, gs] = jnp.dot(f_ref[:, gs], w_ref[...], preferred_element_type=F32).astype(o_ref.dtype)


def _fnet_seq_kernel(a_ref, r_ref, o_ref, acc_ref):
    k = pl.program_id(2)

    @pl.when(k == 0)
    def _():
        acc_ref[...] = jnp.zeros(acc_ref.shape, F32)

    acc_ref[...] += jnp.dot(a_ref[...], r_ref[...], preferred_element_type=F32)

    @pl.when(k == pl.num_programs(2) - 1)
    def _():
        o_ref[...] = acc_ref[...].astype(o_ref.dtype)


def _dft_tables(n, scale):
    j = jnp.arange(n, dtype=jnp.int32)
    idx = (j[:, None] * j[None, :]) & (n - 1)
    ang = idx.astype(F32) * (2.0 * math.pi / n)
    return jnp.cos(ang) * scale, jnp.sin(ang) * scale


def _seq_dft_matrix(n):
    c, s = _dft_tables(n, n ** -0.5)
    return jnp.concatenate([c, -s], axis=1).astype(BF16)


FFT_N2 = 128
FFT_NB = 16


def _fft_tables(n):
    n2 = FFT_N2
    n1 = n // n2
    j1 = jnp.arange(n1, dtype=jnp.int32)[:, None, None]
    k2 = jnp.arange(n2, dtype=jnp.int32)[None, :, None]
    j2 = jnp.arange(n2, dtype=jnp.int32)[None, None, :]
    ang = ((k2 * (j1 + n1 * j2)) & (n - 1)).astype(F32) * (2.0 * math.pi / n)
    s1 = n2 ** -0.5
    m1 = jnp.concatenate([jnp.cos(ang) * s1, -jnp.sin(ang) * s1], axis=1).astype(BF16)
    c1, sn1 = _dft_tables(n1, n1 ** -0.5)
    top = jnp.stack([c1, sn1], axis=2).reshape(n1, 2 * n1)
    bot = jnp.stack([-sn1, c1], axis=2).reshape(n1, 2 * n1)
    w2 = jnp.concatenate([top, bot], axis=0).astype(BF16)
    return m1, w2


def _load_slab(ref, i):
    rows = ref.shape[1] // FFT_NB
    parts = [ref[g, pl.ds(i, rows, stride=FFT_NB), :] for g in range(FNET_GROUPS)]
    return jnp.concatenate(parts, axis=1)


def _fft1_kernel(f_ref, m_ref, o_ref):
    n2 = m_ref.shape[2]
    for i in range(FFT_NB):
        r = jnp.dot(m_ref[i], _load_slab(f_ref, i).astype(BF16), preferred_element_type=F32)
        for g in range(FNET_GROUPS):
            gs = slice(g * FNET_GROUP_DIM, (g + 1) * FNET_GROUP_DIM)
            for p in range(2):
                for kb in range(n2 // FFT_NB):
                    src = p * n2 + kb * FFT_NB
                    dst = (2 * i + p) * FFT_NB
                    o_ref[g, kb, dst:dst + FFT_NB, :] = r[src:src + FFT_NB, gs]


def _fft2_kernel(b_ref, w_ref, cc_ref, cs_ref, o_ref):
    n1 = o_ref.shape[1] // FFT_NB
    for i in range(FFT_NB):
        x = jnp.dot(w_ref[...], _load_slab(b_ref, i).astype(BF16), preferred_element_type=F32).astype(BF16)
        for g in range(FNET_GROUPS):
            gs = slice(g * FNET_GROUP_DIM, (g + 1) * FNET_GROUP_DIM)
            y = (jnp.dot(x[:n1, gs], cc_ref[...], preferred_element_type=F32)
                 + jnp.dot(x[n1:, gs], cs_ref[...], preferred_element_type=F32))
            o_ref[g, pl.ds(i, n1, stride=FFT_NB), :] = y


def _fourier_mix_long(f_lat, m1, w2, w_chan, B, n):
    n2 = FFT_N2
    n1 = n // n2
    G, GD, NB = FNET_GROUPS, FNET_GROUP_DIM, FFT_NB
    bmid = pl.pallas_call(
        _fft1_kernel,
        grid=(B, n1 // NB),
        in_specs=[pl.BlockSpec((None, G, None, n2 * NB, GD), lambda b, j: (b, 0, j, 0, 0)),
                  pl.BlockSpec((NB, 2 * n2, n2), lambda b, j: (j, 0, 0))],
        out_specs=pl.BlockSpec((None, G, n2 // NB, 2 * NB * NB, GD), lambda b, j: (b, 0, 0, j, 0)),
        out_shape=jax.ShapeDtypeStruct((B, G, n2 // NB, 2 * n1 * NB, GD), F32),
        compiler_params=_cparams(("arbitrary", "arbitrary")),
        name="fnet_fft1",
    )(f_lat, m1)
    return pl.pallas_call(
        _fft2_kernel,
        grid=(B, n2 // NB),
        in_specs=[pl.BlockSpec((None, G, None, 2 * n1 * NB, GD), lambda b, j: (b, 0, j, 0, 0)),
                  pl.BlockSpec((2 * n1, 2 * n1), lambda b, j: (0, 0)),
                  pl.BlockSpec((GD, GD), lambda b, j: (0, 0)),
                  pl.BlockSpec((GD, GD), lambda b, j: (0, 1))],
        out_specs=pl.BlockSpec((None, G, None, n1 * NB, GD), lambda b, j: (b, 0, j, 0, 0)),
        out_shape=jax.ShapeDtypeStruct((B, G, n2 // NB, n1 * NB, GD), F32),
        compiler_params=_cparams(("arbitrary", "arbitrary")),
        name="fnet_fft2",
    )(bmid, w2, w_chan, w_chan)


def _fourier_mix(u, row0, B, n, a_mat, w_chan, ts, tm, tk):
    nt = n // ts
    rr = pl.pallas_call(
        _fnet_chan_kernel,
        grid=(B, nt, 2),
        in_specs=[pl.BlockSpec((ts, FNET_WIDTH), lambda b, i, p: (row0 // ts + b * nt + i, U_F // FNET_WIDTH)),
                  pl.BlockSpec((FNET_GROUP_DIM, FNET_GROUP_DIM), lambda b, i, p: (0, p))],
        out_specs=pl.BlockSpec((ts, FNET_WIDTH), lambda b, i, p: (p * nt + i, b)),
        out_shape=jax.ShapeDtypeStruct((2 * n, B * FNET_WIDTH), BF16),
        compiler_params=_cparams(("arbitrary", "arbitrary", "arbitrary")),
        name="fnet_chan",
    )(u, w_chan)
    tn = min(B * FNET_WIDTH, 2048)
    return pl.pallas_call(
        _fnet_seq_kernel,
        grid=(n // tm, (B * FNET_WIDTH) // tn, (2 * n) // tk),
        in_specs=[pl.BlockSpec((tm, tk), lambda i, j, k: (i, k)),
                  pl.BlockSpec((tk, tn), lambda i, j, k: (k, j))],
        out_specs=pl.BlockSpec((tm, tn), lambda i, j, k: (i, j)),
        out_shape=jax.ShapeDtypeStruct((n, B * FNET_WIDTH), BF16),
        scratch_shapes=[pltpu.VMEM((tm, tn), F32)],
        compiler_params=_cparams(("arbitrary", "arbitrary", "arbitrary")),
        name="fnet_seq",
    )(a_mat, rr)


def _merge_kernel(oml_ref, omc_ref, ofl_ref, ofc_ref, onl_ref, onc_ref, gm_ref, gf_ref, gn_ref, x_ref, mod_ref,
                  wm_ref, wf_ref, wn_ref, wo_ref, bo_ref, lg_ref, lb_ref, wr_ref, br_ref, xo_ref, h_ref, lo_ref,
                  *, alpha, nct):
    def gate(g_ref):
        return 1.0 / (1.0 + jnp.exp(-g_ref[...].astype(F32)))

    is_ctx = pl.program_id(0) < nct
    nkb, rows = ofl_ref.shape[1], ofl_ref.shape[2]
    ofl = jnp.concatenate(
        [jnp.concatenate([ofl_ref[g, kb, k1 * FFT_NB:(k1 + 1) * FFT_NB, :]
                          for k1 in range(rows // FFT_NB) for kb in range(nkb)], axis=0)
         for g in range(FNET_GROUPS)], axis=1)
    of = jnp.where(is_ctx, ofc_ref[...].astype(F32), ofl).astype(BF16)
    om = jnp.where(is_ctx, omc_ref[...], oml_ref[...])
    on = jnp.where(is_ctx, onc_ref[...], onl_ref[...])
    m = gate(gm_ref) * jnp.dot(om, wm_ref[...], preferred_element_type=F32)
    m = m + gate(gf_ref) * jnp.dot(of, wf_ref[...], preferred_element_type=F32)
    m = m + gate(gn_ref) * jnp.dot(on, wn_ref[...], preferred_element_type=F32)
    y = jnp.dot(m.astype(BF16), wo_ref[...], preferred_element_type=F32) + bo_ref[...]
    z = alpha * x_ref[...] + mod_ref[2:3, :] * y
    xn = _ln(z) * lg_ref[...] + lb_ref[...]
    xo_ref[...] = xn
    h = (_ln(xn) * (1.0 + mod_ref[4:5, :]) + mod_ref[3:4, :]).astype(BF16)
    h_ref[0] = h
    h_ref[1] = h
    lo_ref[...] = jnp.dot(h, wr_ref[...], preferred_element_type=F32) + br_ref[...]


def _merge(om_lat, om_ctx, of_lat, of_ctx, on_lat, on_ctx, u, xall, mod, mod_row, wm, wf, wn, wo, bo, lg, lb, wr, br,
           alpha, B, S, NC):
    TA, D = xall.shape
    TC = B * NC
    nct = TC // TM
    npb = S // TM
    const = lambda i: (0, 0)
    row = lambda i: (i, 0)
    gcol = U_G // D

    lat = lambda i: jnp.maximum(i - nct, 0)
    lat_row = lambda i: (lat(i), 0)
    ctx_row = lambda i: (jnp.minimum(i, nct - 1), 0)
    outs = pl.pallas_call(
        functools.partial(_merge_kernel, alpha=alpha, nct=nct),
        grid=(TA // TM,),
        in_specs=[pl.BlockSpec((TM, MLA_HEADS * MLA_V), lat_row),
                  pl.BlockSpec((TM, MLA_HEADS * MLA_V), ctx_row),
                  pl.BlockSpec((None, FNET_GROUPS, FFT_N2 // FFT_NB, (TM // FFT_N2) * FFT_NB, FNET_GROUP_DIM),
                               lambda i: (lat(i) // npb, 0, 0, lat(i) % npb, 0)),
                  pl.BlockSpec((TM, FNET_WIDTH), ctx_row),
                  pl.BlockSpec((TM, NAT_WIDTH), lat_row),
                  pl.BlockSpec((TM, NAT_WIDTH), ctx_row),
                  pl.BlockSpec((TM, D), lambda i: (i, gcol)),
                  pl.BlockSpec((TM, D), lambda i: (i, gcol + 1)),
                  pl.BlockSpec((TM, D), lambda i: (i, gcol + 2)),
                  pl.BlockSpec((TM, D), row),
                  pl.BlockSpec((None, 8, D), lambda i: (mod_row(i), 0, 0)),
                  pl.BlockSpec((MLA_HEADS * MLA_V, D), const),
                  pl.BlockSpec((FNET_WIDTH, D), const),
                  pl.BlockSpec((NAT_WIDTH, D), const),
                  pl.BlockSpec((D, D), const),
                  pl.BlockSpec((1, D), const),
                  pl.BlockSpec((1, D), const),
                  pl.BlockSpec((1, D), const),
                  pl.BlockSpec((D, LANES), const),
                  pl.BlockSpec((1, LANES), const)],
        out_specs=[pl.BlockSpec((TM, D), row), pl.BlockSpec((2, TM, D), lambda i: (0, i, 0)),
                   pl.BlockSpec((TM, LANES), row)],
        out_shape=[jax.ShapeDtypeStruct((TA, D), F32), jax.ShapeDtypeStruct((2, TA, D), BF16),
                   jax.ShapeDtypeStruct((TA, LANES), F32)],
        compiler_params=_cparams(("arbitrary",)),
        name="merge",
    )(om_lat, om_ctx, of_lat, of_ctx, on_lat, on_ctx, u, u, u, xall, mod, wm, wf, wn, wo, bo, lg, lb, wr, br)
    return outs


def _router_kernel(lg_ref, e_ref, w_ref, r_ref, cnt_ref, carry_ref):
    lg = lg_ref[...]
    lane = lax.broadcasted_iota(jnp.int32, lg.shape, 1)
    big = jnp.int32(1 << 20)
    is_g = lane < N_GROUPS
    gl = jnp.where(is_g, lg, -jnp.inf)
    ge = jnp.exp(gl - jnp.max(gl, axis=1, keepdims=True))
    p = ge / jnp.sum(ge, axis=1, keepdims=True)
    p_top = jnp.max(p, axis=1, keepdims=True)
    g_idx = jnp.min(jnp.where(is_g & (p == p_top), lane, big), axis=1, keepdims=True)
    lo = N_GROUPS + g_idx * EXPERTS_PER_GROUP
    in_grp = (lane >= lo) & (lane < lo + EXPERTS_PER_GROUP)
    el = jnp.where(in_grp, lg, -jnp.inf)
    m1 = jnp.max(el, axis=1, keepdims=True)
    i1 = jnp.min(jnp.where(in_grp & (el == m1), lane, big), axis=1, keepdims=True)
    rest = in_grp & (lane != i1)
    el2 = jnp.where(rest, lg, -jnp.inf)
    m2 = jnp.max(el2, axis=1, keepdims=True)
    i2 = jnp.min(jnp.where(rest & (el2 == m2), lane, big), axis=1, keepdims=True)
    d = jnp.exp(m2 - m1)
    w1 = p_top * (1.0 / (1.0 + d))
    w2 = p_top * (d / (1.0 + d))
    e_ref[...] = jnp.where(lane == 0, i1 - N_GROUPS, jnp.where(lane == 1, i2 - N_GROUPS, 0))
    w_ref[...] = jnp.where(lane == 0, w1, jnp.where(lane == 1, w2, 0.0))

    @pl.when(pl.program_id(0) == 0)
    def _():
        carry_ref[...] = jnp.zeros(carry_ref.shape, F32)

    tm = lg.shape[0]
    oh1 = lane == i1
    oh2 = lane == i2
    oh = jnp.where(oh1, 1.0, jnp.where(oh2, 1.0, 0.0))
    rr = lax.broadcasted_iota(jnp.int32, (tm, tm), 0)
    cc = lax.broadcasted_iota(jnp.int32, (tm, tm), 1)
    lower = jnp.where(cc < rr, 1.0, 0.0).astype(BF16)
    before = jnp.dot(lower, oh.astype(BF16), preferred_element_type=F32) + carry_ref[0:1, :]
    r1 = jnp.sum(jnp.where(oh1, before, 0.0), axis=1, keepdims=True)
    r2 = jnp.sum(jnp.where(oh2, before, 0.0), axis=1, keepdims=True)
    r_ref[...] = jnp.where(lane == 0, r1, jnp.where(lane == 1, r2, 0.0)).astype(jnp.int32)
    carry_ref[...] = carry_ref[...] + jnp.sum(oh, axis=0, keepdims=True)
    cnt_ref[...] = carry_ref[...].astype(jnp.int32)


def _router(logits):
    TA = logits.shape[0]
    row = lambda i: (i, 0)
    return pl.pallas_call(
        _router_kernel,
        grid=(TA // TM,),
        in_specs=[pl.BlockSpec((TM, LANES), row)],
        out_specs=[pl.BlockSpec((TM, LANES), row)] * 3 + [pl.BlockSpec((8, LANES), lambda i: (0, 0))],
        out_shape=[jax.ShapeDtypeStruct((TA, LANES), jnp.int32), jax.ShapeDtypeStruct((TA, LANES), F32),
                   jax.ShapeDtypeStruct((TA, LANES), jnp.int32), jax.ShapeDtypeStruct((8, LANES), jnp.int32)],
        scratch_shapes=[pltpu.VMEM((8, LANES), F32)],
        compiler_params=_cparams(("arbitrary",)),
        name="router",
    )(logits)


def _moe_kernel(be_ref, nu_ref, x_ref, wg_ref, wu_ref, wd_ref, o_ref, wgb_ref, wub_ref, wdb_ref):
    i = pl.program_id(0)
    changed = jnp.logical_or(i == 0, be_ref[i] != be_ref[jnp.maximum(i - 1, 0)])

    @pl.when(changed)
    def _():
        wgb_ref[...] = wg_ref[...].astype(BF16)
        wub_ref[...] = wu_ref[...].astype(BF16)
        wdb_ref[...] = wd_ref[...].astype(BF16)

    @pl.when(i < nu_ref[0])
    def _():
        x = x_ref[...]
        a = jnp.dot(x, wgb_ref[...], preferred_element_type=F32)
        b = jnp.dot(x, wub_ref[...], preferred_element_type=F32)
        hmid = (a * (1.0 / (1.0 + jnp.exp(-a))) * b).astype(BF16)
        o_ref[...] = jnp.dot(hmid, wdb_ref[...], preferred_element_type=F32).astype(o_ref.dtype)

    @pl.when(i >= nu_ref[0])
    def _():
        o_ref[...] = jnp.zeros(o_ref.shape, o_ref.dtype)


def _moe_experts(xb, block_e, n_used, w_gate, w_up, w_down, layer):
    n_rows, D = xb.shape
    nb = n_rows // MOE_BM
    DE = w_gate.shape[-1]
    xrow = lambda i, be, nu: (jnp.minimum(i, nu[0] - 1), 0)
    wmap = lambda i, be, nu: (layer, be[i], 0, 0)
    grid_spec = pltpu.PrefetchScalarGridSpec(
        num_scalar_prefetch=2,
        grid=(nb,),
        in_specs=[pl.BlockSpec((MOE_BM, D), xrow),
                  pl.BlockSpec((None, None, D, DE), wmap),
                  pl.BlockSpec((None, None, D, DE), wmap),
                  pl.BlockSpec((None, None, DE, D), wmap)],
        out_specs=pl.BlockSpec((MOE_BM, D), lambda i, be, nu: (i, 0)),
        scratch_shapes=[pltpu.VMEM((D, DE), BF16), pltpu.VMEM((D, DE), BF16), pltpu.VMEM((DE, D), BF16)],
    )
    return pl.pallas_call(
        _moe_kernel,
        grid_spec=grid_spec,
        out_shape=jax.ShapeDtypeStruct((n_rows, D), BF16),
        compiler_params=_cparams(("arbitrary",)),
        name="moe_experts",
    )(block_e, n_used, xb, w_gate, w_up, w_down)


def _ffn_out_kernel(x_ref, y0_ref, y1_ref, w_ref, mod_ref, lg_ref, lb_ref, o_ref, *, alpha):
    w = w_ref[...]
    f = y0_ref[...].astype(F32) * w[:, 0:1] + y1_ref[...].astype(F32) * w[:, 1:2]
    z = alpha * x_ref[...] + mod_ref[5:6, :] * f
    o_ref[...] = _ln(z) * lg_ref[...] + lb_ref[...]


def _ffn_out(xall, y2, w_tok, mod, mod_row, lg, lb, alpha, first_tile=0):
    TA, D = xall.shape
    nt = TA // TM
    row = lambda i: (i + first_tile, 0)
    const = lambda i: (0, 0)
    return pl.pallas_call(
        functools.partial(_ffn_out_kernel, alpha=alpha),
        grid=(nt - first_tile,),
        in_specs=[pl.BlockSpec((TM, D), row), pl.BlockSpec((TM, D), row),
                  pl.BlockSpec((TM, D), lambda i: (nt + i + first_tile, 0)),
                  pl.BlockSpec((TM, LANES), row),
                  pl.BlockSpec((None, 8, D), lambda i: (mod_row(i + first_tile), 0, 0)),
                  pl.BlockSpec((1, D), const), pl.BlockSpec((1, D), const)],
        out_specs=pl.BlockSpec((TM, D), lambda i: (i, 0)),
        out_shape=jax.ShapeDtypeStruct((TA - first_tile * TM, D), F32),
        compiler_params=_cparams(("arbitrary",)),
        name="ffn_out",
    )(xall, y2, y2, w_tok, mod, lg, lb)


def _reorder_w_in(w_in, b_in):
    D = w_in.shape[0]
    o_cq, o_ckv, o_kr = 0, MLA_Q_LORA, MLA_Q_LORA + MLA_KV_LORA
    o_f = o_kr + MLA_ROPE
    o_nq = o_f + FNET_WIDTH
    o_g = o_nq + 3 * NAT_WIDTH
    rot_src, rot_sign = _rope_partner()

    def build(a):
        kr = a[..., o_kr:o_kr + MLA_ROPE]
        kr_rot = kr[..., rot_src] * rot_sign
        pad = jnp.zeros(a.shape[:-1] + (LANES - 2 * MLA_ROPE,), a.dtype)
        return jnp.concatenate([a[..., o_f:o_g], a[..., o_g:], a[..., o_cq:o_kr], kr, kr_rot, pad], axis=-1)

    return build(w_in).astype(BF16), build(b_in[None, :])


def _rope_partner():
    q = MLA_ROPE // 4
    src = np.zeros(MLA_ROPE, np.int32)
    sign = np.zeros(MLA_ROPE, np.float32)
    for half in range(2):
        o = half * 2 * q
        for i in range(q):
            src[o + i], sign[o + i] = o + q + i, -1.0
            src[o + q + i], sign[o + q + i] = o + i, 1.0
    return src, sign


def _pad_heads(w, width, offset=0):
    K = w.shape[0]
    w = w.reshape(K, MLA_HEADS, width)
    out = jnp.zeros((K, MLA_HEADS, HEAD_PAD), w.dtype)
    return out.at[:, :, offset:offset + width].set(w).reshape(K, MLA_HEADS * HEAD_PAD)


def _mla_weights(w_uq, w_uk, w_uv):
    rot_src, rot_sign = _rope_partner()
    dq = MLA_NOPE + MLA_ROPE
    wq = w_uq.reshape(-1, MLA_HEADS, dq)
    wq_pe_rot = wq[:, :, MLA_NOPE:][:, :, rot_src] * rot_sign
    wqa = _pad_heads(wq.reshape(-1, MLA_HEADS * dq), dq)
    wqb = _pad_heads(wq_pe_rot.reshape(-1, MLA_HEADS * MLA_ROPE), MLA_ROPE, MLA_NOPE)
    wk = _pad_heads(w_uk, MLA_NOPE)
    K = w_uv.shape[0]
    wv3 = w_uv.reshape(K, MLA_HEADS // 2, 2, MLA_V)
    wv = jnp.zeros((K, MLA_HEADS // 2, 2, HEAD_PAD), w_uv.dtype)
    wv = wv.at[:, :, 0, :MLA_V].set(wv3[:, :, 0]).at[:, :, 1, MLA_V:].set(wv3[:, :, 1])
    wv = wv.reshape(K, MLA_HEADS * HEAD_PAD)
    vone = np.zeros((MLA_HEADS // 2, 2, HEAD_PAD), np.float32)
    vone[:, 0, MLA_V] = 1.0
    vone[:, 1, 0] = 1.0
    pa = np.zeros((LANES, MLA_HEADS, HEAD_PAD), np.float32)
    pb = np.zeros((LANES, MLA_HEADS, HEAD_PAD), np.float32)
    for i in range(MLA_ROPE):
        pa[i, :, MLA_NOPE + i] = 1.0
        pb[MLA_ROPE + i, :, MLA_NOPE + i] = 1.0
    W = MLA_HEADS * HEAD_PAD
    return (wqa.astype(BF16), wqb.astype(BF16), wk.astype(BF16), wv.astype(BF16),
            jnp.asarray(pa.reshape(LANES, W), BF16), jnp.asarray(pb.reshape(LANES, W), BF16),
            jnp.asarray(vone.reshape(1, W)))


def _rope_lane_tables(S):
    t = jnp.arange(S)
    rows = (t // GRID_W).astype(F32)
    cols = (t % GRID_W).astype(F32)
    n_freq = MLA_ROPE // 4
    inv = jnp.power(ROPE_BASE, -jnp.arange(n_freq, dtype=F32) / n_freq)
    ar = rows[:, None] * inv[None, :]
    ac = cols[:, None] * inv[None, :]
    ones = jnp.ones((S, MLA_NOPE), F32)
    zpad = jnp.zeros((S, HEAD_PAD - MLA_NOPE - MLA_ROPE), F32)
    cos = jnp.concatenate([ones, jnp.cos(ar), jnp.cos(ar), jnp.cos(ac), jnp.cos(ac), zpad], axis=1)
    sin = jnp.concatenate([0 * ones, jnp.sin(ar), jnp.sin(ar), jnp.sin(ac), jnp.sin(ac), zpad], axis=1)
    idc = jnp.concatenate([jnp.ones((TM, MLA_NOPE + MLA_ROPE), F32), jnp.zeros((TM, HEAD_PAD - MLA_NOPE - MLA_ROPE), F32)], axis=1)
    return jnp.concatenate([idc, cos], axis=0), jnp.concatenate([jnp.zeros((TM, HEAD_PAD), F32), sin], axis=0)


def _moe_plan(e_ids, ranks, counts, n_blocks):
    t = e_ids.shape[0]
    a = 2 * t
    padded = (counts + MOE_BM - 1) // MOE_BM * MOE_BM
    pends = jnp.cumsum(padded)
    pstarts = pends - padded
    eo = jnp.arange(N_EXPERTS, dtype=jnp.int32)
    pos = ranks + jnp.sum(jnp.where(e_ids[:, :, None] == eo, pstarts, 0), axis=-1)
    blk0 = jnp.arange(n_blocks, dtype=jnp.int32) * MOE_BM
    block_e = jnp.minimum(jnp.sum((pends[None, :] <= blk0[:, None]).astype(jnp.int32), axis=1), N_EXPERTS - 1)
    n_used = (pends[-1] // MOE_BM).astype(jnp.int32).reshape(1)
    nd = n_blocks * MOE_BM - a
    dcum = jnp.cumsum(padded - counts)
    d_e = jnp.sum((dcum[None, :] <= jnp.arange(nd, dtype=jnp.int32)[:, None]).astype(jnp.int32), axis=1)
    keys = jnp.concatenate([e_ids.reshape(a) * 2, d_e * 2 + 1])
    payload = jnp.concatenate([jnp.arange(a, dtype=jnp.int32), (jnp.arange(nd, dtype=jnp.int32) * 2) % a])
    _, row_asg = lax.sort((keys, payload), num_keys=1, is_stable=True)
    row_src = (row_asg % 2) * t + row_asg // 2
    return row_src, pos, block_e, n_used


def kernel(x, c, ctx, c_ctx, w_ada, b_ada, w_in, b_in, q_norm_g, kv_norm_g, w_uq, w_uk, w_uv, rpb, w_o_mla, w_o_fnet, w_o_nat, w_out, b_out, ln1_g, ln1_b, w_rg, b_rg, w_re, b_re, w_gate_e, w_up_e, w_down_e, ln2_g, ln2_b):
    B, S, D = x.shape
    NC = ctx.shape[1]
    L = w_ada.shape[0]
    TC, TL = B * NC, B * S
    TA = TC + TL
    assert TC % TM == 0 and S % TM == 0 and NC % 128 == 0 and B < 8
    alpha = float((2 * L) ** 0.25)
    nct, npb = TC // TM, S // TM

    def mod_row(i):
        return jnp.where(i < nct, B, jnp.maximum(i - nct, 0) // npb)

    def tab_row(i):
        return jnp.where(i < nct, 0, 1 + jnp.maximum(i - nct, 0) % npb)

    cc = jnp.concatenate([c, c_ctx[None, :], jnp.zeros((8 - B - 1, D), F32)], axis=0)
    mod_all = _ada_all(cc, w_ada, b_ada).reshape(L, 8, 6, D)
    mod_all = jnp.concatenate([mod_all, jnp.zeros((L, 8, 2, D), F32)], axis=2)

    tabc, tabs = _rope_lane_tables(S)
    chan_scale = FNET_GROUP_DIM ** -0.5
    cch, sch = _dft_tables(FNET_GROUP_DIM, chan_scale)
    w_chan = jnp.concatenate([cch, sch], axis=1).astype(BF16)
    assert S % (FFT_N2 * FFT_NB) == 0
    m1, w2 = _fft_tables(S)
    a_ctx = _seq_dft_matrix(NC)
    nat_bias = _nat_bias_table(rpb, S // GRID_W)
    qscale = float((MLA_NOPE + MLA_ROPE) ** -0.5) * LOG2E
    rows_n = S // GRID_W
    n_blocks = (2 * TA) // MOE_BM + N_EXPERTS
    tq = math.gcd(1024, math.gcd(S, TC))
    tk = math.gcd(2048, S)

    xall = jnp.concatenate([ctx.reshape(TC, D), x.reshape(TL, D)], axis=0)
    for l in range(L):
        mod = mod_all[l]
        w_in_r, b_in_r = _reorder_w_in(w_in[l], b_in[l])
        u, f_lat = _in_proj(xall, mod, w_in_r, b_in_r, mod_row, B, S, nct)

        wqa, wqb, wk, wv, pa, pb, vone = _mla_weights(w_uq[l], w_uk[l], w_uv[l])
        q, k, v = _qkv(u, tabc, tabs, tab_row, q_norm_g[l][None, :], kv_norm_g[l][None, :],
                       wqa, wqb, wk, wv, pa, pb, vone, qscale)
        om_ctx, om_lat = _mla_attention(q, k, v, B, S, NC, tq, tk)

        on_ctx, on_lat = _nat_attention(u, nat_bias, l, B, S, NC)

        of_lat = _fourier_mix_long(f_lat, m1, w2, w_chan, B, S)
        of_ctx = _fourier_mix(u, 0, B, NC, a_ctx, w_chan, ts=NC, tm=NC, tk=2 * NC)
        of_ctx = of_ctx.reshape(NC, B, FNET_WIDTH).transpose(1, 0, 2).reshape(TC, FNET_WIDTH)

        w_r = jnp.concatenate([w_rg[l], w_re[l], jnp.zeros((D, LANES - N_GROUPS - N_EXPERTS), F32)], axis=1).astype(BF16)
        b_r = jnp.concatenate([b_rg[l], b_re[l], jnp.zeros((LANES - N_GROUPS - N_EXPERTS,), F32)])[None, :]
        xall, h_f, logits = _merge(om_lat, om_ctx, of_lat, of_ctx, on_lat, on_ctx, u, xall, mod, mod_row,
                                   w_o_mla[l].astype(BF16), w_o_fnet[l].astype(BF16), w_o_nat[l].astype(BF16),
                                   w_out[l].astype(BF16), b_out[l][None, :], ln1_g[l][None, :], ln1_b[l][None, :],
                                   w_r, b_r, alpha, B, S, NC)

        e_out, w_tok, r_out, cnt = _router(logits)
        counts = cnt[0, N_GROUPS:N_GROUPS + N_EXPERTS]
        row_src, pos, block_e, n_used = _moe_plan(e_out[:, :2], r_out[:, :2], counts, n_blocks)
        xb = jnp.take(h_f.reshape(2 * TA, D), row_src, axis=0, mode="clip")
        yb = _moe_experts(xb, block_e, n_used, w_gate_e, w_up_e, w_down_e, l)
        y2 = jnp.take(yb, jnp.concatenate([pos[:, 0], pos[:, 1]]), axis=0, mode="clip")
        xall = _ffn_out(xall, y2, w_tok, mod, mod_row, ln2_g[l][None, :], ln2_b[l][None, :], alpha,
                        first_tile=nct if l == L - 1 else 0)
    return xall.reshape(B, S, D)
```

```python
import functools
import math

import numpy as np
import jax
import jax.numpy as jnp
from jax import lax
from jax.experimental import pallas as pl
from jax.experimental.pallas import tpu as pltpu

F32 = jnp.float32
BF16 = jnp.bfloat16

GRID_W = 64
MLA_HEADS = 8
MLA_Q_LORA = 256
MLA_KV_LORA = 128
MLA_NOPE = 64
MLA_ROPE = 32
MLA_V = 64
ROPE_BASE = 10000.0
FNET_GROUPS = 4
FNET_GROUP_DIM = 128
FNET_WIDTH = FNET_GROUPS * FNET_GROUP_DIM
NAT_HEADS = 8
NAT_HEAD_DIM = 64
NAT_WIDTH = NAT_HEADS * NAT_HEAD_DIM
NAT_WIN_H = 8
NAT_WIN_W = 16
N_GROUPS = 4
EXPERTS_PER_GROUP = 8
N_EXPERTS = N_GROUPS * EXPERTS_PER_GROUP
D_EXPERT = 512
LN_EPS = 1e-6

LANES = 128
HEAD_PAD = 128
LOG2E = 1.4426950408889634
NEG_BIG = -1e30
VMEM_LIMIT = 56 * 1024 * 1024

U_F, U_NQ, U_NK, U_NV, U_G, U_CQ, U_CKV, U_KR, U_W = 0, 512, 1024, 1536, 2048, 5120, 5376, 5504, 5632

TM = 512
NAT_ROWS = 4
NAT_KROWS = 12
MOE_BM = 512


def _cparams(sem):
    return pltpu.CompilerParams(dimension_semantics=sem, vmem_limit_bytes=VMEM_LIMIT)


def _ln(x):
    mu = jnp.mean(x, axis=-1, keepdims=True)
    xc = x - mu
    var = jnp.mean(xc * xc, axis=-1, keepdims=True)
    return xc * lax.rsqrt(var + LN_EPS)


def _ada_kernel(c_ref, w_ref, b_ref, o_ref):
    c = c_ref[...]
    a = c * (1.0 / (1.0 + jnp.exp(-c)))
    o_ref[...] = jnp.dot(a, w_ref[...], preferred_element_type=F32,
                         precision=lax.Precision.HIGHEST) + b_ref[...]


def _ada_all(cc, w_ada, b_ada):
    L, D, N = w_ada.shape
    tn = 1536
    return pl.pallas_call(
        _ada_kernel,
        grid=(L, N // tn),
        in_specs=[pl.BlockSpec((8, D), lambda l, j: (0, 0)),
                  pl.BlockSpec((None, D, tn), lambda l, j: (l, 0, j)),
                  pl.BlockSpec((None, 1, tn), lambda l, j: (l, 0, j))],
        out_specs=pl.BlockSpec((None, 8, tn), lambda l, j: (l, 0, j)),
        out_shape=jax.ShapeDtypeStruct((L, 8, N), F32),
        compiler_params=_cparams(("arbitrary", "arbitrary")),
        name="ada_mod",
    )(cc, w_ada, b_ada.reshape(L, 1, N))


def _in_proj_kernel(x_ref, mod_ref, w_ref, b_ref, o_ref, f_ref, *, cw):
    _in_proj_body(x_ref[...], mod_ref, w_ref, b_ref, o_ref, f_ref, cw)


def _ffn_in_proj_kernel(x_ref, y0_ref, y1_ref, wt_ref, modp_ref, lg_ref, lb_ref, mod_ref, w_ref, b_ref,
                        xo_ref, o_ref, f_ref, *, alpha, cw):
    xn = _ffn_combine(x_ref, y0_ref, y1_ref, wt_ref, modp_ref, lg_ref, lb_ref, alpha)
    xo_ref[...] = xn
    _in_proj_body(xn, mod_ref, w_ref, b_ref, o_ref, f_ref, cw)


def _in_proj_body(x, mod_ref, w_ref, b_ref, o_ref, f_ref, cw):
    y = _ln(x)
    h = (y * (1.0 + mod_ref[1:2, :]) + mod_ref[0:1, :]).astype(BF16)
    for j in range(o_ref.shape[1] // cw):
        sl = slice(j * cw, (j + 1) * cw)
        acc = jnp.dot(h, w_ref[:, sl], preferred_element_type=F32) + b_ref[:, sl]
        o_ref[:, sl] = acc.astype(o_ref.dtype)
        if j == U_F // cw:
            nbk = f_ref.shape[1]
            n1 = nbk * FFT_NB
            for g in range(FNET_GROUPS):
                for jb in range(nbk):
                    for jj in range(acc.shape[0] // n1):
                        src = jj * n1 + jb * FFT_NB
                        f_ref[g, jb, jj * FFT_NB:(jj + 1) * FFT_NB, :] = (
                            acc[src:src + FFT_NB, g * FNET_GROUP_DIM:(g + 1) * FNET_GROUP_DIM])


def _in_proj(xall, mod, w_in_r, b_in_r, mod_row, B, S, nct, ffn=None):
    TA, D = xall.shape
    N = w_in_r.shape[1]
    n1 = S // FFT_N2
    npb = S // TM
    nt = TA // TM
    lat = lambda i: jnp.maximum(i - nct, 0)
    row = lambda i: (i, 0)
    const = lambda i: (0, 0)
    mod_spec = pl.BlockSpec((None, 8, D), lambda i: (mod_row(i), 0, 0))
    f_spec = pl.BlockSpec((None, FNET_GROUPS, n1 // FFT_NB, (TM // n1) * FFT_NB, FNET_GROUP_DIM),
                          lambda i: (lat(i) // npb, 0, 0, lat(i) % npb, 0))
    proj_specs = [mod_spec, pl.BlockSpec((D, N), const), pl.BlockSpec((1, N), const)]
    outs = [jax.ShapeDtypeStruct((TA, N), BF16),
            jax.ShapeDtypeStruct((B, FNET_GROUPS, n1 // FFT_NB, FFT_N2 * FFT_NB, FNET_GROUP_DIM), F32)]
    out_specs = [pl.BlockSpec((TM, N), row), f_spec]
    if ffn is None:
        return pl.pallas_call(
            functools.partial(_in_proj_kernel, cw=512),
            grid=(nt,),
            in_specs=[pl.BlockSpec((TM, D), row)] + proj_specs,
            out_specs=out_specs,
            out_shape=outs,
            compiler_params=_cparams(("arbitrary",)),
            name="in_proj",
        )(xall, mod, w_in_r, b_in_r)
    y2, w_tok, mod_prev, lg, lb, alpha = ffn
    xn, u, f_lat = pl.pallas_call(
        functools.partial(_ffn_in_proj_kernel, alpha=alpha, cw=512),
        grid=(nt,),
        in_specs=[pl.BlockSpec((TM, D), row), pl.BlockSpec((TM, D), row), pl.BlockSpec((TM, D), lambda i: (nt + i, 0)),
                  pl.BlockSpec((TM, LANES), row), mod_spec, pl.BlockSpec((1, D), const), pl.BlockSpec((1, D), const)]
                 + proj_specs,
        out_specs=[pl.BlockSpec((TM, D), row)] + out_specs,
        out_shape=[jax.ShapeDtypeStruct((TA, D), F32)] + outs,
        compiler_params=_cparams(("arbitrary",)),
        name="ffn_in_proj",
    )(xall, y2, y2, w_tok, mod_prev, lg, lb, mod, w_in_r, b_in_r)
    return xn, u, f_lat


def _rms(x, g):
    return x * lax.rsqrt(jnp.mean(x * x, axis=-1, keepdims=True) + LN_EPS) * g


def _qkv_kernel(cq_ref, ckv_ref, kr_ref, tc_ref, ts_ref, qg_ref, kvg_ref, wqa_ref, wqb_ref,
                wk_ref, wv_ref, pa_ref, pb_ref, vone_ref, q_ref, k_ref, v_ref, *, qscale):
    cqn = _rms(cq_ref[...].astype(F32), qg_ref[...]).astype(BF16)
    ckvn = _rms(ckv_ref[...].astype(F32), kvg_ref[...]).astype(BF16)
    cos = jnp.concatenate([tc_ref[...]] * MLA_HEADS, axis=1)
    sin = jnp.concatenate([ts_ref[...]] * MLA_HEADS, axis=1)
    qa = jnp.dot(cqn, wqa_ref[...], preferred_element_type=F32)
    qb = jnp.dot(cqn, wqb_ref[...], preferred_element_type=F32)
    q_ref[...] = ((qa * cos + qb * sin) * qscale).astype(q_ref.dtype)
    kr = kr_ref[...]
    ka = jnp.dot(kr, pa_ref[...], preferred_element_type=F32)
    kb = jnp.dot(kr, pb_ref[...], preferred_element_type=F32)
    kn = jnp.dot(ckvn, wk_ref[...], preferred_element_type=F32)
    k_ref[...] = (kn + ka * cos + kb * sin).astype(k_ref.dtype)
    v = jnp.dot(ckvn, wv_ref[...], preferred_element_type=F32) + vone_ref[...]
    v_ref[...] = v.astype(v_ref.dtype)


def _qkv(u, tabc, tabs, tab_row, qg, kvg, wqa, wqb, wk, wv, pa, pb, vone, qscale):
    TA = u.shape[0]
    W = MLA_HEADS * HEAD_PAD
    const = lambda i: (0, 0)
    out = jax.ShapeDtypeStruct((TA, W), BF16)
    return pl.pallas_call(
        functools.partial(_qkv_kernel, qscale=qscale),
        grid=(TA // TM,),
        in_specs=[pl.BlockSpec((TM, MLA_Q_LORA), lambda i: (i, U_CQ // MLA_Q_LORA)),
                  pl.BlockSpec((TM, MLA_KV_LORA), lambda i: (i, U_CKV // MLA_KV_LORA)),
                  pl.BlockSpec((TM, LANES), lambda i: (i, U_KR // LANES)),
                  pl.BlockSpec((TM, LANES), lambda i: (tab_row(i), 0)),
                  pl.BlockSpec((TM, LANES), lambda i: (tab_row(i), 0)),
                  pl.BlockSpec((1, MLA_Q_LORA), const),
                  pl.BlockSpec((1, MLA_KV_LORA), const),
                  pl.BlockSpec((MLA_Q_LORA, W), const),
                  pl.BlockSpec((MLA_Q_LORA, W), const),
                  pl.BlockSpec((MLA_KV_LORA, W), const),
                  pl.BlockSpec((MLA_KV_LORA, W), const),
                  pl.BlockSpec((LANES, W), const),
                  pl.BlockSpec((LANES, W), const),
                  pl.BlockSpec((1, W), const)],
        out_specs=[pl.BlockSpec((TM, W), lambda i: (i, 0))] * 3,
        out_shape=[out, out, out],
        compiler_params=_cparams(("arbitrary",)),
        name="mla_qkv",
    )(u, u, u, tabc, tabs, qg, kvg, wqa, wqb, wk, wv, pa, pb, vone)


def _mla_attend(q_ref, k_ref, v_ref, acc_ref, m_ref):
    tk = k_ref.shape[0]
    for h in range(MLA_HEADS):
        hs = slice(h * HEAD_PAD, (h + 1) * HEAD_PAD)
        s = lax.dot_general(q_ref[:, hs], k_ref[:, hs], (((1,), (1,)), ((), ())),
                            preferred_element_type=F32)
        m_prev = m_ref[h]
        m_new = jnp.maximum(m_prev, jnp.max(s, axis=1, keepdims=True))
        alpha = jnp.exp2(m_prev - m_new)
        p = jnp.exp2((s - jnp.concatenate([m_new] * (tk // LANES), axis=1)).astype(BF16))
        pv = jnp.dot(p, v_ref[:, hs], preferred_element_type=F32)
        acc_ref[h] = acc_ref[h] * alpha + pv
        m_ref[h] = m_new


def _mla_finish(o_ref, acc_ref):
    lane = lax.broadcasted_iota(jnp.int32, acc_ref.shape[1:], 1)
    for hp in range(MLA_HEADS // 2):
        a0 = acc_ref[2 * hp]
        a1 = acc_ref[2 * hp + 1]
        l0 = jnp.sum(jnp.where(lane == MLA_V, a0, 0.0), axis=1, keepdims=True)
        l1 = jnp.sum(jnp.where(lane == 0, a1, 0.0), axis=1, keepdims=True)
        o = jnp.where(lane < MLA_V, a0 / l0, a1 / l1)
        o_ref[:, hp * LANES:(hp + 1) * LANES] = o.astype(o_ref.dtype)


def _mla_lat_kernel(q_ref, kc_ref, vc_ref, kl_ref, vl_ref, o_ref, acc_ref, m_ref):
    j = pl.program_id(2)

    @pl.when(j == 0)
    def _():
        m_ref[...] = jnp.full(m_ref.shape, NEG_BIG, F32)
        acc_ref[...] = jnp.zeros(acc_ref.shape, F32)
        _mla_attend(q_ref, kc_ref, vc_ref, acc_ref, m_ref)

    _mla_attend(q_ref, kl_ref, vl_ref, acc_ref, m_ref)

    @pl.when(j == pl.num_programs(2) - 1)
    def _():
        _mla_finish(o_ref, acc_ref)


def _mla_ctx_kernel(q_ref, kc_ref, vc_ref, o_ref, acc_ref, m_ref):
    m_ref[...] = jnp.full(m_ref.shape, NEG_BIG, F32)
    acc_ref[...] = jnp.zeros(acc_ref.shape, F32)
    _mla_attend(q_ref, kc_ref, vc_ref, acc_ref, m_ref)
    _mla_finish(o_ref, acc_ref)


def _mla_attention(q, k, v, B, S, NC, tq, tk):
    TA, W = q.shape
    TC = B * NC
    OW = MLA_HEADS * MLA_V
    nq, nk = S // tq, S // tk
    scratch = lambda t: [pltpu.VMEM((MLA_HEADS, t, HEAD_PAD), F32), pltpu.VMEM((MLA_HEADS, t, HEAD_PAD), F32)]
    kv_align = math.gcd(TC, tk)
    kv_spec = pl.BlockSpec((pl.Element(tk), pl.Element(W)),
                           lambda b, i, j: (pl.multiple_of(TC + b * S + j * tk, kv_align), 0))
    o_lat = pl.pallas_call(
        _mla_lat_kernel,
        grid=(B, nq, nk),
        in_specs=[pl.BlockSpec((tq, W), lambda b, i, j: (TC // tq + b * nq + i, 0)),
                  pl.BlockSpec((NC, W), lambda b, i, j: (b, 0)),
                  pl.BlockSpec((NC, W), lambda b, i, j: (b, 0)),
                  kv_spec, kv_spec],
        out_specs=pl.BlockSpec((tq, OW), lambda b, i, j: (b * nq + i, 0)),
        out_shape=jax.ShapeDtypeStruct((B * S, OW), BF16),
        scratch_shapes=scratch(tq),
        compiler_params=_cparams(("arbitrary", "arbitrary", "arbitrary")),
        name="mla_attn_lat",
    )(q, k, v, k, v)
    o_ctx = pl.pallas_call(
        _mla_ctx_kernel,
        grid=(B,),
        in_specs=[pl.BlockSpec((NC, W), lambda b: (b, 0))] * 3,
        out_specs=pl.BlockSpec((NC, OW), lambda b: (b, 0)),
        out_shape=jax.ShapeDtypeStruct((TC, OW), BF16),
        scratch_shapes=scratch(NC),
        compiler_params=_cparams(("arbitrary",)),
        name="mla_attn_ctx",
    )(q, k, v)
    return o_ctx, o_lat


def _nat_softmax_pv(scores, values):
    m = scores[0].max(axis=1, keepdims=True)
    for s in scores[1:]:
        m = jnp.maximum(m, s.max(axis=1, keepdims=True))
    l = None
    o = None
    for s, v in zip(scores, values):
        p = jnp.exp2(s - m)
        ls = jnp.sum(p, axis=1, keepdims=True)
        pv = jnp.dot(p.astype(BF16), v, preferred_element_type=F32)
        l = ls if l is None else l + ls
        o = pv if o is None else o + pv
    return o / l


def _nat_lat_kernel(q_ref, k_ref, v_ref, kc_ref, vc_ref, bias_ref, o_ref, *, qscale):
    tq = q_ref.shape[0]
    ck = tq
    nck = k_ref.shape[0] // ck
    lane = lax.broadcasted_iota(jnp.int32, (tq, LANES), 1)
    dn = (((1,), (1,)), ((), ()))
    for hp in range(NAT_HEADS // 2):
        ls = slice(hp * LANES, (hp + 1) * LANES)
        qp = q_ref[:, ls].astype(F32) * qscale
        keys = [k_ref[c * ck:(c + 1) * ck, ls] for c in range(nck)] + [kc_ref[:, ls]]
        vals = [v_ref[c * ck:(c + 1) * ck, ls] for c in range(nck)] + [vc_ref[:, ls]]
        outs = []
        for hh in range(2):
            mask = (lane < NAT_HEAD_DIM) if hh == 0 else (lane >= NAT_HEAD_DIM)
            qm = jnp.where(mask, qp, 0.0).astype(BF16)
            scores = []
            for c, kk in enumerate(keys):
                s = lax.dot_general(qm, kk, dn, preferred_element_type=F32)
                if c < nck:
                    s = s + bias_ref[2 * hp + hh, :, c * ck:(c + 1) * ck].astype(F32)
                scores.append(s)
            outs.append(_nat_softmax_pv(scores, vals))
        o_ref[:, ls] = jnp.where(lane < NAT_HEAD_DIM, outs[0], outs[1]).astype(o_ref.dtype)


def _nat_ctx_kernel(q_ref, kc_ref, vc_ref, o_ref, *, qscale):
    tq = q_ref.shape[0]
    lane = lax.broadcasted_iota(jnp.int32, (tq, LANES), 1)
    dn = (((1,), (1,)), ((), ()))
    for hp in range(NAT_HEADS // 2):
        ls = slice(hp * LANES, (hp + 1) * LANES)
        qp = q_ref[:, ls].astype(F32) * qscale
        outs = []
        for hh in range(2):
            mask = (lane < NAT_HEAD_DIM) if hh == 0 else (lane >= NAT_HEAD_DIM)
            qm = jnp.where(mask, qp, 0.0).astype(BF16)
            s = lax.dot_general(qm, kc_ref[:, ls], dn, preferred_element_type=F32)
            outs.append(_nat_softmax_pv([s], [vc_ref[:, ls]]))
        o_ref[:, ls] = jnp.where(lane < NAT_HEAD_DIM, outs[0], outs[1]).astype(o_ref.dtype)


def _nat_bias_table(rpb, rows_n):
    L = rpb.shape[0]
    qc = np.arange(GRID_W)
    cs = np.clip(qc - NAT_WIN_W // 2, 0, GRID_W - NAT_WIN_W)
    kc = np.arange(GRID_W)
    colvalid = (kc[None, :] >= cs[:, None]) & (kc[None, :] < cs[:, None] + NAT_WIN_W)
    dc = np.clip(kc[None, :] - qc[:, None] + NAT_WIN_W - 1, 0, 2 * NAT_WIN_W - 2)
    bt = jnp.where(colvalid, rpb[:, :, :, dc] * LOG2E, NEG_BIG)
    n_dr = 2 * NAT_WIN_H - 1
    bt = jnp.concatenate([bt, jnp.full((L, NAT_HEADS, 1, GRID_W, GRID_W), NEG_BIG, F32)], axis=2).astype(BF16)
    kh = NAT_WIN_H
    dr = np.full((3, NAT_ROWS, NAT_KROWS), n_dr, np.int32)
    for v, r0 in enumerate((0, 2 * NAT_ROWS, rows_n - NAT_ROWS)):
        kstart = int(np.clip(r0 - kh // 2, 0, rows_n - NAT_KROWS))
        for i in range(NAT_ROWS):
            r = r0 + i
            rs = int(np.clip(r - kh // 2, 0, rows_n - kh))
            for jj in range(NAT_KROWS):
                kr = kstart + jj
                if rs <= kr < rs + kh:
                    dr[v, i, jj] = kr - r + NAT_WIN_H - 1
    t = bt[:, :, dr]
    t = t.transpose(0, 2, 1, 3, 5, 4, 6)
    return t.reshape(L, 3, NAT_HEADS, NAT_ROWS * GRID_W, NAT_KROWS * GRID_W)


def _nat_attention(u, bias, layer, B, S, NC):
    TA = u.shape[0]
    TC = B * NC
    rows_n = S // GRID_W
    nblk = rows_n // NAT_ROWS
    tq = NAT_ROWS * GRID_W
    ck = tq
    qscale = float(NAT_HEAD_DIM ** -0.5) * LOG2E
    cq, ckk, cv = U_NQ // NAT_WIDTH, U_NK // NAT_WIDTH, U_NV // NAT_WIDTH
    base = lambda b: TC // tq + b * nblk

    def kstart_blk(i):
        return jnp.clip(i * NAT_ROWS - NAT_WIN_H // 2, 0, rows_n - NAT_KROWS) // NAT_ROWS

    def kv_spec(col):
        return pl.BlockSpec((pl.Element(NAT_KROWS * GRID_W), pl.Element(NAT_WIDTH)),
                            lambda b, i: (pl.multiple_of((base(b) + kstart_blk(i)) * ck, ck), col * NAT_WIDTH))

    def variant(i):
        return jnp.where(i == 0, 0, jnp.where(i == nblk - 1, 2, 1))

    o_lat = pl.pallas_call(
        functools.partial(_nat_lat_kernel, qscale=qscale),
        grid=(B, nblk),
        in_specs=[pl.BlockSpec((tq, NAT_WIDTH), lambda b, i: (base(b) + i, cq)),
                  kv_spec(ckk), kv_spec(cv),
                  pl.BlockSpec((NC, NAT_WIDTH), lambda b, i: (b, ckk)),
                  pl.BlockSpec((NC, NAT_WIDTH), lambda b, i: (b, cv)),
                  pl.BlockSpec((None, None, NAT_HEADS, tq, NAT_KROWS * GRID_W),
                               lambda b, i: (layer, variant(i), 0, 0, 0))],
        out_specs=pl.BlockSpec((tq, NAT_WIDTH), lambda b, i: (b * nblk + i, 0)),
        out_shape=jax.ShapeDtypeStruct((B * S, NAT_WIDTH), BF16),
        compiler_params=_cparams(("arbitrary", "arbitrary")),
        name="nat_attn_lat",
    )(u, u, u, u, u, bias)
    o_ctx = pl.pallas_call(
        functools.partial(_nat_ctx_kernel, qscale=qscale),
        grid=(B,),
        in_specs=[pl.BlockSpec((NC, NAT_WIDTH), lambda b: (b, cq)),
                  pl.BlockSpec((NC, NAT_WIDTH), lambda b: (b, ckk)),
                  pl.BlockSpec((NC, NAT_WIDTH), lambda b: (b, cv))],
        out_specs=pl.BlockSpec((NC, NAT_WIDTH), lambda b: (b, 0)),
        out_shape=jax.ShapeDtypeStruct((TC, NAT_WIDTH), BF16),
        compiler_params=_cparams(("arbitrary",)),
        name="nat_attn_ctx",
    )(u, u, u)
    return o_ctx, o_lat


def _fnet_chan_kernel(f_ref, w_ref, o_ref):
    for g in range(FNET_GROUPS):
        gs = slice(g * FNET_GROUP_DIM, (g + 1) * FNET_GROUP_DIM)
        o_ref[:, gs] = jnp.dot(f_ref[:, gs], w_ref[...], preferred_element_type=F32).astype(o_ref.dtype)


def _fnet_seq_kernel(a_ref, r_ref, o_ref, acc_ref):
    k = pl.program_id(2)

    @pl.when(k == 0)
    def _():
        acc_ref[...] = jnp.zeros(acc_ref.shape, F32)

    acc_ref[...] += jnp.dot(a_ref[...], r_ref[...], preferred_element_type=F32)

    @pl.when(k == pl.num_programs(2) - 1)
    def _():
        o_ref[...] = acc_ref[...].astype(o_ref.dtype)


def _dft_tables(n, scale):
    j = jnp.arange(n, dtype=jnp.int32)
    idx = (j[:, None] * j[None, :]) & (n - 1)
    ang = idx.astype(F32) * (2.0 * math.pi / n)
    return jnp.cos(ang) * scale, jnp.sin(ang) * scale


def _seq_dft_matrix(n):
    c, s = _dft_tables(n, n ** -0.5)
    return jnp.concatenate([c, -s], axis=1).astype(BF16)


FFT_N2 = 128
FFT_NB = 16


def _fft_tables(n):
    n2 = FFT_N2
    n1 = n // n2
    j1 = jnp.arange(n1, dtype=jnp.int32)[:, None, None]
    k2 = jnp.arange(n2, dtype=jnp.int32)[None, :, None]
    j2 = jnp.arange(n2, dtype=jnp.int32)[None, None, :]
    ang = ((k2 * (j1 + n1 * j2)) & (n - 1)).astype(F32) * (2.0 * math.pi / n)
    s1 = n2 ** -0.5
    m1 = jnp.concatenate([jnp.cos(ang) * s1, -jnp.sin(ang) * s1], axis=1).astype(BF16)
    c1, sn1 = _dft_tables(n1, n1 ** -0.5)
    top = jnp.stack([c1, sn1], axis=2).reshape(n1, 2 * n1)
    bot = jnp.stack([-sn1, c1], axis=2).reshape(n1, 2 * n1)
    w2 = jnp.concatenate([top, bot], axis=0).astype(BF16)
    return m1, w2


def _load_slab(ref, i):
    rows = ref.shape[1] // FFT_NB
    parts = [ref[g, pl.ds(i, rows, stride=FFT_NB), :] for g in range(FNET_GROUPS)]
    return jnp.concatenate(parts, axis=1)


def _fft1_kernel(f_ref, m_ref, o_ref):
    n2 = m_ref.shape[2]
    for i in range(FFT_NB):
        r = jnp.dot(m_ref[i], _load_slab(f_ref, i).astype(BF16), preferred_element_type=F32)
        for g in range(FNET_GROUPS):
            gs = slice(g * FNET_GROUP_DIM, (g + 1) * FNET_GROUP_DIM)
            for p in range(2):
                for kb in range(n2 // FFT_NB):
                    src = p * n2 + kb * FFT_NB
                    dst = (2 * i + p) * FFT_NB
                    o_ref[g, kb, dst:dst + FFT_NB, :] = r[src:src + FFT_NB, gs]


def _fft2_kernel(b_ref, w_ref, cc_ref, cs_ref, o_ref):
    n1 = o_ref.shape[1] // FFT_NB
    for i in range(FFT_NB):
        x = jnp.dot(w_ref[...], _load_slab(b_ref, i).astype(BF16), preferred_element_type=F32).astype(BF16)
        for g in range(FNET_GROUPS):
            gs = slice(g * FNET_GROUP_DIM, (g + 1) * FNET_GROUP_DIM)
            y = (jnp.dot(x[:n1, gs], cc_ref[...], preferred_element_type=F32)
                 + jnp.dot(x[n1:, gs], cs_ref[...], preferred_element_type=F32))
            o_ref[g, pl.ds(i, n1, stride=FFT_NB), :] = y


def _fourier_mix_long(f_lat, m1, w2, w_chan, B, n):
    n2 = FFT_N2
    n1 = n // n2
    G, GD, NB = FNET_GROUPS, FNET_GROUP_DIM, FFT_NB
    bmid = pl.pallas_call(
        _fft1_kernel,
        grid=(B, n1 // NB),
        in_specs=[pl.BlockSpec((None, G, None, n2 * NB, GD), lambda b, j: (b, 0, j, 0, 0)),
                  pl.BlockSpec((NB, 2 * n2, n2), lambda b, j: (j, 0, 0))],
        out_specs=pl.BlockSpec((None, G, n2 // NB, 2 * NB * NB, GD), lambda b, j: (b, 0, 0, j, 0)),
        out_shape=jax.ShapeDtypeStruct((B, G, n2 // NB, 2 * n1 * NB, GD), F32),
        compiler_params=_cparams(("arbitrary", "arbitrary")),
        name="fnet_fft1",
    )(f_lat, m1)
    return pl.pallas_call(
        _fft2_kernel,
        grid=(B, n2 // NB),
        in_specs=[pl.BlockSpec((None, G, None, 2 * n1 * NB, GD), lambda b, j: (b, 0, j, 0, 0)),
                  pl.BlockSpec((2 * n1, 2 * n1), lambda b, j: (0, 0)),
                  pl.BlockSpec((GD, GD), lambda b, j: (0, 0)),
                  pl.BlockSpec((GD, GD), lambda b, j: (0, 1))],
        out_specs=pl.BlockSpec((None, G, None, n1 * NB, GD), lambda b, j: (b, 0, j, 0, 0)),
        out_shape=jax.ShapeDtypeStruct((B, G, n2 // NB, n1 * NB, GD), F32),
        compiler_params=_cparams(("arbitrary", "arbitrary")),
        name="fnet_fft2",
    )(bmid, w2, w_chan, w_chan)


def _fourier_mix(u, row0, B, n, a_mat, w_chan, ts, tm, tk):
    nt = n // ts
    rr = pl.pallas_call(
        _fnet_chan_kernel,
        grid=(B, nt, 2),
        in_specs=[pl.BlockSpec((ts, FNET_WIDTH), lambda b, i, p: (row0 // ts + b * nt + i, U_F // FNET_WIDTH)),
                  pl.BlockSpec((FNET_GROUP_DIM, FNET_GROUP_DIM), lambda b, i, p: (0, p))],
        out_specs=pl.BlockSpec((ts, FNET_WIDTH), lambda b, i, p: (p * nt + i, b)),
        out_shape=jax.ShapeDtypeStruct((2 * n, B * FNET_WIDTH), BF16),
        compiler_params=_cparams(("arbitrary", "arbitrary", "arbitrary")),
        name="fnet_chan",
    )(u, w_chan)
    tn = min(B * FNET_WIDTH, 2048)
    return pl.pallas_call(
        _fnet_seq_kernel,
        grid=(n // tm, (B * FNET_WIDTH) // tn, (2 * n) // tk),
        in_specs=[pl.BlockSpec((tm, tk), lambda i, j, k: (i, k)),
                  pl.BlockSpec((tk, tn), lambda i, j, k: (k, j))],
        out_specs=pl.BlockSpec((tm, tn), lambda i, j, k: (i, j)),
        out_shape=jax.ShapeDtypeStruct((n, B * FNET_WIDTH), BF16),
        scratch_shapes=[pltpu.VMEM((tm, tn), F32)],
        compiler_params=_cparams(("arbitrary", "arbitrary", "arbitrary")),
        name="fnet_seq",
    )(a_mat, rr)


def _merge_kernel(oml_ref, omc_ref, ofl_ref, ofc_ref, onl_ref, onc_ref, gm_ref, gf_ref, gn_ref, x_ref, mod_ref,
                  wm_ref, wf_ref, wn_ref, wo_ref, bo_ref, lg_ref, lb_ref, wr_ref, br_ref, xo_ref, h_ref, lo_ref,
                  *, alpha, nct):
    def gate(g_ref):
        return 1.0 / (1.0 + jnp.exp(-g_ref[...].astype(F32)))

    is_ctx = pl.program_id(0) < nct
    nkb, rows = ofl_ref.shape[1], ofl_ref.shape[2]
    ofl = jnp.concatenate(
        [jnp.concatenate([ofl_ref[g, kb, k1 * FFT_NB:(k1 + 1) * FFT_NB, :]
                          for k1 in range(rows // FFT_NB) for kb in range(nkb)], axis=0)
         for g in range(FNET_GROUPS)], axis=1)
    of = jnp.where(is_ctx, ofc_ref[...].astype(F32), ofl).astype(BF16)
    om = jnp.where(is_ctx, omc_ref[...], oml_ref[...])
    on = jnp.where(is_ctx, onc_ref[...], onl_ref[...])
    m = gate(gm_ref) * jnp.dot(om, wm_ref[...], preferred_element_type=F32)
    m = m + gate(gf_ref) * jnp.dot(of, wf_ref[...], preferred_element_type=F32)
    m = m + gate(gn_ref) * jnp.dot(on, wn_ref[...], preferred_element_type=F32)
    y = jnp.dot(m.astype(BF16), wo_ref[...], preferred_element_type=F32) + bo_ref[...]
    z = alpha * x_ref[...] + mod_ref[2:3, :] * y
    xn = _ln(z) * lg_ref[...] + lb_ref[...]
    xo_ref[...] = xn
    h = (_ln(xn) * (1.0 + mod_ref[4:5, :]) + mod_ref[3:4, :]).astype(BF16)
    h_ref[0] = h
    h_ref[1] = h
    lo_ref[...] = jnp.dot(h, wr_ref[...], preferred_element_type=F32) + br_ref[...]


def _merge(om_lat, om_ctx, of_lat, of_ctx, on_lat, on_ctx, u, xall, mod, mod_row, wm, wf, wn, wo, bo, lg, lb, wr, br,
           alpha, B, S, NC):
    TA, D = xall.shape
    TC = B * NC
    nct = TC // TM
    npb = S // TM
    const = lambda i: (0, 0)
    row = lambda i: (i, 0)
    gcol = U_G // D

    lat = lambda i: jnp.maximum(i - nct, 0)
    lat_row = lambda i: (lat(i), 0)
    ctx_row = lambda i: (jnp.minimum(i, nct - 1), 0)
    outs = pl.pallas_call(
        functools.partial(_merge_kernel, alpha=alpha, nct=nct),
        grid=(TA // TM,),
        in_specs=[pl.BlockSpec((TM, MLA_HEADS * MLA_V), lat_row),
                  pl.BlockSpec((TM, MLA_HEADS * MLA_V), ctx_row),
                  pl.BlockSpec((None, FNET_GROUPS, FFT_N2 // FFT_NB, (TM // FFT_N2) * FFT_NB, FNET_GROUP_DIM),
                               lambda i: (lat(i) // npb, 0, 0, lat(i) % npb, 0)),
                  pl.BlockSpec((TM, FNET_WIDTH), ctx_row),
                  pl.BlockSpec((TM, NAT_WIDTH), lat_row),
                  pl.BlockSpec((TM, NAT_WIDTH), ctx_row),
                  pl.BlockSpec((TM, D), lambda i: (i, gcol)),
                  pl.BlockSpec((TM, D), lambda i: (i, gcol + 1)),
                  pl.BlockSpec((TM, D), lambda i: (i, gcol + 2)),
                  pl.BlockSpec((TM, D), row),
                  pl.BlockSpec((None, 8, D), lambda i: (mod_row(i), 0, 0)),
                  pl.BlockSpec((MLA_HEADS * MLA_V, D), const),
                  pl.BlockSpec((FNET_WIDTH, D), const),
                  pl.BlockSpec((NAT_WIDTH, D), const),
                  pl.BlockSpec((D, D), const),
                  pl.BlockSpec((1, D), const),
                  pl.BlockSpec((1, D), const),
                  pl.BlockSpec((1, D), const),
                  pl.BlockSpec((D, LANES), const),
                  pl.BlockSpec((1, LANES), const)],
        out_specs=[pl.BlockSpec((TM, D), row), pl.BlockSpec((2, TM, D), lambda i: (0, i, 0)),
                   pl.BlockSpec((TM, LANES), row)],
        out_shape=[jax.ShapeDtypeStruct((TA, D), F32), jax.ShapeDtypeStruct((2, TA, D), BF16),
                   jax.ShapeDtypeStruct((TA, LANES), F32)],
        compiler_params=_cparams(("arbitrary",)),
        name="merge",
    )(om_lat, om_ctx, of_lat, of_ctx, on_lat, on_ctx, u, u, u, xall, mod, wm, wf, wn, wo, bo, lg, lb, wr, br)
    return outs


def _router_kernel(lg_ref, e_ref, w_ref, r_ref, cnt_ref, carry_ref):
    lg = lg_ref[...]
    lane = lax.broadcasted_iota(jnp.int32, lg.shape, 1)
    big = jnp.int32(1 << 20)
    is_g = lane < N_GROUPS
    gl = jnp.where(is_g, lg, -jnp.inf)
    ge = jnp.exp(gl - jnp.max(gl, axis=1, keepdims=True))
    p = ge / jnp.sum(ge, axis=1, keepdims=True)
    p_top = jnp.max(p, axis=1, keepdims=True)
    g_idx = jnp.min(jnp.where(is_g & (p == p_top), lane, big), axis=1, keepdims=True)
    lo = N_GROUPS + g_idx * EXPERTS_PER_GROUP
    in_grp = (lane >= lo) & (lane < lo + EXPERTS_PER_GROUP)
    el = jnp.where(in_grp, lg, -jnp.inf)
    m1 = jnp.max(el, axis=1, keepdims=True)
    i1 = jnp.min(jnp.where(in_grp & (el == m1), lane, big), axis=1, keepdims=True)
    rest = in_grp & (lane != i1)
    el2 = jnp.where(rest, lg, -jnp.inf)
    m2 = jnp.max(el2, axis=1, keepdims=True)
    i2 = jnp.min(jnp.where(rest & (el2 == m2), lane, big), axis=1, keepdims=True)
    d = jnp.exp(m2 - m1)
    w1 = p_top * (1.0 / (1.0 + d))
    w2 = p_top * (d / (1.0 + d))
    e_ref[...] = jnp.where(lane == 0, i1 - N_GROUPS, jnp.where(lane == 1, i2 - N_GROUPS, 0))
    w_ref[...] = jnp.where(lane == 0, w1, jnp.where(lane == 1, w2, 0.0))

    @pl.when(pl.program_id(0) == 0)
    def _():
        carry_ref[...] = jnp.zeros(carry_ref.shape, F32)

    tm = lg.shape[0]
    oh1 = lane == i1
    oh2 = lane == i2
    oh = jnp.where(oh1, 1.0, jnp.where(oh2, 1.0, 0.0))
    rr = lax.broadcasted_iota(jnp.int32, (tm, tm), 0)
    cc = lax.broadcasted_iota(jnp.int32, (tm, tm), 1)
    lower = jnp.where(cc < rr, 1.0, 0.0).astype(BF16)
    before = jnp.dot(lower, oh.astype(BF16), preferred_element_type=F32) + carry_ref[0:1, :]
    r1 = jnp.sum(jnp.where(oh1, before, 0.0), axis=1, keepdims=True)
    r2 = jnp.sum(jnp.where(oh2, before, 0.0), axis=1, keepdims=True)
    r_ref[...] = jnp.where(lane == 0, r1, jnp.where(lane == 1, r2, 0.0)).astype(jnp.int32)
    carry_ref[...] = carry_ref[...] + jnp.sum(oh, axis=0, keepdims=True)
    cnt_ref[...] = carry_ref[...].astype(jnp.int32)


def _router(logits):
    TA = logits.shape[0]
    row = lambda i: (i, 0)
    return pl.pallas_call(
        _router_kernel,
        grid=(TA // TM,),
        in_specs=[pl.BlockSpec((TM, LANES), row)],
        out_specs=[pl.BlockSpec((TM, LANES), row)] * 3 + [pl.BlockSpec((8, LANES), lambda i: (0, 0))],
        out_shape=[jax.ShapeDtypeStruct((TA, LANES), jnp.int32), jax.ShapeDtypeStruct((TA, LANES), F32),
                   jax.ShapeDtypeStruct((TA, LANES), jnp.int32), jax.ShapeDtypeStruct((8, LANES), jnp.int32)],
        scratch_shapes=[pltpu.VMEM((8, LANES), F32)],
        compiler_params=_cparams(("arbitrary",)),
        name="router",
    )(logits)


def _moe_kernel(be_ref, nu_ref, x_ref, wg_ref, wu_ref, wd_ref, o_ref, wgb_ref, wub_ref, wdb_ref):
    i = pl.program_id(0)
    changed = jnp.logical_or(i == 0, be_ref[i] != be_ref[jnp.maximum(i - 1, 0)])

    @pl.when(changed)
    def _():
        wgb_ref[...] = wg_ref[...].astype(BF16)
        wub_ref[...] = wu_ref[...].astype(BF16)
        wdb_ref[...] = wd_ref[...].astype(BF16)

    @pl.when(i < nu_ref[0])
    def _():
        x = x_ref[...]
        a = jnp.dot(x, wgb_ref[...], preferred_element_type=F32)
        b = jnp.dot(x, wub_ref[...], preferred_element_type=F32)
        hmid = (a * (1.0 / (1.0 + jnp.exp(-a))) * b).astype(BF16)
        o_ref[...] = jnp.dot(hmid, wdb_ref[...], preferred_element_type=F32).astype(o_ref.dtype)

    @pl.when(i >= nu_ref[0])
    def _():
        o_ref[...] = jnp.zeros(o_ref.shape, o_ref.dtype)


def _moe_experts(xb, block_e, n_used, w_gate, w_up, w_down, layer):
    n_rows, D = xb.shape
    nb = n_rows // MOE_BM
    DE = w_gate.shape[-1]
    xrow = lambda i, be, nu: (jnp.minimum(i, nu[0] - 1), 0)
    wmap = lambda i, be, nu: (layer, be[i], 0, 0)
    grid_spec = pltpu.PrefetchScalarGridSpec(
        num_scalar_prefetch=2,
        grid=(nb,),
        in_specs=[pl.BlockSpec((MOE_BM, D), xrow),
                  pl.BlockSpec((None, None, D, DE), wmap),
                  pl.BlockSpec((None, None, D, DE), wmap),
                  pl.BlockSpec((None, None, DE, D), wmap)],
        out_specs=pl.BlockSpec((MOE_BM, D), lambda i, be, nu: (i, 0)),
        scratch_shapes=[pltpu.VMEM((D, DE), BF16), pltpu.VMEM((D, DE), BF16), pltpu.VMEM((DE, D), BF16)],
    )
    return pl.pallas_call(
        _moe_kernel,
        grid_spec=grid_spec,
        out_shape=jax.ShapeDtypeStruct((n_rows, D), BF16),
        compiler_params=_cparams(("arbitrary",)),
        name="moe_experts",
    )(block_e, n_used, xb, w_gate, w_up, w_down)


def _ffn_combine(x_ref, y0_ref, y1_ref, w_ref, mod_ref, lg_ref, lb_ref, alpha):
    w = w_ref[...]
    f = y0_ref[...].astype(F32) * w[:, 0:1] + y1_ref[...].astype(F32) * w[:, 1:2]
    z = alpha * x_ref[...] + mod_ref[5:6, :] * f
    return _ln(z) * lg_ref[...] + lb_ref[...]


def _ffn_out_kernel(x_ref, y0_ref, y1_ref, w_ref, mod_ref, lg_ref, lb_ref, o_ref, *, alpha):
    o_ref[...] = _ffn_combine(x_ref, y0_ref, y1_ref, w_ref, mod_ref, lg_ref, lb_ref, alpha)


def _ffn_out(xall, y2, w_tok, mod, mod_row, lg, lb, alpha, first_tile=0):
    TA, D = xall.shape
    nt = TA // TM
    row = lambda i: (i + first_tile, 0)
    const = lambda i: (0, 0)
    return pl.pallas_call(
        functools.partial(_ffn_out_kernel, alpha=alpha),
        grid=(nt - first_tile,),
        in_specs=[pl.BlockSpec((TM, D), row), pl.BlockSpec((TM, D), row),
                  pl.BlockSpec((TM, D), lambda i: (nt + i + first_tile, 0)),
                  pl.BlockSpec((TM, LANES), row),
                  pl.BlockSpec((None, 8, D), lambda i: (mod_row(i + first_tile), 0, 0)),
                  pl.BlockSpec((1, D), const), pl.BlockSpec((1, D), const)],
        out_specs=pl.BlockSpec((TM, D), lambda i: (i, 0)),
        out_shape=jax.ShapeDtypeStruct((TA - first_tile * TM, D), F32),
        compiler_params=_cparams(("arbitrary",)),
        name="ffn_out",
    )(xall, y2, y2, w_tok, mod, lg, lb)


def _reorder_w_in(w_in, b_in):
    D = w_in.shape[0]
    o_cq, o_ckv, o_kr = 0, MLA_Q_LORA, MLA_Q_LORA + MLA_KV_LORA
    o_f = o_kr + MLA_ROPE
    o_nq = o_f + FNET_WIDTH
    o_g = o_nq + 3 * NAT_WIDTH
    rot_src, rot_sign = _rope_partner()

    def build(a):
        kr = a[..., o_kr:o_kr + MLA_ROPE]
        kr_rot = kr[..., rot_src] * rot_sign
        pad = jnp.zeros(a.shape[:-1] + (LANES - 2 * MLA_ROPE,), a.dtype)
        return jnp.concatenate([a[..., o_f:o_g], a[..., o_g:], a[..., o_cq:o_kr], kr, kr_rot, pad], axis=-1)

    return build(w_in).astype(BF16), build(b_in[None, :])


def _rope_partner():
    q = MLA_ROPE // 4
    src = np.zeros(MLA_ROPE, np.int32)
    sign = np.zeros(MLA_ROPE, np.float32)
    for half in range(2):
        o = half * 2 * q
        for i in range(q):
            src[o + i], sign[o + i] = o + q + i, -1.0
            src[o + q + i], sign[o + q + i] = o + i, 1.0
    return src, sign


def _pad_heads(w, width, offset=0):
    K = w.shape[0]
    w = w.reshape(K, MLA_HEADS, width)
    out = jnp.zeros((K, MLA_HEADS, HEAD_PAD), w.dtype)
    return out.at[:, :, offset:offset + width].set(w).reshape(K, MLA_HEADS * HEAD_PAD)


def _mla_weights(w_uq, w_uk, w_uv):
    rot_src, rot_sign = _rope_partner()
    dq = MLA_NOPE + MLA_ROPE
    wq = w_uq.reshape(-1, MLA_HEADS, dq)
    wq_pe_rot = wq[:, :, MLA_NOPE:][:, :, rot_src] * rot_sign
    wqa = _pad_heads(wq.reshape(-1, MLA_HEADS * dq), dq)
    wqb = _pad_heads(wq_pe_rot.reshape(-1, MLA_HEADS * MLA_ROPE), MLA_ROPE, MLA_NOPE)
    wk = _pad_heads(w_uk, MLA_NOPE)
    K = w_uv.shape[0]
    wv3 = w_uv.reshape(K, MLA_HEADS // 2, 2, MLA_V)
    wv = jnp.zeros((K, MLA_HEADS // 2, 2, HEAD_PAD), w_uv.dtype)
    wv = wv.at[:, :, 0, :MLA_V].set(wv3[:, :, 0]).at[:, :, 1, MLA_V:].set(wv3[:, :, 1])
    wv = wv.reshape(K, MLA_HEADS * HEAD_PAD)
    vone = np.zeros((MLA_HEADS // 2, 2, HEAD_PAD), np.float32)
    vone[:, 0, MLA_V] = 1.0
    vone[:, 1, 0] = 1.0
    pa = np.zeros((LANES, MLA_HEADS, HEAD_PAD), np.float32)
    pb = np.zeros((LANES, MLA_HEADS, HEAD_PAD), np.float32)
    for i in range(MLA_ROPE):
        pa[i, :, MLA_NOPE + i] = 1.0
        pb[MLA_ROPE + i, :, MLA_NOPE + i] = 1.0
    W = MLA_HEADS * HEAD_PAD
    return (wqa.astype(BF16), wqb.astype(BF16), wk.astype(BF16), wv.astype(BF16),
            jnp.asarray(pa.reshape(LANES, W), BF16), jnp.asarray(pb.reshape(LANES, W), BF16),
            jnp.asarray(vone.reshape(1, W)))


def _rope_lane_tables(S):
    t = jnp.arange(S)
    rows = (t // GRID_W).astype(F32)
    cols = (t % GRID_W).astype(F32)
    n_freq = MLA_ROPE // 4
    inv = jnp.power(ROPE_BASE, -jnp.arange(n_freq, dtype=F32) / n_freq)
    ar = rows[:, None] * inv[None, :]
    ac = cols[:, None] * inv[None, :]
    ones = jnp.ones((S, MLA_NOPE), F32)
    zpad = jnp.zeros((S, HEAD_PAD - MLA_NOPE - MLA_ROPE), F32)
    cos = jnp.concatenate([ones, jnp.cos(ar), jnp.cos(ar), jnp.cos(ac), jnp.cos(ac), zpad], axis=1)
    sin = jnp.concatenate([0 * ones, jnp.sin(ar), jnp.sin(ar), jnp.sin(ac), jnp.sin(ac), zpad], axis=1)
    idc = jnp.concatenate([jnp.ones((TM, MLA_NOPE + MLA_ROPE), F32), jnp.zeros((TM, HEAD_PAD - MLA_NOPE - MLA_ROPE), F32)], axis=1)
    return jnp.concatenate([idc, cos], axis=0), jnp.concatenate([jnp.zeros((TM, HEAD_PAD), F32), sin], axis=0)


def _moe_plan(e_ids, ranks, counts, n_blocks):
    t = e_ids.shape[0]
    a = 2 * t
    padded = (counts + MOE_BM - 1) // MOE_BM * MOE_BM
    pends = jnp.cumsum(padded)
    pstarts = pends - padded
    eo = jnp.arange(N_EXPERTS, dtype=jnp.int32)
    pos = ranks + jnp.sum(jnp.where(e_ids[:, :, None] == eo, pstarts, 0), axis=-1)
    blk0 = jnp.arange(n_blocks, dtype=jnp.int32) * MOE_BM
    block_e = jnp.minimum(jnp.sum((pends[None, :] <= blk0[:, None]).astype(jnp.int32), axis=1), N_EXPERTS - 1)
    n_used = (pends[-1] // MOE_BM).astype(jnp.int32).reshape(1)
    nd = n_blocks * MOE_BM - a
    dcum = jnp.cumsum(padded - counts)
    d_e = jnp.sum((dcum[None, :] <= jnp.arange(nd, dtype=jnp.int32)[:, None]).astype(jnp.int32), axis=1)
    keys = jnp.concatenate([e_ids.reshape(a) * 2, d_e * 2 + 1])
    payload = jnp.concatenate([jnp.arange(a, dtype=jnp.int32), (jnp.arange(nd, dtype=jnp.int32) * 2) % a])
    shift = max(a - 1, 1).bit_length()
    assert (2 * N_EXPERTS + 2) << shift < 2 ** 31
    row_asg = jnp.sort(keys * (1 << shift) + payload) & ((1 << shift) - 1)
    row_src = (row_asg % 2) * t + row_asg // 2
    return row_src, pos, block_e, n_used


def kernel(x, c, ctx, c_ctx, w_ada, b_ada, w_in, b_in, q_norm_g, kv_norm_g, w_uq, w_uk, w_uv, rpb, w_o_mla, w_o_fnet, w_o_nat, w_out, b_out, ln1_g, ln1_b, w_rg, b_rg, w_re, b_re, w_gate_e, w_up_e, w_down_e, ln2_g, ln2_b):
    B, S, D = x.shape
    NC = ctx.shape[1]
    L = w_ada.shape[0]
    TC, TL = B * NC, B * S
    TA = TC + TL
    assert TC % TM == 0 and S % TM == 0 and NC % 128 == 0 and B < 8
    alpha = float((2 * L) ** 0.25)
    nct, npb = TC // TM, S // TM

    def mod_row(i):
        return jnp.where(i < nct, B, jnp.maximum(i - nct, 0) // npb)

    def tab_row(i):
        return jnp.where(i < nct, 0, 1 + jnp.maximum(i - nct, 0) % npb)

    cc = jnp.concatenate([c, c_ctx[None, :], jnp.zeros((8 - B - 1, D), F32)], axis=0)
    mod_all = _ada_all(cc, w_ada, b_ada).reshape(L, 8, 6, D)
    mod_all = jnp.concatenate([mod_all, jnp.zeros((L, 8, 2, D), F32)], axis=2)

    tabc, tabs = _rope_lane_tables(S)
    chan_scale = FNET_GROUP_DIM ** -0.5
    cch, sch = _dft_tables(FNET_GROUP_DIM, chan_scale)
    w_chan = jnp.concatenate([cch, sch], axis=1).astype(BF16)
    assert S % (FFT_N2 * FFT_NB) == 0
    m1, w2 = _fft_tables(S)
    a_ctx = _seq_dft_matrix(NC)
    nat_bias = _nat_bias_table(rpb, S // GRID_W)
    qscale = float((MLA_NOPE + MLA_ROPE) ** -0.5) * LOG2E
    rows_n = S // GRID_W
    n_blocks = (2 * TA) // MOE_BM + N_EXPERTS
    tq = math.gcd(1024, math.gcd(S, TC))
    tk = math.gcd(2048, S)

    xall = jnp.concatenate([ctx.reshape(TC, D), x.reshape(TL, D)], axis=0)
    pending_ffn = None
    for l in range(L):
        mod = mod_all[l]
        w_in_r, b_in_r = _reorder_w_in(w_in[l], b_in[l])
        if pending_ffn is None:
            u, f_lat = _in_proj(xall, mod, w_in_r, b_in_r, mod_row, B, S, nct)
        else:
            xall, u, f_lat = _in_proj(xall, mod, w_in_r, b_in_r, mod_row, B, S, nct, ffn=pending_ffn)

        wqa, wqb, wk, wv, pa, pb, vone = _mla_weights(w_uq[l], w_uk[l], w_uv[l])
        q, k, v = _qkv(u, tabc, tabs, tab_row, q_norm_g[l][None, :], kv_norm_g[l][None, :],
                       wqa, wqb, wk, wv, pa, pb, vone, qscale)
        om_ctx, om_lat = _mla_attention(q, k, v, B, S, NC, tq, tk)

        on_ctx, on_lat = _nat_attention(u, nat_bias, l, B, S, NC)

        of_lat = _fourier_mix_long(f_lat, m1, w2, w_chan, B, S)
        of_ctx = _fourier_mix(u, 0, B, NC, a_ctx, w_chan, ts=NC, tm=NC, tk=2 * NC)
        of_ctx = of_ctx.reshape(NC, B, FNET_WIDTH).transpose(1, 0, 2).reshape(TC, FNET_WIDTH)

        w_r = jnp.concatenate([w_rg[l], w_re[l], jnp.zeros((D, LANES - N_GROUPS - N_EXPERTS), F32)], axis=1).astype(BF16)
        b_r = jnp.concatenate([b_rg[l], b_re[l], jnp.zeros((LANES - N_GROUPS - N_EXPERTS,), F32)])[None, :]
        xall, h_f, logits = _merge(om_lat, om_ctx, of_lat, of_ctx, on_lat, on_ctx, u, xall, mod, mod_row,
                                   w_o_mla[l].astype(BF16), w_o_fnet[l].astype(BF16), w_o_nat[l].astype(BF16),
                                   w_out[l].astype(BF16), b_out[l][None, :], ln1_g[l][None, :], ln1_b[l][None, :],
                                   w_r, b_r, alpha, B, S, NC)

        e_out, w_tok, r_out, cnt = _router(logits)
        counts = cnt[0, N_GROUPS:N_GROUPS + N_EXPERTS]
        row_src, pos, block_e, n_used = _moe_plan(e_out[:, :2], r_out[:, :2], counts, n_blocks)
        xb = jnp.take(h_f.reshape(2 * TA, D), row_src, axis=0, mode="clip")
        yb = _moe_experts(xb, block_e, n_used, w_gate_e, w_up_e, w_down_e, l)
        y2 = jnp.take(yb, jnp.concatenate([pos[:, 0], pos[:, 1]]), axis=0, mode="clip")
        pending_ffn = (y2, w_tok, mod, ln2_g[l][None, :], ln2_b[l][None, :], alpha)
    y2, w_tok, mod, lg, lb, _ = pending_ffn
    out = _ffn_out(xall, y2, w_tok, mod, mod_row, lg, lb, alpha, first_tile=nct)
    return out.reshape(B, S, D)
```

```python
import functools
import math

import numpy as np
import jax
import jax.numpy as jnp
from jax import lax
from jax.experimental import pallas as pl
from jax.experimental.pallas import tpu as pltpu

F32 = jnp.float32
BF16 = jnp.bfloat16

GRID_W = 64
MLA_HEADS = 8
MLA_Q_LORA = 256
MLA_KV_LORA = 128
MLA_NOPE = 64
MLA_ROPE = 32
MLA_V = 64
ROPE_BASE = 10000.0
FNET_GROUPS = 4
FNET_GROUP_DIM = 128
FNET_WIDTH = FNET_GROUPS * FNET_GROUP_DIM
NAT_HEADS = 8
NAT_HEAD_DIM = 64
NAT_WIDTH = NAT_HEADS * NAT_HEAD_DIM
NAT_WIN_H = 8
NAT_WIN_W = 16
N_GROUPS = 4
EXPERTS_PER_GROUP = 8
N_EXPERTS = N_GROUPS * EXPERTS_PER_GROUP
LN_EPS = 1e-6

LANES = 128
HEAD_PAD = 128
LOG2E = 1.4426950408889634
NEG_BIG = -1e30
VMEM_LIMIT = 56 * 1024 * 1024

U_F, U_NQ, U_NK, U_NV, U_G, U_CQ, U_CKV, U_KR = 0, 512, 1024, 1536, 2048, 5120, 5376, 5504

TM = 512
NAT_ROWS = 4
NAT_KROWS = 12
MOE_BM = 512


def _cparams(sem):
    return pltpu.CompilerParams(dimension_semantics=sem, vmem_limit_bytes=VMEM_LIMIT)


def _ln(x):
    mu = jnp.mean(x, axis=-1, keepdims=True)
    xc = x - mu
    var = jnp.mean(xc * xc, axis=-1, keepdims=True)
    return xc * lax.rsqrt(var + LN_EPS)


def _ada_kernel(c_ref, w_ref, b_ref, o_ref):
    c = c_ref[...]
    a = c * (1.0 / (1.0 + jnp.exp(-c)))
    o_ref[...] = jnp.dot(a, w_ref[...], preferred_element_type=F32,
                         precision=lax.Precision.HIGHEST) + b_ref[...]


def _ada_all(cc, w_ada, b_ada):
    L, D, N = w_ada.shape
    tn = 1536
    return pl.pallas_call(
        _ada_kernel,
        grid=(L, N // tn),
        in_specs=[pl.BlockSpec((8, D), lambda l, j: (0, 0)),
                  pl.BlockSpec((None, D, tn), lambda l, j: (l, 0, j)),
                  pl.BlockSpec((None, 1, tn), lambda l, j: (l, 0, j))],
        out_specs=pl.BlockSpec((None, 8, tn), lambda l, j: (l, 0, j)),
        out_shape=jax.ShapeDtypeStruct((L, 8, N), F32),
        compiler_params=_cparams(("arbitrary", "arbitrary")),
        name="ada_mod",
    )(cc, w_ada, b_ada.reshape(L, 1, N))


def _in_proj_kernel(x_ref, mod_ref, w_ref, b_ref, o_ref, f_ref, *, cw):
    _in_proj_body(x_ref[...], mod_ref, w_ref, b_ref, o_ref, f_ref, cw)


def _ffn_in_proj_kernel(x_ref, y0_ref, y1_ref, wt_ref, modp_ref, lg_ref, lb_ref, mod_ref, w_ref, b_ref,
                        xo_ref, o_ref, f_ref, *, alpha, cw):
    xn = _ffn_combine(x_ref, y0_ref, y1_ref, wt_ref, modp_ref, lg_ref, lb_ref, alpha)
    xo_ref[...] = xn
    _in_proj_body(xn, mod_ref, w_ref, b_ref, o_ref, f_ref, cw)


def _in_proj_body(x, mod_ref, w_ref, b_ref, o_ref, f_ref, cw):
    y = _ln(x)
    h = (y * (1.0 + mod_ref[1:2, :]) + mod_ref[0:1, :]).astype(BF16)
    for j in range(o_ref.shape[1] // cw):
        sl = slice(j * cw, (j + 1) * cw)
        acc = jnp.dot(h, w_ref[:, sl], preferred_element_type=F32) + b_ref[:, sl]
        o_ref[:, sl] = acc.astype(o_ref.dtype)
        if j == U_F // cw:
            nbk = f_ref.shape[1]
            n1 = nbk * FFT_NB
            for g in range(FNET_GROUPS):
                for jb in range(nbk):
                    for jj in range(acc.shape[0] // n1):
                        src = jj * n1 + jb * FFT_NB
                        f_ref[g, jb, jj * FFT_NB:(jj + 1) * FFT_NB, :] = (
                            acc[src:src + FFT_NB, g * FNET_GROUP_DIM:(g + 1) * FNET_GROUP_DIM])


def _in_proj(xall, mod, w_in_r, b_in_r, mod_row, B, S, nct, ffn=None):
    TA, D = xall.shape
    N = w_in_r.shape[1]
    n1 = S // FFT_N2
    npb = S // TM
    nt = TA // TM
    lat = lambda i: jnp.maximum(i - nct, 0)
    row = lambda i: (i, 0)
    const = lambda i: (0, 0)
    mod_spec = pl.BlockSpec((None, 8, D), lambda i: (mod_row(i), 0, 0))
    f_spec = pl.BlockSpec((None, FNET_GROUPS, n1 // FFT_NB, (TM // n1) * FFT_NB, FNET_GROUP_DIM),
                          lambda i: (lat(i) // npb, 0, 0, lat(i) % npb, 0))
    proj_specs = [mod_spec, pl.BlockSpec((D, N), const), pl.BlockSpec((1, N), const)]
    outs = [jax.ShapeDtypeStruct((TA, N), BF16),
            jax.ShapeDtypeStruct((B, FNET_GROUPS, n1 // FFT_NB, FFT_N2 * FFT_NB, FNET_GROUP_DIM), F32)]
    out_specs = [pl.BlockSpec((TM, N), row), f_spec]
    if ffn is None:
        return pl.pallas_call(
            functools.partial(_in_proj_kernel, cw=512),
            grid=(nt,),
            in_specs=[pl.BlockSpec((TM, D), row)] + proj_specs,
            out_specs=out_specs,
            out_shape=outs,
            compiler_params=_cparams(("arbitrary",)),
            name="in_proj",
        )(xall, mod, w_in_r, b_in_r)
    y2, w_tok, mod_prev, lg, lb, alpha = ffn
    xn, u, f_lat = pl.pallas_call(
        functools.partial(_ffn_in_proj_kernel, alpha=alpha, cw=512),
        grid=(nt,),
        in_specs=[pl.BlockSpec((TM, D), row), pl.BlockSpec((TM, D), row), pl.BlockSpec((TM, D), lambda i: (nt + i, 0)),
                  pl.BlockSpec((TM, LANES), row), mod_spec, pl.BlockSpec((1, D), const), pl.BlockSpec((1, D), const)]
                 + proj_specs,
        out_specs=[pl.BlockSpec((TM, D), row)] + out_specs,
        out_shape=[jax.ShapeDtypeStruct((TA, D), F32)] + outs,
        compiler_params=_cparams(("arbitrary",)),
        name="ffn_in_proj",
    )(xall, y2, y2, w_tok, mod_prev, lg, lb, mod, w_in_r, b_in_r)
    return xn, u, f_lat


def _rms(x, g):
    return x * lax.rsqrt(jnp.mean(x * x, axis=-1, keepdims=True) + LN_EPS) * g


def _qkv_kernel(cq_ref, ckv_ref, kr_ref, tc_ref, ts_ref, qg_ref, kvg_ref, wqa_ref, wqb_ref,
                wk_ref, wv_ref, pa_ref, pb_ref, vone_ref, q_ref, k_ref, v_ref, *, qscale):
    cqn = _rms(cq_ref[...].astype(F32), qg_ref[...]).astype(BF16)
    ckvn = _rms(ckv_ref[...].astype(F32), kvg_ref[...]).astype(BF16)
    cos = jnp.concatenate([tc_ref[...]] * MLA_HEADS, axis=1)
    sin = jnp.concatenate([ts_ref[...]] * MLA_HEADS, axis=1)
    qa = jnp.dot(cqn, wqa_ref[...], preferred_element_type=F32)
    qb = jnp.dot(cqn, wqb_ref[...], preferred_element_type=F32)
    q_ref[...] = ((qa * cos + qb * sin) * qscale).astype(q_ref.dtype)
    kr = kr_ref[...]
    ka = jnp.dot(kr, pa_ref[...], preferred_element_type=F32)
    kb = jnp.dot(kr, pb_ref[...], preferred_element_type=F32)
    kn = jnp.dot(ckvn, wk_ref[...], preferred_element_type=F32)
    k_ref[...] = (kn + ka * cos + kb * sin).astype(k_ref.dtype)
    v = jnp.dot(ckvn, wv_ref[...], preferred_element_type=F32) + vone_ref[...]
    v_ref[...] = v.astype(v_ref.dtype)


def _qkv(u, tabc, tabs, tab_row, qg, kvg, wqa, wqb, wk, wv, pa, pb, vone, qscale):
    TA = u.shape[0]
    W = MLA_HEADS * HEAD_PAD
    const = lambda i: (0, 0)
    out = jax.ShapeDtypeStruct((TA, W), BF16)
    return pl.pallas_call(
        functools.partial(_qkv_kernel, qscale=qscale),
        grid=(TA // TM,),
        in_specs=[pl.BlockSpec((TM, MLA_Q_LORA), lambda i: (i, U_CQ // MLA_Q_LORA)),
                  pl.BlockSpec((TM, MLA_KV_LORA), lambda i: (i, U_CKV // MLA_KV_LORA)),
                  pl.BlockSpec((TM, LANES), lambda i: (i, U_KR // LANES)),
                  pl.BlockSpec((TM, LANES), lambda i: (tab_row(i), 0)),
                  pl.BlockSpec((TM, LANES), lambda i: (tab_row(i), 0)),
                  pl.BlockSpec((1, MLA_Q_LORA), const),
                  pl.BlockSpec((1, MLA_KV_LORA), const),
                  pl.BlockSpec((MLA_Q_LORA, W), const),
                  pl.BlockSpec((MLA_Q_LORA, W), const),
                  pl.BlockSpec((MLA_KV_LORA, W), const),
                  pl.BlockSpec((MLA_KV_LORA, W), const),
                  pl.BlockSpec((LANES, W), const),
                  pl.BlockSpec((LANES, W), const),
                  pl.BlockSpec((1, W), const)],
        out_specs=[pl.BlockSpec((TM, W), lambda i: (i, 0))] * 3,
        out_shape=[out, out, out],
        compiler_params=_cparams(("arbitrary",)),
        name="mla_qkv",
    )(u, u, u, tabc, tabs, qg, kvg, wqa, wqb, wk, wv, pa, pb, vone)


def _mla_attend(q_ref, k_ref, v_ref, acc_ref, m_ref):
    tk = k_ref.shape[0]
    for h in range(MLA_HEADS):
        hs = slice(h * HEAD_PAD, (h + 1) * HEAD_PAD)
        s = lax.dot_general(q_ref[:, hs], k_ref[:, hs], (((1,), (1,)), ((), ())),
                            preferred_element_type=F32)
        m_prev = m_ref[h]
        m_new = jnp.maximum(m_prev, jnp.max(s, axis=1, keepdims=True))
        alpha = jnp.exp2(m_prev - m_new)
        p = jnp.exp2((s - jnp.concatenate([m_new] * (tk // LANES), axis=1)).astype(BF16))
        pv = jnp.dot(p, v_ref[:, hs], preferred_element_type=F32)
        acc_ref[h] = acc_ref[h] * alpha + pv
        m_ref[h] = m_new


def _mla_finish(o_ref, acc_ref):
    lane = lax.broadcasted_iota(jnp.int32, acc_ref.shape[1:], 1)
    for hp in range(MLA_HEADS // 2):
        a0 = acc_ref[2 * hp]
        a1 = acc_ref[2 * hp + 1]
        l0 = jnp.sum(jnp.where(lane == MLA_V, a0, 0.0), axis=1, keepdims=True)
        l1 = jnp.sum(jnp.where(lane == 0, a1, 0.0), axis=1, keepdims=True)
        o = jnp.where(lane < MLA_V, a0 / l0, a1 / l1)
        o_ref[:, hp * LANES:(hp + 1) * LANES] = o.astype(o_ref.dtype)


def _mla_lat_kernel(q_ref, kc_ref, vc_ref, kl_ref, vl_ref, o_ref, acc_ref, m_ref):
    j = pl.program_id(2)

    @pl.when(j == 0)
    def _():
        m_ref[...] = jnp.full(m_ref.shape, NEG_BIG, F32)
        acc_ref[...] = jnp.zeros(acc_ref.shape, F32)
        _mla_attend(q_ref, kc_ref, vc_ref, acc_ref, m_ref)

    _mla_attend(q_ref, kl_ref, vl_ref, acc_ref, m_ref)

    @pl.when(j == pl.num_programs(2) - 1)
    def _():
        _mla_finish(o_ref, acc_ref)


def _mla_ctx_kernel(q_ref, kc_ref, vc_ref, o_ref, acc_ref, m_ref):
    m_ref[...] = jnp.full(m_ref.shape, NEG_BIG, F32)
    acc_ref[...] = jnp.zeros(acc_ref.shape, F32)
    _mla_attend(q_ref, kc_ref, vc_ref, acc_ref, m_ref)
    _mla_finish(o_ref, acc_ref)


def _mla_attention(q, k, v, B, S, NC, tq, tk):
    TA, W = q.shape
    TC = B * NC
    OW = MLA_HEADS * MLA_V
    nq, nk = S // tq, S // tk
    scratch = lambda t: [pltpu.VMEM((MLA_HEADS, t, HEAD_PAD), F32), pltpu.VMEM((MLA_HEADS, t, HEAD_PAD), F32)]
    kv_align = math.gcd(TC, tk)
    kv_spec = pl.BlockSpec((pl.Element(tk), pl.Element(W)),
                           lambda b, i, j: (pl.multiple_of(TC + b * S + j * tk, kv_align), 0))
    o_lat = pl.pallas_call(
        _mla_lat_kernel,
        grid=(B, nq, nk),
        in_specs=[pl.BlockSpec((tq, W), lambda b, i, j: (TC // tq + b * nq + i, 0)),
                  pl.BlockSpec((NC, W), lambda b, i, j: (b, 0)),
                  pl.BlockSpec((NC, W), lambda b, i, j: (b, 0)),
                  kv_spec, kv_spec],
        out_specs=pl.BlockSpec((tq, OW), lambda b, i, j: (b * nq + i, 0)),
        out_shape=jax.ShapeDtypeStruct((B * S, OW), BF16),
        scratch_shapes=scratch(tq),
        compiler_params=_cparams(("arbitrary", "arbitrary", "arbitrary")),
        name="mla_attn_lat",
    )(q, k, v, k, v)
    o_ctx = pl.pallas_call(
        _mla_ctx_kernel,
        grid=(B,),
        in_specs=[pl.BlockSpec((NC, W), lambda b: (b, 0))] * 3,
        out_specs=pl.BlockSpec((NC, OW), lambda b: (b, 0)),
        out_shape=jax.ShapeDtypeStruct((TC, OW), BF16),
        scratch_shapes=scratch(NC),
        compiler_params=_cparams(("arbitrary",)),
        name="mla_attn_ctx",
    )(q, k, v)
    return o_ctx, o_lat


def _nat_softmax_pv(scores, values):
    m = scores[0].max(axis=1, keepdims=True)
    for s in scores[1:]:
        m = jnp.maximum(m, s.max(axis=1, keepdims=True))
    l = None
    o = None
    for s, v in zip(scores, values):
        p = jnp.exp2(s - m)
        ls = jnp.sum(p, axis=1, keepdims=True)
        pv = jnp.dot(p.astype(BF16), v, preferred_element_type=F32)
        l = ls if l is None else l + ls
        o = pv if o is None else o + pv
    return o / l


def _nat_lat_kernel(q_ref, k_ref, v_ref, kc_ref, vc_ref, bias_ref, o_ref, *, qscale):
    tq = q_ref.shape[0]
    ck = tq
    nck = k_ref.shape[0] // ck
    lane = lax.broadcasted_iota(jnp.int32, (tq, LANES), 1)
    dn = (((1,), (1,)), ((), ()))
    for hp in range(NAT_HEADS // 2):
        ls = slice(hp * LANES, (hp + 1) * LANES)
        qp = q_ref[:, ls].astype(F32) * qscale
        keys = [k_ref[c * ck:(c + 1) * ck, ls] for c in range(nck)] + [kc_ref[:, ls]]
        vals = [v_ref[c * ck:(c + 1) * ck, ls] for c in range(nck)] + [vc_ref[:, ls]]
        outs = []
        for hh in range(2):
            mask = (lane < NAT_HEAD_DIM) if hh == 0 else (lane >= NAT_HEAD_DIM)
            qm = jnp.where(mask, qp, 0.0).astype(BF16)
            scores = []
            for c, kk in enumerate(keys):
                s = lax.dot_general(qm, kk, dn, preferred_element_type=F32)
                if c < nck:
                    s = s + bias_ref[2 * hp + hh, :, c * ck:(c + 1) * ck].astype(F32)
                scores.append(s)
            outs.append(_nat_softmax_pv(scores, vals))
        o_ref[:, ls] = jnp.where(lane < NAT_HEAD_DIM, outs[0], outs[1]).astype(o_ref.dtype)


def _nat_ctx_kernel(q_ref, kc_ref, vc_ref, o_ref, *, qscale):
    tq = q_ref.shape[0]
    lane = lax.broadcasted_iota(jnp.int32, (tq, LANES), 1)
    dn = (((1,), (1,)), ((), ()))
    for hp in range(NAT_HEADS // 2):
        ls = slice(hp * LANES, (hp + 1) * LANES)
        qp = q_ref[:, ls].astype(F32) * qscale
        outs = []
        for hh in range(2):
            mask = (lane < NAT_HEAD_DIM) if hh == 0 else (lane >= NAT_HEAD_DIM)
            qm = jnp.where(mask, qp, 0.0).astype(BF16)
            s = lax.dot_general(qm, kc_ref[:, ls], dn, preferred_element_type=F32)
            outs.append(_nat_softmax_pv([s], [vc_ref[:, ls]]))
        o_ref[:, ls] = jnp.where(lane < NAT_HEAD_DIM, outs[0], outs[1]).astype(o_ref.dtype)


def _nat_bias_table(rpb, rows_n):
    L = rpb.shape[0]
    qc = np.arange(GRID_W)
    cs = np.clip(qc - NAT_WIN_W // 2, 0, GRID_W - NAT_WIN_W)
    kc = np.arange(GRID_W)
    colvalid = (kc[None, :] >= cs[:, None]) & (kc[None, :] < cs[:, None] + NAT_WIN_W)
    dc = np.clip(kc[None, :] - qc[:, None] + NAT_WIN_W - 1, 0, 2 * NAT_WIN_W - 2)
    bt = jnp.where(colvalid, rpb[:, :, :, dc] * LOG2E, NEG_BIG)
    n_dr = 2 * NAT_WIN_H - 1
    bt = jnp.concatenate([bt, jnp.full((L, NAT_HEADS, 1, GRID_W, GRID_W), NEG_BIG, F32)], axis=2).astype(BF16)
    kh = NAT_WIN_H
    dr = np.full((3, NAT_ROWS, NAT_KROWS), n_dr, np.int32)
    for v, r0 in enumerate((0, 2 * NAT_ROWS, rows_n - NAT_ROWS)):
        kstart = int(np.clip(r0 - kh // 2, 0, rows_n - NAT_KROWS))
        for i in range(NAT_ROWS):
            r = r0 + i
            rs = int(np.clip(r - kh // 2, 0, rows_n - kh))
            for jj in range(NAT_KROWS):
                kr = kstart + jj
                if rs <= kr < rs + kh:
                    dr[v, i, jj] = kr - r + NAT_WIN_H - 1
    t = bt[:, :, dr]
    t = t.transpose(0, 2, 1, 3, 5, 4, 6)
    return t.reshape(L, 3, NAT_HEADS, NAT_ROWS * GRID_W, NAT_KROWS * GRID_W)


def _nat_attention(u, bias, layer, B, S, NC):
    TA = u.shape[0]
    TC = B * NC
    rows_n = S // GRID_W
    nblk = rows_n // NAT_ROWS
    tq = NAT_ROWS * GRID_W
    ck = tq
    qscale = float(NAT_HEAD_DIM ** -0.5) * LOG2E
    cq, ckk, cv = U_NQ // NAT_WIDTH, U_NK // NAT_WIDTH, U_NV // NAT_WIDTH
    base = lambda b: TC // tq + b * nblk

    def kstart_blk(i):
        return jnp.clip(i * NAT_ROWS - NAT_WIN_H // 2, 0, rows_n - NAT_KROWS) // NAT_ROWS

    def kv_spec(col):
        return pl.BlockSpec((pl.Element(NAT_KROWS * GRID_W), pl.Element(NAT_WIDTH)),
                            lambda b, i: (pl.multiple_of((base(b) + kstart_blk(i)) * ck, ck), col * NAT_WIDTH))

    def variant(i):
        return jnp.where(i == 0, 0, jnp.where(i == nblk - 1, 2, 1))

    o_lat = pl.pallas_call(
        functools.partial(_nat_lat_kernel, qscale=qscale),
        grid=(B, nblk),
        in_specs=[pl.BlockSpec((tq, NAT_WIDTH), lambda b, i: (base(b) + i, cq)),
                  kv_spec(ckk), kv_spec(cv),
                  pl.BlockSpec((NC, NAT_WIDTH), lambda b, i: (b, ckk)),
                  pl.BlockSpec((NC, NAT_WIDTH), lambda b, i: (b, cv)),
                  pl.BlockSpec((None, None, NAT_HEADS, tq, NAT_KROWS * GRID_W),
                               lambda b, i: (layer, variant(i), 0, 0, 0))],
        out_specs=pl.BlockSpec((tq, NAT_WIDTH), lambda b, i: (b * nblk + i, 0)),
        out_shape=jax.ShapeDtypeStruct((B * S, NAT_WIDTH), BF16),
        compiler_params=_cparams(("arbitrary", "arbitrary")),
        name="nat_attn_lat",
    )(u, u, u, u, u, bias)
    o_ctx = pl.pallas_call(
        functools.partial(_nat_ctx_kernel, qscale=qscale),
        grid=(B,),
        in_specs=[pl.BlockSpec((NC, NAT_WIDTH), lambda b: (b, cq)),
                  pl.BlockSpec((NC, NAT_WIDTH), lambda b: (b, ckk)),
                  pl.BlockSpec((NC, NAT_WIDTH), lambda b: (b, cv))],
        out_specs=pl.BlockSpec((NC, NAT_WIDTH), lambda b: (b, 0)),
        out_shape=jax.ShapeDtypeStruct((TC, NAT_WIDTH), BF16),
        compiler_params=_cparams(("arbitrary",)),
        name="nat_attn_ctx",
    )(u, u, u)
    return o_ctx, o_lat


def _fnet_chan_kernel(f_ref, w_ref, o_ref):
    for g in range(FNET_GROUPS):
        gs = slice(g * FNET_GROUP_DIM, (g + 1) * FNET_GROUP_DIM)
        o_ref[:, gs] = jnp.dot(f_ref[:, gs], w_ref[...], preferred_element_type=F32).astype(o_ref.dtype)


def _fnet_seq_kernel(a_ref, r_ref, o_ref, acc_ref):
    k = pl.program_id(2)

    @pl.when(k == 0)
    def _():
        acc_ref[...] = jnp.zeros(acc_ref.shape, F32)

    acc_ref[...] += jnp.dot(a_ref[...], r_ref[...], preferred_element_type=F32)

    @pl.when(k == pl.num_programs(2) - 1)
    def _():
        o_ref[...] = acc_ref[...].astype(o_ref.dtype)


def _dft_tables(n, scale):
    j = jnp.arange(n, dtype=jnp.int32)
    idx = (j[:, None] * j[None, :]) & (n - 1)
    ang = idx.astype(F32) * (2.0 * math.pi / n)
    return jnp.cos(ang) * scale, jnp.sin(ang) * scale


def _seq_dft_matrix(n):
    c, s = _dft_tables(n, n ** -0.5)
    return jnp.concatenate([c, -s], axis=1).astype(BF16)


FFT_N2 = 128
FFT_NB = 16


def _fft_tables(n):
    n2 = FFT_N2
    n1 = n // n2
    j1 = jnp.arange(n1, dtype=jnp.int32)[:, None, None]
    k2 = jnp.arange(n2, dtype=jnp.int32)[None, :, None]
    j2 = jnp.arange(n2, dtype=jnp.int32)[None, None, :]
    ang = ((k2 * (j1 + n1 * j2)) & (n - 1)).astype(F32) * (2.0 * math.pi / n)
    s1 = n2 ** -0.5
    m1 = jnp.concatenate([jnp.cos(ang) * s1, -jnp.sin(ang) * s1], axis=1).astype(BF16)
    c1, sn1 = _dft_tables(n1, n1 ** -0.5)
    top = jnp.stack([c1, sn1], axis=2).reshape(n1, 2 * n1)
    bot = jnp.stack([-sn1, c1], axis=2).reshape(n1, 2 * n1)
    w2 = jnp.concatenate([top, bot], axis=0).astype(BF16)
    return m1, w2


def _load_slab(ref, i):
    rows = ref.shape[1] // FFT_NB
    parts = [ref[g, pl.ds(i, rows, stride=FFT_NB), :] for g in range(FNET_GROUPS)]
    return jnp.concatenate(parts, axis=1)


def _fft1_kernel(f_ref, m_ref, o_ref):
    n2 = m_ref.shape[2]
    for i in range(FFT_NB):
        r = jnp.dot(m_ref[i], _load_slab(f_ref, i).astype(BF16), preferred_element_type=F32)
        for g in range(FNET_GROUPS):
            gs = slice(g * FNET_GROUP_DIM, (g + 1) * FNET_GROUP_DIM)
            for p in range(2):
                for kb in range(n2 // FFT_NB):
                    src = p * n2 + kb * FFT_NB
                    dst = (2 * i + p) * FFT_NB
                    o_ref[g, kb, dst:dst + FFT_NB, :] = r[src:src + FFT_NB, gs]


def _fft2_kernel(b_ref, w_ref, cc_ref, cs_ref, o_ref):
    n1 = o_ref.shape[1] // FFT_NB
    for i in range(FFT_NB):
        x = jnp.dot(w_ref[...], _load_slab(b_ref, i).astype(BF16), preferred_element_type=F32).astype(BF16)
        for g in range(FNET_GROUPS):
            gs = slice(g * FNET_GROUP_DIM, (g + 1) * FNET_GROUP_DIM)
            y = (jnp.dot(x[:n1, gs], cc_ref[...], preferred_element_type=F32)
                 + jnp.dot(x[n1:, gs], cs_ref[...], preferred_element_type=F32))
            o_ref[g, pl.ds(i, n1, stride=FFT_NB), :] = y


def _fourier_mix_long(f_lat, m1, w2, w_chan, B, n):
    n2 = FFT_N2
    n1 = n // n2
    G, GD, NB = FNET_GROUPS, FNET_GROUP_DIM, FFT_NB
    bmid = pl.pallas_call(
        _fft1_kernel,
        grid=(B, n1 // NB),
        in_specs=[pl.BlockSpec((None, G, None, n2 * NB, GD), lambda b, j: (b, 0, j, 0, 0)),
                  pl.BlockSpec((NB, 2 * n2, n2), lambda b, j: (j, 0, 0))],
        out_specs=pl.BlockSpec((None, G, n2 // NB, 2 * NB * NB, GD), lambda b, j: (b, 0, 0, j, 0)),
        out_shape=jax.ShapeDtypeStruct((B, G, n2 // NB, 2 * n1 * NB, GD), F32),
        compiler_params=_cparams(("arbitrary", "arbitrary")),
        name="fnet_fft1",
    )(f_lat, m1)
    return pl.pallas_call(
        _fft2_kernel,
        grid=(B, n2 // NB),
        in_specs=[pl.BlockSpec((None, G, None, 2 * n1 * NB, GD), lambda b, j: (b, 0, j, 0, 0)),
                  pl.BlockSpec((2 * n1, 2 * n1), lambda b, j: (0, 0)),
                  pl.BlockSpec((GD, GD), lambda b, j: (0, 0)),
                  pl.BlockSpec((GD, GD), lambda b, j: (0, 1))],
        out_specs=pl.BlockSpec((None, G, None, n1 * NB, GD), lambda b, j: (b, 0, j, 0, 0)),
        out_shape=jax.ShapeDtypeStruct((B, G, n2 // NB, n1 * NB, GD), F32),
        compiler_params=_cparams(("arbitrary", "arbitrary")),
        name="fnet_fft2",
    )(bmid, w2, w_chan, w_chan)


def _fourier_mix(u, row0, B, n, a_mat, w_chan, ts, tm, tk):
    nt = n // ts
    rr = pl.pallas_call(
        _fnet_chan_kernel,
        grid=(B, nt, 2),
        in_specs=[pl.BlockSpec((ts, FNET_WIDTH), lambda b, i, p: (row0 // ts + b * nt + i, U_F // FNET_WIDTH)),
                  pl.BlockSpec((FNET_GROUP_DIM, FNET_GROUP_DIM), lambda b, i, p: (0, p))],
        out_specs=pl.BlockSpec((ts, FNET_WIDTH), lambda b, i, p: (p * nt + i, b)),
        out_shape=jax.ShapeDtypeStruct((2 * n, B * FNET_WIDTH), BF16),
        compiler_params=_cparams(("arbitrary", "arbitrary", "arbitrary")),
        name="fnet_chan",
    )(u, w_chan)
    tn = min(B * FNET_WIDTH, 2048)
    return pl.pallas_call(
        _fnet_seq_kernel,
        grid=(n // tm, (B * FNET_WIDTH) // tn, (2 * n) // tk),
        in_specs=[pl.BlockSpec((tm, tk), lambda i, j, k: (i, k)),
                  pl.BlockSpec((tk, tn), lambda i, j, k: (k, j))],
        out_specs=pl.BlockSpec((tm, tn), lambda i, j, k: (i, j)),
        out_shape=jax.ShapeDtypeStruct((n, B * FNET_WIDTH), BF16),
        scratch_shapes=[pltpu.VMEM((tm, tn), F32)],
        compiler_params=_cparams(("arbitrary", "arbitrary", "arbitrary")),
        name="fnet_seq",
    )(a_mat, rr)


def _merge_kernel(oml_ref, omc_ref, ofl_ref, ofc_ref, onl_ref, onc_ref, gm_ref, gf_ref, gn_ref, x_ref, mod_ref,
                  wm_ref, wf_ref, wn_ref, wo_ref, bo_ref, lg_ref, lb_ref, wr_ref, br_ref, xo_ref, h_ref, lo_ref,
                  *, alpha, nct):
    def gate(g_ref):
        return 1.0 / (1.0 + jnp.exp(-g_ref[...].astype(F32)))

    is_ctx = pl.program_id(0) < nct
    nkb, rows = ofl_ref.shape[1], ofl_ref.shape[2]
    ofl = jnp.concatenate(
        [jnp.concatenate([ofl_ref[g, kb, k1 * FFT_NB:(k1 + 1) * FFT_NB, :]
                          for k1 in range(rows // FFT_NB) for kb in range(nkb)], axis=0)
         for g in range(FNET_GROUPS)], axis=1)
    of = jnp.where(is_ctx, ofc_ref[...].astype(F32), ofl).astype(BF16)
    om = jnp.where(is_ctx, omc_ref[...], oml_ref[...])
    on = jnp.where(is_ctx, onc_ref[...], onl_ref[...])
    m = gate(gm_ref) * jnp.dot(om, wm_ref[...], preferred_element_type=F32)
    m = m + gate(gf_ref) * jnp.dot(of, wf_ref[...], preferred_element_type=F32)
    m = m + gate(gn_ref) * jnp.dot(on, wn_ref[...], preferred_element_type=F32)
    y = jnp.dot(m.astype(BF16), wo_ref[...], preferred_element_type=F32) + bo_ref[...]
    z = alpha * x_ref[...] + mod_ref[2:3, :] * y
    xn = _ln(z) * lg_ref[...] + lb_ref[...]
    xo_ref[...] = xn
    h = (_ln(xn) * (1.0 + mod_ref[4:5, :]) + mod_ref[3:4, :]).astype(BF16)
    h_ref[0] = h
    h_ref[1] = h
    lo_ref[...] = jnp.dot(h, wr_ref[...], preferred_element_type=F32) + br_ref[...]


def _merge(om_lat, om_ctx, of_lat, of_ctx, on_lat, on_ctx, u, xall, mod, mod_row, wm, wf, wn, wo, bo, lg, lb, wr, br,
           alpha, B, S, NC):
    TA, D = xall.shape
    TC = B * NC
    nct = TC // TM
    npb = S // TM
    const = lambda i: (0, 0)
    row = lambda i: (i, 0)
    gcol = U_G // D

    lat = lambda i: jnp.maximum(i - nct, 0)
    lat_row = lambda i: (lat(i), 0)
    ctx_row = lambda i: (jnp.minimum(i, nct - 1), 0)
    outs = pl.pallas_call(
        functools.partial(_merge_kernel, alpha=alpha, nct=nct),
        grid=(TA // TM,),
        in_specs=[pl.BlockSpec((TM, MLA_HEADS * MLA_V), lat_row),
                  pl.BlockSpec((TM, MLA_HEADS * MLA_V), ctx_row),
                  pl.BlockSpec((None, FNET_GROUPS, FFT_N2 // FFT_NB, (TM // FFT_N2) * FFT_NB, FNET_GROUP_DIM),
                               lambda i: (lat(i) // npb, 0, 0, lat(i) % npb, 0)),
                  pl.BlockSpec((TM, FNET_WIDTH), ctx_row),
                  pl.BlockSpec((TM, NAT_WIDTH), lat_row),
                  pl.BlockSpec((TM, NAT_WIDTH), ctx_row),
                  pl.BlockSpec((TM, D), lambda i: (i, gcol)),
                  pl.BlockSpec((TM, D), lambda i: (i, gcol + 1)),
                  pl.BlockSpec((TM, D), lambda i: (i, gcol + 2)),
                  pl.BlockSpec((TM, D), row),
                  pl.BlockSpec((None, 8, D), lambda i: (mod_row(i), 0, 0)),
                  pl.BlockSpec((MLA_HEADS * MLA_V, D), const),
                  pl.BlockSpec((FNET_WIDTH, D), const),
                  pl.BlockSpec((NAT_WIDTH, D), const),
                  pl.BlockSpec((D, D), const),
                  pl.BlockSpec((1, D), const),
                  pl.BlockSpec((1, D), const),
                  pl.BlockSpec((1, D), const),
                  pl.BlockSpec((D, LANES), const),
                  pl.BlockSpec((1, LANES), const)],
        out_specs=[pl.BlockSpec((TM, D), row), pl.BlockSpec((2, TM, D), lambda i: (0, i, 0)),
                   pl.BlockSpec((TM, LANES), row)],
        out_shape=[jax.ShapeDtypeStruct((TA, D), F32), jax.ShapeDtypeStruct((2, TA, D), BF16),
                   jax.ShapeDtypeStruct((TA, LANES), F32)],
        compiler_params=_cparams(("arbitrary",)),
        name="merge",
    )(om_lat, om_ctx, of_lat, of_ctx, on_lat, on_ctx, u, u, u, xall, mod, wm, wf, wn, wo, bo, lg, lb, wr, br)
    return outs


def _router_kernel(lg_ref, e_ref, w_ref, r_ref, cnt_ref, carry_ref):
    lg = lg_ref[...]
    lane = lax.broadcasted_iota(jnp.int32, lg.shape, 1)
    big = jnp.int32(1 << 20)
    is_g = lane < N_GROUPS
    gl = jnp.where(is_g, lg, -jnp.inf)
    ge = jnp.exp(gl - jnp.max(gl, axis=1, keepdims=True))
    p = ge / jnp.sum(ge, axis=1, keepdims=True)
    p_top = jnp.max(p, axis=1, keepdims=True)
    g_idx = jnp.min(jnp.where(is_g & (p == p_top), lane, big), axis=1, keepdims=True)
    lo = N_GROUPS + g_idx * EXPERTS_PER_GROUP
    in_grp = (lane >= lo) & (lane < lo + EXPERTS_PER_GROUP)
    el = jnp.where(in_grp, lg, -jnp.inf)
    m1 = jnp.max(el, axis=1, keepdims=True)
    i1 = jnp.min(jnp.where(in_grp & (el == m1), lane, big), axis=1, keepdims=True)
    rest = in_grp & (lane != i1)
    el2 = jnp.where(rest, lg, -jnp.inf)
    m2 = jnp.max(el2, axis=1, keepdims=True)
    i2 = jnp.min(jnp.where(rest & (el2 == m2), lane, big), axis=1, keepdims=True)
    d = jnp.exp(m2 - m1)
    w1 = p_top * (1.0 / (1.0 + d))
    w2 = p_top * (d / (1.0 + d))
    e_ref[...] = jnp.where(lane == 0, i1 - N_GROUPS, jnp.where(lane == 1, i2 - N_GROUPS, 0))
    w_ref[...] = jnp.where(lane == 0, w1, jnp.where(lane == 1, w2, 0.0))

    @pl.when(pl.program_id(0) == 0)
    def _():
        carry_ref[...] = jnp.zeros(carry_ref.shape, F32)

    tm = lg.shape[0]
    oh1 = lane == i1
    oh2 = lane == i2
    oh = jnp.where(oh1, 1.0, jnp.where(oh2, 1.0, 0.0))
    rr = lax.broadcasted_iota(jnp.int32, (tm, tm), 0)
    cc = lax.broadcasted_iota(jnp.int32, (tm, tm), 1)
    lower = jnp.where(cc < rr, 1.0, 0.0).astype(BF16)
    before = jnp.dot(lower, oh.astype(BF16), preferred_element_type=F32) + carry_ref[0:1, :]
    r1 = jnp.sum(jnp.where(oh1, before, 0.0), axis=1, keepdims=True)
    r2 = jnp.sum(jnp.where(oh2, before, 0.0), axis=1, keepdims=True)
    r_ref[...] = jnp.where(lane == 0, r1, jnp.where(lane == 1, r2, 0.0)).astype(jnp.int32)
    carry_ref[...] = carry_ref[...] + jnp.sum(oh, axis=0, keepdims=True)
    cnt_ref[...] = carry_ref[...].astype(jnp.int32)


def _router(logits):
    TA = logits.shape[0]
    row = lambda i: (i, 0)
    return pl.pallas_call(
        _router_kernel,
        grid=(TA // TM,),
        in_specs=[pl.BlockSpec((TM, LANES), row)],
        out_specs=[pl.BlockSpec((TM, LANES), row)] * 3 + [pl.BlockSpec((8, LANES), lambda i: (0, 0))],
        out_shape=[jax.ShapeDtypeStruct((TA, LANES), jnp.int32), jax.ShapeDtypeStruct((TA, LANES), F32),
                   jax.ShapeDtypeStruct((TA, LANES), jnp.int32), jax.ShapeDtypeStruct((8, LANES), jnp.int32)],
        scratch_shapes=[pltpu.VMEM((8, LANES), F32)],
        compiler_params=_cparams(("arbitrary",)),
        name="router",
    )(logits)


def _moe_kernel(be_ref, nu_ref, x_ref, wg_ref, wu_ref, wd_ref, o_ref, wgb_ref, wub_ref, wdb_ref):
    i = pl.program_id(0)
    changed = jnp.logical_or(i == 0, be_ref[i] != be_ref[jnp.maximum(i - 1, 0)])

    @pl.when(changed)
    def _():
        wgb_ref[...] = wg_ref[...].astype(BF16)
        wub_ref[...] = wu_ref[...].astype(BF16)
        wdb_ref[...] = wd_ref[...].astype(BF16)

    @pl.when(i < nu_ref[0])
    def _():
        x = x_ref[...]
        a = jnp.dot(x, wgb_ref[...], preferred_element_type=F32)
        b = jnp.dot(x, wub_ref[...], preferred_element_type=F32)
        hmid = (a * (1.0 / (1.0 + jnp.exp(-a))) * b).astype(BF16)
        o_ref[...] = jnp.dot(hmid, wdb_ref[...], preferred_element_type=F32).astype(o_ref.dtype)

    @pl.when(i >= nu_ref[0])
    def _():
        o_ref[...] = jnp.zeros(o_ref.shape, o_ref.dtype)


def _moe_experts(xb, block_e, n_used, w_gate, w_up, w_down, layer):
    n_rows, D = xb.shape
    nb = n_rows // MOE_BM
    DE = w_gate.shape[-1]
    xrow = lambda i, be, nu: (jnp.minimum(i, nu[0] - 1), 0)
    wmap = lambda i, be, nu: (layer, be[i], 0, 0)
    grid_spec = pltpu.PrefetchScalarGridSpec(
        num_scalar_prefetch=2,
        grid=(nb,),
        in_specs=[pl.BlockSpec((MOE_BM, D), xrow),
                  pl.BlockSpec((None, None, D, DE), wmap),
                  pl.BlockSpec((None, None, D, DE), wmap),
                  pl.BlockSpec((None, None, DE, D), wmap)],
        out_specs=pl.BlockSpec((MOE_BM, D), lambda i, be, nu: (i, 0)),
        scratch_shapes=[pltpu.VMEM((D, DE), BF16), pltpu.VMEM((D, DE), BF16), pltpu.VMEM((DE, D), BF16)],
    )
    return pl.pallas_call(
        _moe_kernel,
        grid_spec=grid_spec,
        out_shape=jax.ShapeDtypeStruct((n_rows, D), BF16),
        compiler_params=_cparams(("arbitrary",)),
        name="moe_experts",
    )(block_e, n_used, xb, w_gate, w_up, w_down)


def _ffn_combine(x_ref, y0_ref, y1_ref, w_ref, mod_ref, lg_ref, lb_ref, alpha):
    w = w_ref[...]
    f = y0_ref[...].astype(F32) * w[:, 0:1] + y1_ref[...].astype(F32) * w[:, 1:2]
    z = alpha * x_ref[...] + mod_ref[5:6, :] * f
    return _ln(z) * lg_ref[...] + lb_ref[...]


def _ffn_out_kernel(x_ref, y0_ref, y1_ref, w_ref, mod_ref, lg_ref, lb_ref, o_ref, *, alpha):
    o_ref[...] = _ffn_combine(x_ref, y0_ref, y1_ref, w_ref, mod_ref, lg_ref, lb_ref, alpha)


def _ffn_out(xall, y2, w_tok, mod, mod_row, lg, lb, alpha, first_tile=0):
    TA, D = xall.shape
    nt = TA // TM
    row = lambda i: (i + first_tile, 0)
    const = lambda i: (0, 0)
    return pl.pallas_call(
        functools.partial(_ffn_out_kernel, alpha=alpha),
        grid=(nt - first_tile,),
        in_specs=[pl.BlockSpec((TM, D), row), pl.BlockSpec((TM, D), row),
                  pl.BlockSpec((TM, D), lambda i: (nt + i + first_tile, 0)),
                  pl.BlockSpec((TM, LANES), row),
                  pl.BlockSpec((None, 8, D), lambda i: (mod_row(i + first_tile), 0, 0)),
                  pl.BlockSpec((1, D), const), pl.BlockSpec((1, D), const)],
        out_specs=pl.BlockSpec((TM, D), lambda i: (i, 0)),
        out_shape=jax.ShapeDtypeStruct((TA - first_tile * TM, D), F32),
        compiler_params=_cparams(("arbitrary",)),
        name="ffn_out",
    )(xall, y2, y2, w_tok, mod, lg, lb)


def _reorder_w_in(w_in, b_in):
    D = w_in.shape[0]
    o_cq, o_ckv, o_kr = 0, MLA_Q_LORA, MLA_Q_LORA + MLA_KV_LORA
    o_f = o_kr + MLA_ROPE
    o_nq = o_f + FNET_WIDTH
    o_g = o_nq + 3 * NAT_WIDTH
    rot_src, rot_sign = _rope_partner()

    def build(a):
        kr = a[..., o_kr:o_kr + MLA_ROPE]
        kr_rot = kr[..., rot_src] * rot_sign
        pad = jnp.zeros(a.shape[:-1] + (LANES - 2 * MLA_ROPE,), a.dtype)
        return jnp.concatenate([a[..., o_f:o_g], a[..., o_g:], a[..., o_cq:o_kr], kr, kr_rot, pad], axis=-1)

    return build(w_in).astype(BF16), build(b_in[None, :])


def _rope_partner():
    q = MLA_ROPE // 4
    src = np.zeros(MLA_ROPE, np.int32)
    sign = np.zeros(MLA_ROPE, np.float32)
    for half in range(2):
        o = half * 2 * q
        for i in range(q):
            src[o + i], sign[o + i] = o + q + i, -1.0
            src[o + q + i], sign[o + q + i] = o + i, 1.0
    return src, sign


def _pad_heads(w, width, offset=0):
    K = w.shape[0]
    w = w.reshape(K, MLA_HEADS, width)
    out = jnp.zeros((K, MLA_HEADS, HEAD_PAD), w.dtype)
    return out.at[:, :, offset:offset + width].set(w).reshape(K, MLA_HEADS * HEAD_PAD)


def _mla_weights(w_uq, w_uk, w_uv):
    rot_src, rot_sign = _rope_partner()
    dq = MLA_NOPE + MLA_ROPE
    wq = w_uq.reshape(-1, MLA_HEADS, dq)
    wq_pe_rot = wq[:, :, MLA_NOPE:][:, :, rot_src] * rot_sign
    wqa = _pad_heads(wq.reshape(-1, MLA_HEADS * dq), dq)
    wqb = _pad_heads(wq_pe_rot.reshape(-1, MLA_HEADS * MLA_ROPE), MLA_ROPE, MLA_NOPE)
    wk = _pad_heads(w_uk, MLA_NOPE)
    K = w_uv.shape[0]
    wv3 = w_uv.reshape(K, MLA_HEADS // 2, 2, MLA_V)
    wv = jnp.zeros((K, MLA_HEADS // 2, 2, HEAD_PAD), w_uv.dtype)
    wv = wv.at[:, :, 0, :MLA_V].set(wv3[:, :, 0]).at[:, :, 1, MLA_V:].set(wv3[:, :, 1])
    wv = wv.reshape(K, MLA_HEADS * HEAD_PAD)
    vone = np.zeros((MLA_HEADS // 2, 2, HEAD_PAD), np.float32)
    vone[:, 0, MLA_V] = 1.0
    vone[:, 1, 0] = 1.0
    pa = np.zeros((LANES, MLA_HEADS, HEAD_PAD), np.float32)
    pb = np.zeros((LANES, MLA_HEADS, HEAD_PAD), np.float32)
    for i in range(MLA_ROPE):
        pa[i, :, MLA_NOPE + i] = 1.0
        pb[MLA_ROPE + i, :, MLA_NOPE + i] = 1.0
    W = MLA_HEADS * HEAD_PAD
    return (wqa.astype(BF16), wqb.astype(BF16), wk.astype(BF16), wv.astype(BF16),
            jnp.asarray(pa.reshape(LANES, W), BF16), jnp.asarray(pb.reshape(LANES, W), BF16),
            jnp.asarray(vone.reshape(1, W)))


def _rope_lane_tables(S):
    t = jnp.arange(S)
    rows = (t // GRID_W).astype(F32)
    cols = (t % GRID_W).astype(F32)
    n_freq = MLA_ROPE // 4
    inv = jnp.power(ROPE_BASE, -jnp.arange(n_freq, dtype=F32) / n_freq)
    ar = rows[:, None] * inv[None, :]
    ac = cols[:, None] * inv[None, :]
    ones = jnp.ones((S, MLA_NOPE), F32)
    zpad = jnp.zeros((S, HEAD_PAD - MLA_NOPE - MLA_ROPE), F32)
    cos = jnp.concatenate([ones, jnp.cos(ar), jnp.cos(ar), jnp.cos(ac), jnp.cos(ac), zpad], axis=1)
    sin = jnp.concatenate([0 * ones, jnp.sin(ar), jnp.sin(ar), jnp.sin(ac), jnp.sin(ac), zpad], axis=1)
    idc = jnp.concatenate([jnp.ones((TM, MLA_NOPE + MLA_ROPE), F32), jnp.zeros((TM, HEAD_PAD - MLA_NOPE - MLA_ROPE), F32)], axis=1)
    return jnp.concatenate([idc, cos], axis=0), jnp.concatenate([jnp.zeros((TM, HEAD_PAD), F32), sin], axis=0)


def _moe_plan(e_ids, ranks, counts, n_blocks):
    t = e_ids.shape[0]
    a = 2 * t
    padded = (counts + MOE_BM - 1) // MOE_BM * MOE_BM
    pends = jnp.cumsum(padded)
    pstarts = pends - padded
    eo = jnp.arange(N_EXPERTS, dtype=jnp.int32)
    pos = ranks + jnp.sum(jnp.where(e_ids[:, :, None] == eo, pstarts, 0), axis=-1)
    blk0 = jnp.arange(n_blocks, dtype=jnp.int32) * MOE_BM
    block_e = jnp.minimum(jnp.sum((pends[None, :] <= blk0[:, None]).astype(jnp.int32), axis=1), N_EXPERTS - 1)
    n_used = (pends[-1] // MOE_BM).astype(jnp.int32).reshape(1)
    nd = n_blocks * MOE_BM - a
    dcum = jnp.cumsum(padded - counts)
    d_e = jnp.sum((dcum[None, :] <= jnp.arange(nd, dtype=jnp.int32)[:, None]).astype(jnp.int32), axis=1)
    keys = jnp.concatenate([e_ids.reshape(a) * 2, d_e * 2 + 1])
    payload = jnp.concatenate([jnp.arange(a, dtype=jnp.int32), (jnp.arange(nd, dtype=jnp.int32) * 2) % a])
    shift = max(a - 1, 1).bit_length()
    assert (2 * N_EXPERTS + 2) << shift < 2 ** 31
    row_asg = jnp.sort(keys * (1 << shift) + payload) & ((1 << shift) - 1)
    row_src = (row_asg % 2) * t + row_asg // 2
    return row_src, pos, block_e, n_used


def kernel(x, c, ctx, c_ctx, w_ada, b_ada, w_in, b_in, q_norm_g, kv_norm_g, w_uq, w_uk, w_uv, rpb, w_o_mla, w_o_fnet, w_o_nat, w_out, b_out, ln1_g, ln1_b, w_rg, b_rg, w_re, b_re, w_gate_e, w_up_e, w_down_e, ln2_g, ln2_b):
    B, S, D = x.shape
    NC = ctx.shape[1]
    L = w_ada.shape[0]
    TC, TL = B * NC, B * S
    TA = TC + TL
    assert TC % TM == 0 and S % TM == 0 and NC % 128 == 0 and B < 8
    alpha = float((2 * L) ** 0.25)
    nct, npb = TC // TM, S // TM

    def mod_row(i):
        return jnp.where(i < nct, B, jnp.maximum(i - nct, 0) // npb)

    def tab_row(i):
        return jnp.where(i < nct, 0, 1 + jnp.maximum(i - nct, 0) % npb)

    cc = jnp.concatenate([c, c_ctx[None, :], jnp.zeros((8 - B - 1, D), F32)], axis=0)
    mod_all = _ada_all(cc, w_ada, b_ada).reshape(L, 8, 6, D)
    mod_all = jnp.concatenate([mod_all, jnp.zeros((L, 8, 2, D), F32)], axis=2)

    tabc, tabs = _rope_lane_tables(S)
    chan_scale = FNET_GROUP_DIM ** -0.5
    cch, sch = _dft_tables(FNET_GROUP_DIM, chan_scale)
    w_chan = jnp.concatenate([cch, sch], axis=1).astype(BF16)
    assert S % (FFT_N2 * FFT_NB) == 0 and TM % (S // FFT_N2) == 0 and TM % FFT_N2 == 0
    m1, w2 = _fft_tables(S)
    a_ctx = _seq_dft_matrix(NC)
    nat_bias = _nat_bias_table(rpb, S // GRID_W)
    qscale = float((MLA_NOPE + MLA_ROPE) ** -0.5) * LOG2E
    n_blocks = (2 * TA) // MOE_BM + N_EXPERTS
    tq = math.gcd(1024, math.gcd(S, TC))
    tk = math.gcd(2048, S)

    xall = jnp.concatenate([ctx.reshape(TC, D), x.reshape(TL, D)], axis=0)
    pending_ffn = None
    for l in range(L):
        mod = mod_all[l]
        w_in_r, b_in_r = _reorder_w_in(w_in[l], b_in[l])
        if pending_ffn is None:
            u, f_lat = _in_proj(xall, mod, w_in_r, b_in_r, mod_row, B, S, nct)
        else:
            xall, u, f_lat = _in_proj(xall, mod, w_in_r, b_in_r, mod_row, B, S, nct, ffn=pending_ffn)

        wqa, wqb, wk, wv, pa, pb, vone = _mla_weights(w_uq[l], w_uk[l], w_uv[l])
        q, k, v = _qkv(u, tabc, tabs, tab_row, q_norm_g[l][None, :], kv_norm_g[l][None, :],
                       wqa, wqb, wk, wv, pa, pb, vone, qscale)
        om_ctx, om_lat = _mla_attention(q, k, v, B, S, NC, tq, tk)

        on_ctx, on_lat = _nat_attention(u, nat_bias, l, B, S, NC)

        of_lat = _fourier_mix_long(f_lat, m1, w2, w_chan, B, S)
        of_ctx = _fourier_mix(u, 0, B, NC, a_ctx, w_chan, ts=NC, tm=NC, tk=2 * NC)
        of_ctx = of_ctx.reshape(NC, B, FNET_WIDTH).transpose(1, 0, 2).reshape(TC, FNET_WIDTH)

        w_r = jnp.concatenate([w_rg[l], w_re[l], jnp.zeros((D, LANES - N_GROUPS - N_EXPERTS), F32)], axis=1).astype(BF16)
        b_r = jnp.concatenate([b_rg[l], b_re[l], jnp.zeros((LANES - N_GROUPS - N_EXPERTS,), F32)])[None, :]
        xall, h_f, logits = _merge(om_lat, om_ctx, of_lat, of_ctx, on_lat, on_ctx, u, xall, mod, mod_row,
                                   w_o_mla[l].astype(BF16), w_o_fnet[l].astype(BF16), w_o_nat[l].astype(BF16),
                                   w_out[l].astype(BF16), b_out[l][None, :], ln1_g[l][None, :], ln1_b[l][None, :],
                                   w_r, b_r, alpha, B, S, NC)

        e_out, w_tok, r_out, cnt = _router(logits)
        counts = cnt[0, N_GROUPS:N_GROUPS + N_EXPERTS]
        row_src, pos, block_e, n_used = _moe_plan(e_out[:, :2], r_out[:, :2], counts, n_blocks)
        xb = jnp.take(h_f.reshape(2 * TA, D), row_src, axis=0, mode="clip")
        yb = _moe_experts(xb, block_e, n_used, w_gate_e, w_up_e, w_down_e, l)
        y2 = jnp.take(yb, jnp.concatenate([pos[:, 0], pos[:, 1]]), axis=0, mode="clip")
        pending_ffn = (y2, w_tok, mod, ln2_g[l][None, :], ln2_b[l][None, :], alpha)
    y2, w_tok, mod, lg, lb, _ = pending_ffn
    out = _ffn_out(xall, y2, w_tok, mod, mod_row, lg, lb, alpha, first_tile=nct)
    return out.reshape(B, S, D)
```

```python
import functools
import math

import numpy as np
import jax
import jax.numpy as jnp
from jax import lax
from jax.experimental import pallas as pl
from jax.experimental.pallas import tpu as pltpu

F32 = jnp.float32
BF16 = jnp.bfloat16

GRID_W = 64
MLA_HEADS = 8
MLA_Q_LORA = 256
MLA_KV_LORA = 128
MLA_NOPE = 64
MLA_ROPE = 32
MLA_V = 64
ROPE_BASE = 10000.0
FNET_GROUPS = 4
FNET_GROUP_DIM = 128
FNET_WIDTH = FNET_GROUPS * FNET_GROUP_DIM
NAT_HEADS = 8
NAT_HEAD_DIM = 64
NAT_WIDTH = NAT_HEADS * NAT_HEAD_DIM
NAT_WIN_H = 8
NAT_WIN_W = 16
N_GROUPS = 4
EXPERTS_PER_GROUP = 8
N_EXPERTS = N_GROUPS * EXPERTS_PER_GROUP
LN_EPS = 1e-6

LANES = 128
HEAD_PAD = 128
LOG2E = 1.4426950408889634
NEG_BIG = -1e30
VMEM_LIMIT = 56 * 1024 * 1024

U_F, U_NQ, U_NK, U_NV, U_G, U_CQ, U_CKV, U_KR = 0, 512, 1024, 1536, 2048, 5120, 5376, 5504

TM = 512
NAT_ROWS = 4
NAT_KROWS = 12
MOE_BM = 512


def _cparams(sem):
    return pltpu.CompilerParams(dimension_semantics=sem, vmem_limit_bytes=VMEM_LIMIT)


def _ln(x):
    mu = jnp.mean(x, axis=-1, keepdims=True)
    xc = x - mu
    var = jnp.mean(xc * xc, axis=-1, keepdims=True)
    return xc * lax.rsqrt(var + LN_EPS)


def _ada_kernel(c_ref, w_ref, b_ref, o_ref):
    c = c_ref[...]
    a = c * (1.0 / (1.0 + jnp.exp(-c)))
    o_ref[...] = jnp.dot(a, w_ref[...], preferred_element_type=F32,
                         precision=lax.Precision.HIGHEST) + b_ref[...]


def _ada_all(cc, w_ada, b_ada):
    L, D, N = w_ada.shape
    tn = 1536
    return pl.pallas_call(
        _ada_kernel,
        grid=(L, N // tn),
        in_specs=[pl.BlockSpec((8, D), lambda l, j: (0, 0)),
                  pl.BlockSpec((None, D, tn), lambda l, j: (l, 0, j)),
                  pl.BlockSpec((None, 1, tn), lambda l, j: (l, 0, j))],
        out_specs=pl.BlockSpec((None, 8, tn), lambda l, j: (l, 0, j)),
        out_shape=jax.ShapeDtypeStruct((L, 8, N), F32),
        compiler_params=_cparams(("arbitrary", "arbitrary")),
        name="ada_mod",
    )(cc, w_ada, b_ada.reshape(L, 1, N))


def _in_proj_kernel(x_ref, mod_ref, w_ref, b_ref, o_ref, f_ref, *, cw):
    _in_proj_body(x_ref[...], mod_ref, w_ref, b_ref, o_ref, f_ref, cw)


def _ffn_in_proj_kernel(x_ref, y0_ref, y1_ref, wt_ref, modp_ref, lg_ref, lb_ref, mod_ref, w_ref, b_ref,
                        xo_ref, o_ref, f_ref, *, alpha, cw):
    xn = _ffn_combine(x_ref, y0_ref, y1_ref, wt_ref, modp_ref, lg_ref, lb_ref, alpha)
    xo_ref[...] = xn
    _in_proj_body(xn, mod_ref, w_ref, b_ref, o_ref, f_ref, cw)


def _in_proj_body(x, mod_ref, w_ref, b_ref, o_ref, f_ref, cw):
    y = _ln(x)
    h = (y * (1.0 + mod_ref[1:2, :]) + mod_ref[0:1, :]).astype(BF16)
    for j in range(o_ref.shape[1] // cw):
        sl = slice(j * cw, (j + 1) * cw)
        acc = jnp.dot(h, w_ref[:, sl], preferred_element_type=F32) + b_ref[:, sl]
        o_ref[:, sl] = acc.astype(o_ref.dtype)
        if j == U_F // cw:
            nbk = f_ref.shape[1]
            n1 = nbk * FFT_NB
            for g in range(FNET_GROUPS):
                for jb in range(nbk):
                    for jj in range(acc.shape[0] // n1):
                        src = jj * n1 + jb * FFT_NB
                        f_ref[g, jb, jj * FFT_NB:(jj + 1) * FFT_NB, :] = (
                            acc[src:src + FFT_NB, g * FNET_GROUP_DIM:(g + 1) * FNET_GROUP_DIM])


def _in_proj(xall, mod, w_in_r, b_in_r, mod_row, B, S, nct, ffn=None):
    TA, D = xall.shape
    N = w_in_r.shape[1]
    n1 = S // FFT_N2
    npb = S // TM
    nt = TA // TM
    lat = lambda i: jnp.maximum(i - nct, 0)
    row = lambda i: (i, 0)
    const = lambda i: (0, 0)
    mod_spec = pl.BlockSpec((None, 8, D), lambda i: (mod_row(i), 0, 0))
    f_spec = pl.BlockSpec((None, FNET_GROUPS, n1 // FFT_NB, (TM // n1) * FFT_NB, FNET_GROUP_DIM),
                          lambda i: (lat(i) // npb, 0, 0, lat(i) % npb, 0))
    proj_specs = [mod_spec, pl.BlockSpec((D, N), const), pl.BlockSpec((1, N), const)]
    outs = [jax.ShapeDtypeStruct((TA, N), BF16),
            jax.ShapeDtypeStruct((B, FNET_GROUPS, n1 // FFT_NB, FFT_N2 * FFT_NB, FNET_GROUP_DIM), F32)]
    out_specs = [pl.BlockSpec((TM, N), row), f_spec]
    if ffn is None:
        return pl.pallas_call(
            functools.partial(_in_proj_kernel, cw=512),
            grid=(nt,),
            in_specs=[pl.BlockSpec((TM, D), row)] + proj_specs,
            out_specs=out_specs,
            out_shape=outs,
            compiler_params=_cparams(("arbitrary",)),
            name="in_proj",
        )(xall, mod, w_in_r, b_in_r)
    y2, w_tok, mod_prev, lg, lb, alpha = ffn
    xn, u, f_lat = pl.pallas_call(
        functools.partial(_ffn_in_proj_kernel, alpha=alpha, cw=512),
        grid=(nt,),
        in_specs=[pl.BlockSpec((TM, D), row), pl.BlockSpec((TM, D), row), pl.BlockSpec((TM, D), lambda i: (nt + i, 0)),
                  pl.BlockSpec((TM, LANES), row), mod_spec, pl.BlockSpec((1, D), const), pl.BlockSpec((1, D), const)]
                 + proj_specs,
        out_specs=[pl.BlockSpec((TM, D), row)] + out_specs,
        out_shape=[jax.ShapeDtypeStruct((TA, D), F32)] + outs,
        compiler_params=_cparams(("arbitrary",)),
        name="ffn_in_proj",
    )(xall, y2, y2, w_tok, mod_prev, lg, lb, mod, w_in_r, b_in_r)
    return xn, u, f_lat


def _rms(x, g):
    return x * lax.rsqrt(jnp.mean(x * x, axis=-1, keepdims=True) + LN_EPS) * g


def _qkv_kernel(cq_ref, ckv_ref, kr_ref, tc_ref, ts_ref, qg_ref, kvg_ref, wqa_ref, wqb_ref,
                wk_ref, wv_ref, pa_ref, pb_ref, vone_ref, q_ref, k_ref, v_ref, *, qscale):
    cqn = _rms(cq_ref[...].astype(F32), qg_ref[...]).astype(BF16)
    ckvn = _rms(ckv_ref[...].astype(F32), kvg_ref[...]).astype(BF16)
    cos = jnp.concatenate([tc_ref[...]] * MLA_HEADS, axis=1)
    sin = jnp.concatenate([ts_ref[...]] * MLA_HEADS, axis=1)
    qa = jnp.dot(cqn, wqa_ref[...], preferred_element_type=F32)
    qb = jnp.dot(cqn, wqb_ref[...], preferred_element_type=F32)
    q_ref[...] = ((qa * cos + qb * sin) * qscale).astype(q_ref.dtype)
    kr = kr_ref[...]
    ka = jnp.dot(kr, pa_ref[...], preferred_element_type=F32)
    kb = jnp.dot(kr, pb_ref[...], preferred_element_type=F32)
    kn = jnp.dot(ckvn, wk_ref[...], preferred_element_type=F32)
    k_ref[...] = (kn + ka * cos + kb * sin).astype(k_ref.dtype)
    v = jnp.dot(ckvn, wv_ref[...], preferred_element_type=F32) + vone_ref[...]
    v_ref[...] = v.astype(v_ref.dtype)


def _qkv(u, tabc, tabs, tab_row, qg, kvg, wqa, wqb, wk, wv, pa, pb, vone, qscale):
    TA = u.shape[0]
    W = MLA_HEADS * HEAD_PAD
    const = lambda i: (0, 0)
    out = jax.ShapeDtypeStruct((TA, W), BF16)
    return pl.pallas_call(
        functools.partial(_qkv_kernel, qscale=qscale),
        grid=(TA // TM,),
        in_specs=[pl.BlockSpec((TM, MLA_Q_LORA), lambda i: (i, U_CQ // MLA_Q_LORA)),
                  pl.BlockSpec((TM, MLA_KV_LORA), lambda i: (i, U_CKV // MLA_KV_LORA)),
                  pl.BlockSpec((TM, LANES), lambda i: (i, U_KR // LANES)),
                  pl.BlockSpec((TM, LANES), lambda i: (tab_row(i), 0)),
                  pl.BlockSpec((TM, LANES), lambda i: (tab_row(i), 0)),
                  pl.BlockSpec((1, MLA_Q_LORA), const),
                  pl.BlockSpec((1, MLA_KV_LORA), const),
                  pl.BlockSpec((MLA_Q_LORA, W), const),
                  pl.BlockSpec((MLA_Q_LORA, W), const),
                  pl.BlockSpec((MLA_KV_LORA, W), const),
                  pl.BlockSpec((MLA_KV_LORA, W), const),
                  pl.BlockSpec((LANES, W), const),
                  pl.BlockSpec((LANES, W), const),
                  pl.BlockSpec((1, W), const)],
        out_specs=[pl.BlockSpec((TM, W), lambda i: (i, 0))] * 3,
        out_shape=[out, out, out],
        compiler_params=_cparams(("arbitrary",)),
        name="mla_qkv",
    )(u, u, u, tabc, tabs, qg, kvg, wqa, wqb, wk, wv, pa, pb, vone)


def _mla_attend(q_ref, k_ref, v_ref, acc_ref, m_ref):
    tk = k_ref.shape[0]
    for h in range(MLA_HEADS):
        hs = slice(h * HEAD_PAD, (h + 1) * HEAD_PAD)
        s = lax.dot_general(q_ref[:, hs], k_ref[:, hs], (((1,), (1,)), ((), ())),
                            preferred_element_type=F32)
        m_prev = m_ref[h]
        m_new = jnp.maximum(m_prev, jnp.max(s, axis=1, keepdims=True))
        alpha = jnp.exp2(m_prev - m_new)
        p = jnp.exp2((s - jnp.concatenate([m_new] * (tk // LANES), axis=1)).astype(BF16))
        pv = jnp.dot(p, v_ref[:, hs], preferred_element_type=F32)
        acc_ref[h] = acc_ref[h] * alpha + pv
        m_ref[h] = m_new


def _mla_finish(o_ref, acc_ref):
    lane = lax.broadcasted_iota(jnp.int32, acc_ref.shape[1:], 1)
    for hp in range(MLA_HEADS // 2):
        a0 = acc_ref[2 * hp]
        a1 = acc_ref[2 * hp + 1]
        l0 = jnp.sum(jnp.where(lane == MLA_V, a0, 0.0), axis=1, keepdims=True)
        l1 = jnp.sum(jnp.where(lane == 0, a1, 0.0), axis=1, keepdims=True)
        o = jnp.where(lane < MLA_V, a0 / l0, a1 / l1)
        o_ref[:, hp * LANES:(hp + 1) * LANES] = o.astype(o_ref.dtype)


def _mla_lat_kernel(q_ref, kc_ref, vc_ref, kl_ref, vl_ref, o_ref, acc_ref, m_ref):
    j = pl.program_id(2)

    @pl.when(j == 0)
    def _():
        m_ref[...] = jnp.full(m_ref.shape, NEG_BIG, F32)
        acc_ref[...] = jnp.zeros(acc_ref.shape, F32)
        _mla_attend(q_ref, kc_ref, vc_ref, acc_ref, m_ref)

    _mla_attend(q_ref, kl_ref, vl_ref, acc_ref, m_ref)

    @pl.when(j == pl.num_programs(2) - 1)
    def _():
        _mla_finish(o_ref, acc_ref)


def _mla_ctx_kernel(q_ref, kc_ref, vc_ref, o_ref, acc_ref, m_ref):
    m_ref[...] = jnp.full(m_ref.shape, NEG_BIG, F32)
    acc_ref[...] = jnp.zeros(acc_ref.shape, F32)
    _mla_attend(q_ref, kc_ref, vc_ref, acc_ref, m_ref)
    _mla_finish(o_ref, acc_ref)


def _mla_attention(q, k, v, B, S, NC, tq, tk):
    TA, W = q.shape
    TC = B * NC
    OW = MLA_HEADS * MLA_V
    nq, nk = S // tq, S // tk
    scratch = lambda t: [pltpu.VMEM((MLA_HEADS, t, HEAD_PAD), F32), pltpu.VMEM((MLA_HEADS, t, HEAD_PAD), F32)]
    kv_align = math.gcd(TC, tk)
    kv_spec = pl.BlockSpec((pl.Element(tk), pl.Element(W)),
                           lambda b, i, j: (pl.multiple_of(TC + b * S + j * tk, kv_align), 0))
    o_lat = pl.pallas_call(
        _mla_lat_kernel,
        grid=(B, nq, nk),
        in_specs=[pl.BlockSpec((tq, W), lambda b, i, j: (TC // tq + b * nq + i, 0)),
                  pl.BlockSpec((NC, W), lambda b, i, j: (b, 0)),
                  pl.BlockSpec((NC, W), lambda b, i, j: (b, 0)),
                  kv_spec, kv_spec],
        out_specs=pl.BlockSpec((tq, OW), lambda b, i, j: (b * nq + i, 0)),
        out_shape=jax.ShapeDtypeStruct((B * S, OW), BF16),
        scratch_shapes=scratch(tq),
        compiler_params=_cparams(("arbitrary", "arbitrary", "arbitrary")),
        name="mla_attn_lat",
    )(q, k, v, k, v)
    o_ctx = pl.pallas_call(
        _mla_ctx_kernel,
        grid=(B,),
        in_specs=[pl.BlockSpec((NC, W), lambda b: (b, 0))] * 3,
        out_specs=pl.BlockSpec((NC, OW), lambda b: (b, 0)),
        out_shape=jax.ShapeDtypeStruct((TC, OW), BF16),
        scratch_shapes=scratch(NC),
        compiler_params=_cparams(("arbitrary",)),
        name="mla_attn_ctx",
    )(q, k, v)
    return o_ctx, o_lat


def _nat_softmax_pv(scores, values):
    m = scores[0].max(axis=1, keepdims=True)
    for s in scores[1:]:
        m = jnp.maximum(m, s.max(axis=1, keepdims=True))
    l = None
    o = None
    for s, v in zip(scores, values):
        p = jnp.exp2(s - m)
        ls = jnp.sum(p, axis=1, keepdims=True)
        pv = jnp.dot(p.astype(BF16), v, preferred_element_type=F32)
        l = ls if l is None else l + ls
        o = pv if o is None else o + pv
    return o / l


def _nat_lat_kernel(q_ref, k_ref, v_ref, kc_ref, vc_ref, bias_ref, o_ref, *, qscale):
    tq = q_ref.shape[0]
    ck = tq
    nck = k_ref.shape[0] // ck
    lane = lax.broadcasted_iota(jnp.int32, (tq, LANES), 1)
    dn = (((1,), (1,)), ((), ()))
    for hp in range(NAT_HEADS // 2):
        ls = slice(hp * LANES, (hp + 1) * LANES)
        qp = q_ref[:, ls].astype(F32) * qscale
        keys = [k_ref[c * ck:(c + 1) * ck, ls] for c in range(nck)] + [kc_ref[:, ls]]
        vals = [v_ref[c * ck:(c + 1) * ck, ls] for c in range(nck)] + [vc_ref[:, ls]]
        outs = []
        for hh in range(2):
            mask = (lane < NAT_HEAD_DIM) if hh == 0 else (lane >= NAT_HEAD_DIM)
            qm = jnp.where(mask, qp, 0.0).astype(BF16)
            scores = []
            for c, kk in enumerate(keys):
                s = lax.dot_general(qm, kk, dn, preferred_element_type=F32)
                if c < nck:
                    s = s + bias_ref[2 * hp + hh, :, c * ck:(c + 1) * ck].astype(F32)
                scores.append(s)
            outs.append(_nat_softmax_pv(scores, vals))
        o_ref[:, ls] = jnp.where(lane < NAT_HEAD_DIM, outs[0], outs[1]).astype(o_ref.dtype)


def _nat_ctx_kernel(q_ref, kc_ref, vc_ref, o_ref, *, qscale):
    tq = q_ref.shape[0]
    lane = lax.broadcasted_iota(jnp.int32, (tq, LANES), 1)
    dn = (((1,), (1,)), ((), ()))
    for hp in range(NAT_HEADS // 2):
        ls = slice(hp * LANES, (hp + 1) * LANES)
        qp = q_ref[:, ls].astype(F32) * qscale
        outs = []
        for hh in range(2):
            mask = (lane < NAT_HEAD_DIM) if hh == 0 else (lane >= NAT_HEAD_DIM)
            qm = jnp.where(mask, qp, 0.0).astype(BF16)
            s = lax.dot_general(qm, kc_ref[:, ls], dn, preferred_element_type=F32)
            outs.append(_nat_softmax_pv([s], [vc_ref[:, ls]]))
        o_ref[:, ls] = jnp.where(lane < NAT_HEAD_DIM, outs[0], outs[1]).astype(o_ref.dtype)


def _nat_bias_table(rpb, rows_n):
    L = rpb.shape[0]
    qc = np.arange(GRID_W)
    cs = np.clip(qc - NAT_WIN_W // 2, 0, GRID_W - NAT_WIN_W)
    kc = np.arange(GRID_W)
    colvalid = (kc[None, :] >= cs[:, None]) & (kc[None, :] < cs[:, None] + NAT_WIN_W)
    dc = np.clip(kc[None, :] - qc[:, None] + NAT_WIN_W - 1, 0, 2 * NAT_WIN_W - 2)
    bt = jnp.where(colvalid, rpb[:, :, :, dc] * LOG2E, NEG_BIG)
    n_dr = 2 * NAT_WIN_H - 1
    bt = jnp.concatenate([bt, jnp.full((L, NAT_HEADS, 1, GRID_W, GRID_W), NEG_BIG, F32)], axis=2).astype(BF16)
    kh = NAT_WIN_H
    dr = np.full((3, NAT_ROWS, NAT_KROWS), n_dr, np.int32)
    for v, r0 in enumerate((0, 2 * NAT_ROWS, rows_n - NAT_ROWS)):
        kstart = int(np.clip(r0 - kh // 2, 0, rows_n - NAT_KROWS))
        for i in range(NAT_ROWS):
            r = r0 + i
            rs = int(np.clip(r - kh // 2, 0, rows_n - kh))
            for jj in range(NAT_KROWS):
                kr = kstart + jj
                if rs <= kr < rs + kh:
                    dr[v, i, jj] = kr - r + NAT_WIN_H - 1
    t = bt[:, :, dr]
    t = t.transpose(0, 2, 1, 3, 5, 4, 6)
    return t.reshape(L, 3, NAT_HEADS, NAT_ROWS * GRID_W, NAT_KROWS * GRID_W)


def _nat_attention(u, bias, layer, B, S, NC):
    TA = u.shape[0]
    TC = B * NC
    rows_n = S // GRID_W
    nblk = rows_n // NAT_ROWS
    tq = NAT_ROWS * GRID_W
    ck = tq
    qscale = float(NAT_HEAD_DIM ** -0.5) * LOG2E
    cq, ckk, cv = U_NQ // NAT_WIDTH, U_NK // NAT_WIDTH, U_NV // NAT_WIDTH
    base = lambda b: TC // tq + b * nblk

    def kstart_blk(i):
        return jnp.clip(i * NAT_ROWS - NAT_WIN_H // 2, 0, rows_n - NAT_KROWS) // NAT_ROWS

    def kv_spec(col):
        return pl.BlockSpec((pl.Element(NAT_KROWS * GRID_W), pl.Element(NAT_WIDTH)),
                            lambda b, i: (pl.multiple_of((base(b) + kstart_blk(i)) * ck, ck), col * NAT_WIDTH))

    def variant(i):
        return jnp.where(i == 0, 0, jnp.where(i == nblk - 1, 2, 1))

    o_lat = pl.pallas_call(
        functools.partial(_nat_lat_kernel, qscale=qscale),
        grid=(B, nblk),
        in_specs=[pl.BlockSpec((tq, NAT_WIDTH), lambda b, i: (base(b) + i, cq)),
                  kv_spec(ckk), kv_spec(cv),
                  pl.BlockSpec((NC, NAT_WIDTH), lambda b, i: (b, ckk)),
                  pl.BlockSpec((NC, NAT_WIDTH), lambda b, i: (b, cv)),
                  pl.BlockSpec((None, None, NAT_HEADS, tq, NAT_KROWS * GRID_W),
                               lambda b, i: (layer, variant(i), 0, 0, 0))],
        out_specs=pl.BlockSpec((tq, NAT_WIDTH), lambda b, i: (b * nblk + i, 0)),
        out_shape=jax.ShapeDtypeStruct((B * S, NAT_WIDTH), BF16),
        compiler_params=_cparams(("arbitrary", "arbitrary")),
        name="nat_attn_lat",
    )(u, u, u, u, u, bias)
    o_ctx = pl.pallas_call(
        functools.partial(_nat_ctx_kernel, qscale=qscale),
        grid=(B,),
        in_specs=[pl.BlockSpec((NC, NAT_WIDTH), lambda b: (b, cq)),
                  pl.BlockSpec((NC, NAT_WIDTH), lambda b: (b, ckk)),
                  pl.BlockSpec((NC, NAT_WIDTH), lambda b: (b, cv))],
        out_specs=pl.BlockSpec((NC, NAT_WIDTH), lambda b: (b, 0)),
        out_shape=jax.ShapeDtypeStruct((TC, NAT_WIDTH), BF16),
        compiler_params=_cparams(("arbitrary",)),
        name="nat_attn_ctx",
    )(u, u, u)
    return o_ctx, o_lat


def _fnet_chan_kernel(f_ref, w_ref, o_ref):
    for g in range(FNET_GROUPS):
        gs = slice(g * FNET_GROUP_DIM, (g + 1) * FNET_GROUP_DIM)
        o_ref[:, gs] = jnp.dot(f_ref[:, gs], w_ref[...], preferred_element_type=F32).astype(o_ref.dtype)


def _fnet_seq_kernel(a_ref, r_ref, o_ref, acc_ref):
    k = pl.program_id(2)

    @pl.when(k == 0)
    def _():
        acc_ref[...] = jnp.zeros(acc_ref.shape, F32)

    acc_ref[...] += jnp.dot(a_ref[...], r_ref[...], preferred_element_type=F32)

    @pl.when(k == pl.num_programs(2) - 1)
    def _():
        o_ref[...] = acc_ref[...].astype(o_ref.dtype)


def _dft_tables(n, scale):
    j = jnp.arange(n, dtype=jnp.int32)
    idx = (j[:, None] * j[None, :]) & (n - 1)
    ang = idx.astype(F32) * (2.0 * math.pi / n)
    return jnp.cos(ang) * scale, jnp.sin(ang) * scale


def _seq_dft_matrix(n):
    c, s = _dft_tables(n, n ** -0.5)
    return jnp.concatenate([c, -s], axis=1).astype(BF16)


FFT_N2 = 128
FFT_NB = 16


def _fft_tables(n):
    n2 = FFT_N2
    n1 = n // n2
    j1 = jnp.arange(n1, dtype=jnp.int32)[:, None, None]
    k2 = jnp.arange(n2, dtype=jnp.int32)[None, :, None]
    j2 = jnp.arange(n2, dtype=jnp.int32)[None, None, :]
    ang = ((k2 * (j1 + n1 * j2)) & (n - 1)).astype(F32) * (2.0 * math.pi / n)
    s1 = n2 ** -0.5
    m1 = jnp.concatenate([jnp.cos(ang) * s1, -jnp.sin(ang) * s1], axis=1).astype(BF16)
    c1, sn1 = _dft_tables(n1, n1 ** -0.5)
    top = jnp.stack([c1, sn1], axis=2).reshape(n1, 2 * n1)
    bot = jnp.stack([-sn1, c1], axis=2).reshape(n1, 2 * n1)
    w2 = jnp.concatenate([top, bot], axis=0).astype(BF16)
    return m1, w2


def _load_slab(ref, i):
    rows = ref.shape[1] // FFT_NB
    parts = [ref[g, pl.ds(i, rows, stride=FFT_NB), :] for g in range(FNET_GROUPS)]
    return jnp.concatenate(parts, axis=1)


def _fft1_kernel(f_ref, m_ref, o_ref):
    n2 = m_ref.shape[2]
    for i in range(FFT_NB):
        r = jnp.dot(m_ref[i], _load_slab(f_ref, i).astype(BF16), preferred_element_type=F32)
        for g in range(FNET_GROUPS):
            gs = slice(g * FNET_GROUP_DIM, (g + 1) * FNET_GROUP_DIM)
            for p in range(2):
                for kb in range(n2 // FFT_NB):
                    src = p * n2 + kb * FFT_NB
                    dst = (2 * i + p) * FFT_NB
                    o_ref[g, kb, dst:dst + FFT_NB, :] = r[src:src + FFT_NB, gs]


def _fft2_kernel(b_ref, w_ref, cc_ref, cs_ref, o_ref):
    n1 = o_ref.shape[1] // FFT_NB
    for i in range(FFT_NB):
        x = jnp.dot(w_ref[...], _load_slab(b_ref, i).astype(BF16), preferred_element_type=F32).astype(BF16)
        for g in range(FNET_GROUPS):
            gs = slice(g * FNET_GROUP_DIM, (g + 1) * FNET_GROUP_DIM)
            y = (jnp.dot(x[:n1, gs], cc_ref[...], preferred_element_type=F32)
                 + jnp.dot(x[n1:, gs], cs_ref[...], preferred_element_type=F32))
            o_ref[g, pl.ds(i, n1, stride=FFT_NB), :] = y


def _fourier_mix_long(f_lat, m1, w2, w_chan, B, n):
    n2 = FFT_N2
    n1 = n // n2
    G, GD, NB = FNET_GROUPS, FNET_GROUP_DIM, FFT_NB
    bmid = pl.pallas_call(
        _fft1_kernel,
        grid=(B, n1 // NB),
        in_specs=[pl.BlockSpec((None, G, None, n2 * NB, GD), lambda b, j: (b, 0, j, 0, 0)),
                  pl.BlockSpec((NB, 2 * n2, n2), lambda b, j: (j, 0, 0))],
        out_specs=pl.BlockSpec((None, G, n2 // NB, 2 * NB * NB, GD), lambda b, j: (b, 0, 0, j, 0)),
        out_shape=jax.ShapeDtypeStruct((B, G, n2 // NB, 2 * n1 * NB, GD), F32),
        compiler_params=_cparams(("arbitrary", "arbitrary")),
        name="fnet_fft1",
    )(f_lat, m1)
    return pl.pallas_call(
        _fft2_kernel,
        grid=(B, n2 // NB),
        in_specs=[pl.BlockSpec((None, G, None, 2 * n1 * NB, GD), lambda b, j: (b, 0, j, 0, 0)),
                  pl.BlockSpec((2 * n1, 2 * n1), lambda b, j: (0, 0)),
                  pl.BlockSpec((GD, GD), lambda b, j: (0, 0)),
                  pl.BlockSpec((GD, GD), lambda b, j: (0, 1))],
        out_specs=pl.BlockSpec((None, G, None, n1 * NB, GD), lambda b, j: (b, 0, j, 0, 0)),
        out_shape=jax.ShapeDtypeStruct((B, G, n2 // NB, n1 * NB, GD), F32),
        compiler_params=_cparams(("arbitrary", "arbitrary")),
        name="fnet_fft2",
    )(bmid, w2, w_chan, w_chan)


def _fourier_mix(u, row0, B, n, a_mat, w_chan, ts, tm, tk):
    nt = n // ts
    rr = pl.pallas_call(
        _fnet_chan_kernel,
        grid=(B, nt, 2),
        in_specs=[pl.BlockSpec((ts, FNET_WIDTH), lambda b, i, p: (row0 // ts + b * nt + i, U_F // FNET_WIDTH)),
                  pl.BlockSpec((FNET_GROUP_DIM, FNET_GROUP_DIM), lambda b, i, p: (0, p))],
        out_specs=pl.BlockSpec((ts, FNET_WIDTH), lambda b, i, p: (p * nt + i, b)),
        out_shape=jax.ShapeDtypeStruct((2 * n, B * FNET_WIDTH), BF16),
        compiler_params=_cparams(("arbitrary", "arbitrary", "arbitrary")),
        name="fnet_chan",
    )(u, w_chan)
    tn = min(B * FNET_WIDTH, 2048)
    return pl.pallas_call(
        _fnet_seq_kernel,
        grid=(n // tm, (B * FNET_WIDTH) // tn, (2 * n) // tk),
        in_specs=[pl.BlockSpec((tm, tk), lambda i, j, k: (i, k)),
                  pl.BlockSpec((tk, tn), lambda i, j, k: (k, j))],
        out_specs=pl.BlockSpec((tm, tn), lambda i, j, k: (i, j)),
        out_shape=jax.ShapeDtypeStruct((n, B * FNET_WIDTH), BF16),
        scratch_shapes=[pltpu.VMEM((tm, tn), F32)],
        compiler_params=_cparams(("arbitrary", "arbitrary", "arbitrary")),
        name="fnet_seq",
    )(a_mat, rr)


def _merge_kernel(oml_ref, omc_ref, ofl_ref, ofc_ref, onl_ref, onc_ref, gm_ref, gf_ref, gn_ref, x_ref, mod_ref,
                  wm_ref, wf_ref, wn_ref, wo_ref, bo_ref, lg_ref, lb_ref, wr_ref, br_ref, xo_ref, h_ref,
                  e_ref, w_ref, r_ref, cnt_ref, carry_ref, *, alpha, nct):
    def gate(g_ref):
        return 1.0 / (1.0 + jnp.exp(-g_ref[...].astype(F32)))

    is_ctx = pl.program_id(0) < nct
    nkb, rows = ofl_ref.shape[1], ofl_ref.shape[2]
    ofl = jnp.concatenate(
        [jnp.concatenate([ofl_ref[g, kb, k1 * FFT_NB:(k1 + 1) * FFT_NB, :]
                          for k1 in range(rows // FFT_NB) for kb in range(nkb)], axis=0)
         for g in range(FNET_GROUPS)], axis=1)
    of = jnp.where(is_ctx, ofc_ref[...].astype(F32), ofl).astype(BF16)
    om = jnp.where(is_ctx, omc_ref[...], oml_ref[...])
    on = jnp.where(is_ctx, onc_ref[...], onl_ref[...])
    m = gate(gm_ref) * jnp.dot(om, wm_ref[...], preferred_element_type=F32)
    m = m + gate(gf_ref) * jnp.dot(of, wf_ref[...], preferred_element_type=F32)
    m = m + gate(gn_ref) * jnp.dot(on, wn_ref[...], preferred_element_type=F32)
    y = jnp.dot(m.astype(BF16), wo_ref[...], preferred_element_type=F32) + bo_ref[...]
    z = alpha * x_ref[...] + mod_ref[2:3, :] * y
    xn = _ln(z) * lg_ref[...] + lb_ref[...]
    xo_ref[...] = xn
    h = (_ln(xn) * (1.0 + mod_ref[4:5, :]) + mod_ref[3:4, :]).astype(BF16)
    h_ref[0] = h
    h_ref[1] = h
    logits = jnp.dot(h, wr_ref[...], preferred_element_type=F32) + br_ref[...]
    _route(logits, e_ref, w_ref, r_ref, cnt_ref, carry_ref)


def _merge(om_lat, om_ctx, of_lat, of_ctx, on_lat, on_ctx, u, xall, mod, mod_row, wm, wf, wn, wo, bo, lg, lb, wr, br,
           alpha, B, S, NC):
    TA, D = xall.shape
    TC = B * NC
    nct = TC // TM
    npb = S // TM
    const = lambda i: (0, 0)
    row = lambda i: (i, 0)
    gcol = U_G // D

    lat = lambda i: jnp.maximum(i - nct, 0)
    lat_row = lambda i: (lat(i), 0)
    ctx_row = lambda i: (jnp.minimum(i, nct - 1), 0)
    outs = pl.pallas_call(
        functools.partial(_merge_kernel, alpha=alpha, nct=nct),
        grid=(TA // TM,),
        in_specs=[pl.BlockSpec((TM, MLA_HEADS * MLA_V), lat_row),
                  pl.BlockSpec((TM, MLA_HEADS * MLA_V), ctx_row),
                  pl.BlockSpec((None, FNET_GROUPS, FFT_N2 // FFT_NB, (TM // FFT_N2) * FFT_NB, FNET_GROUP_DIM),
                               lambda i: (lat(i) // npb, 0, 0, lat(i) % npb, 0)),
                  pl.BlockSpec((TM, FNET_WIDTH), ctx_row),
                  pl.BlockSpec((TM, NAT_WIDTH), lat_row),
                  pl.BlockSpec((TM, NAT_WIDTH), ctx_row),
                  pl.BlockSpec((TM, D), lambda i: (i, gcol)),
                  pl.BlockSpec((TM, D), lambda i: (i, gcol + 1)),
                  pl.BlockSpec((TM, D), lambda i: (i, gcol + 2)),
                  pl.BlockSpec((TM, D), row),
                  pl.BlockSpec((None, 8, D), lambda i: (mod_row(i), 0, 0)),
                  pl.BlockSpec((MLA_HEADS * MLA_V, D), const),
                  pl.BlockSpec((FNET_WIDTH, D), const),
                  pl.BlockSpec((NAT_WIDTH, D), const),
                  pl.BlockSpec((D, D), const),
                  pl.BlockSpec((1, D), const),
                  pl.BlockSpec((1, D), const),
                  pl.BlockSpec((1, D), const),
                  pl.BlockSpec((D, LANES), const),
                  pl.BlockSpec((1, LANES), const)],
        out_specs=[pl.BlockSpec((TM, D), row), pl.BlockSpec((2, TM, D), lambda i: (0, i, 0)),
                   pl.BlockSpec((TM, LANES), row), pl.BlockSpec((TM, LANES), row), pl.BlockSpec((TM, LANES), row),
                   pl.BlockSpec((8, LANES), const)],
        out_shape=[jax.ShapeDtypeStruct((TA, D), F32), jax.ShapeDtypeStruct((2, TA, D), BF16),
                   jax.ShapeDtypeStruct((TA, LANES), jnp.int32), jax.ShapeDtypeStruct((TA, LANES), F32),
                   jax.ShapeDtypeStruct((TA, LANES), jnp.int32), jax.ShapeDtypeStruct((8, LANES), jnp.int32)],
        scratch_shapes=[pltpu.VMEM((8, LANES), F32)],
        compiler_params=_cparams(("arbitrary",)),
        name="merge",
    )(om_lat, om_ctx, of_lat, of_ctx, on_lat, on_ctx, u, u, u, xall, mod, wm, wf, wn, wo, bo, lg, lb, wr, br)
    return outs


def _route(lg, e_ref, w_ref, r_ref, cnt_ref, carry_ref):
    lane = lax.broadcasted_iota(jnp.int32, lg.shape, 1)
    big = jnp.int32(1 << 20)
    is_g = lane < N_GROUPS
    gl = jnp.where(is_g, lg, -jnp.inf)
    ge = jnp.exp(gl - jnp.max(gl, axis=1, keepdims=True))
    p = ge / jnp.sum(ge, axis=1, keepdims=True)
    p_top = jnp.max(p, axis=1, keepdims=True)
    g_idx = jnp.min(jnp.where(is_g & (p == p_top), lane, big), axis=1, keepdims=True)
    lo = N_GROUPS + g_idx * EXPERTS_PER_GROUP
    in_grp = (lane >= lo) & (lane < lo + EXPERTS_PER_GROUP)
    el = jnp.where(in_grp, lg, -jnp.inf)
    m1 = jnp.max(el, axis=1, keepdims=True)
    i1 = jnp.min(jnp.where(in_grp & (el == m1), lane, big), axis=1, keepdims=True)
    rest = in_grp & (lane != i1)
    el2 = jnp.where(rest, lg, -jnp.inf)
    m2 = jnp.max(el2, axis=1, keepdims=True)
    i2 = jnp.min(jnp.where(rest & (el2 == m2), lane, big), axis=1, keepdims=True)
    d = jnp.exp(m2 - m1)
    w1 = p_top * (1.0 / (1.0 + d))
    w2 = p_top * (d / (1.0 + d))
    e_ref[...] = jnp.where(lane == 0, i1 - N_GROUPS, jnp.where(lane == 1, i2 - N_GROUPS, 0))
    w_ref[...] = jnp.where(lane == 0, w1, jnp.where(lane == 1, w2, 0.0))

    @pl.when(pl.program_id(0) == 0)
    def _():
        carry_ref[...] = jnp.zeros(carry_ref.shape, F32)

    tm = lg.shape[0]
    oh1 = lane == i1
    oh2 = lane == i2
    oh = jnp.where(oh1, 1.0, jnp.where(oh2, 1.0, 0.0))
    rr = lax.broadcasted_iota(jnp.int32, (tm, tm), 0)
    cc = lax.broadcasted_iota(jnp.int32, (tm, tm), 1)
    lower = jnp.where(cc < rr, 1.0, 0.0).astype(BF16)
    before = jnp.dot(lower, oh.astype(BF16), preferred_element_type=F32) + carry_ref[0:1, :]
    r1 = jnp.sum(jnp.where(oh1, before, 0.0), axis=1, keepdims=True)
    r2 = jnp.sum(jnp.where(oh2, before, 0.0), axis=1, keepdims=True)
    r_ref[...] = jnp.where(lane == 0, r1, jnp.where(lane == 1, r2, 0.0)).astype(jnp.int32)
    carry_ref[...] = carry_ref[...] + jnp.sum(oh, axis=0, keepdims=True)
    cnt_ref[...] = carry_ref[...].astype(jnp.int32)


def _moe_kernel(be_ref, nu_ref, x_ref, wg_ref, wu_ref, wd_ref, o_ref, wgb_ref, wub_ref, wdb_ref):
    i = pl.program_id(0)
    changed = jnp.logical_or(i == 0, be_ref[i] != be_ref[jnp.maximum(i - 1, 0)])

    @pl.when(changed)
    def _():
        wgb_ref[...] = wg_ref[...].astype(BF16)
        wub_ref[...] = wu_ref[...].astype(BF16)
        wdb_ref[...] = wd_ref[...].astype(BF16)

    @pl.when(i < nu_ref[0])
    def _():
        x = x_ref[...]
        a = jnp.dot(x, wgb_ref[...], preferred_element_type=F32)
        b = jnp.dot(x, wub_ref[...], preferred_element_type=F32)
        hmid = (a * (1.0 / (1.0 + jnp.exp(-a))) * b).astype(BF16)
        o_ref[...] = jnp.dot(hmid, wdb_ref[...], preferred_element_type=F32).astype(o_ref.dtype)

    @pl.when(i >= nu_ref[0])
    def _():
        o_ref[...] = jnp.zeros(o_ref.shape, o_ref.dtype)


def _moe_experts(xb, block_e, n_used, w_gate, w_up, w_down, layer):
    n_rows, D = xb.shape
    nb = n_rows // MOE_BM
    DE = w_gate.shape[-1]
    xrow = lambda i, be, nu: (jnp.minimum(i, nu[0] - 1), 0)
    wmap = lambda i, be, nu: (layer, be[i], 0, 0)
    grid_spec = pltpu.PrefetchScalarGridSpec(
        num_scalar_prefetch=2,
        grid=(nb,),
        in_specs=[pl.BlockSpec((MOE_BM, D), xrow),
                  pl.BlockSpec((None, None, D, DE), wmap),
                  pl.BlockSpec((None, None, D, DE), wmap),
                  pl.BlockSpec((None, None, DE, D), wmap)],
        out_specs=pl.BlockSpec((MOE_BM, D), lambda i, be, nu: (i, 0)),
        scratch_shapes=[pltpu.VMEM((D, DE), BF16), pltpu.VMEM((D, DE), BF16), pltpu.VMEM((DE, D), BF16)],
    )
    return pl.pallas_call(
        _moe_kernel,
        grid_spec=grid_spec,
        out_shape=jax.ShapeDtypeStruct((n_rows, D), BF16),
        compiler_params=_cparams(("arbitrary",)),
        name="moe_experts",
    )(block_e, n_used, xb, w_gate, w_up, w_down)


def _ffn_combine(x_ref, y0_ref, y1_ref, w_ref, mod_ref, lg_ref, lb_ref, alpha):
    w = w_ref[...]
    f = y0_ref[...].astype(F32) * w[:, 0:1] + y1_ref[...].astype(F32) * w[:, 1:2]
    z = alpha * x_ref[...] + mod_ref[5:6, :] * f
    return _ln(z) * lg_ref[...] + lb_ref[...]


def _ffn_out_kernel(x_ref, y0_ref, y1_ref, w_ref, mod_ref, lg_ref, lb_ref, o_ref, *, alpha):
    o_ref[...] = _ffn_combine(x_ref, y0_ref, y1_ref, w_ref, mod_ref, lg_ref, lb_ref, alpha)


def _ffn_out(xall, y2, w_tok, mod, mod_row, lg, lb, alpha, first_tile=0):
    TA, D = xall.shape
    nt = TA // TM
    row = lambda i: (i + first_tile, 0)
    const = lambda i: (0, 0)
    return pl.pallas_call(
        functools.partial(_ffn_out_kernel, alpha=alpha),
        grid=(nt - first_tile,),
        in_specs=[pl.BlockSpec((TM, D), row), pl.BlockSpec((TM, D), row),
                  pl.BlockSpec((TM, D), lambda i: (nt + i + first_tile, 0)),
                  pl.BlockSpec((TM, LANES), row),
                  pl.BlockSpec((None, 8, D), lambda i: (mod_row(i + first_tile), 0, 0)),
                  pl.BlockSpec((1, D), const), pl.BlockSpec((1, D), const)],
        out_specs=pl.BlockSpec((TM, D), lambda i: (i, 0)),
        out_shape=jax.ShapeDtypeStruct((TA - first_tile * TM, D), F32),
        compiler_params=_cparams(("arbitrary",)),
        name="ffn_out",
    )(xall, y2, y2, w_tok, mod, lg, lb)


def _reorder_w_in(w_in, b_in):
    D = w_in.shape[0]
    o_cq, o_ckv, o_kr = 0, MLA_Q_LORA, MLA_Q_LORA + MLA_KV_LORA
    o_f = o_kr + MLA_ROPE
    o_nq = o_f + FNET_WIDTH
    o_g = o_nq + 3 * NAT_WIDTH
    rot_src, rot_sign = _rope_partner()

    def build(a):
        kr = a[..., o_kr:o_kr + MLA_ROPE]
        kr_rot = kr[..., rot_src] * rot_sign
        pad = jnp.zeros(a.shape[:-1] + (LANES - 2 * MLA_ROPE,), a.dtype)
        return jnp.concatenate([a[..., o_f:o_g], a[..., o_g:], a[..., o_cq:o_kr], kr, kr_rot, pad], axis=-1)

    return build(w_in).astype(BF16), build(b_in[None, :])


def _rope_partner():
    q = MLA_ROPE // 4
    src = np.zeros(MLA_ROPE, np.int32)
    sign = np.zeros(MLA_ROPE, np.float32)
    for half in range(2):
        o = half * 2 * q
        for i in range(q):
            src[o + i], sign[o + i] = o + q + i, -1.0
            src[o + q + i], sign[o + q + i] = o + i, 1.0
    return src, sign


def _pad_heads(w, width, offset=0):
    K = w.shape[0]
    w = w.reshape(K, MLA_HEADS, width)
    out = jnp.zeros((K, MLA_HEADS, HEAD_PAD), w.dtype)
    return out.at[:, :, offset:offset + width].set(w).reshape(K, MLA_HEADS * HEAD_PAD)


def _mla_weights(w_uq, w_uk, w_uv):
    rot_src, rot_sign = _rope_partner()
    dq = MLA_NOPE + MLA_ROPE
    wq = w_uq.reshape(-1, MLA_HEADS, dq)
    wq_pe_rot = wq[:, :, MLA_NOPE:][:, :, rot_src] * rot_sign
    wqa = _pad_heads(wq.reshape(-1, MLA_HEADS * dq), dq)
    wqb = _pad_heads(wq_pe_rot.reshape(-1, MLA_HEADS * MLA_ROPE), MLA_ROPE, MLA_NOPE)
    wk = _pad_heads(w_uk, MLA_NOPE)
    K = w_uv.shape[0]
    wv3 = w_uv.reshape(K, MLA_HEADS // 2, 2, MLA_V)
    wv = jnp.zeros((K, MLA_HEADS // 2, 2, HEAD_PAD), w_uv.dtype)
    wv = wv.at[:, :, 0, :MLA_V].set(wv3[:, :, 0]).at[:, :, 1, MLA_V:].set(wv3[:, :, 1])
    wv = wv.reshape(K, MLA_HEADS * HEAD_PAD)
    vone = np.zeros((MLA_HEADS // 2, 2, HEAD_PAD), np.float32)
    vone[:, 0, MLA_V] = 1.0
    vone[:, 1, 0] = 1.0
    pa = np.zeros((LANES, MLA_HEADS, HEAD_PAD), np.float32)
    pb = np.zeros((LANES, MLA_HEADS, HEAD_PAD), np.float32)
    for i in range(MLA_ROPE):
        pa[i, :, MLA_NOPE + i] = 1.0
        pb[MLA_ROPE + i, :, MLA_NOPE + i] = 1.0
    W = MLA_HEADS * HEAD_PAD
    return (wqa.astype(BF16), wqb.astype(BF16), wk.astype(BF16), wv.astype(BF16),
            jnp.asarray(pa.reshape(LANES, W), BF16), jnp.asarray(pb.reshape(LANES, W), BF16),
            jnp.asarray(vone.reshape(1, W)))


def _rope_lane_tables(S):
    t = jnp.arange(S)
    rows = (t // GRID_W).astype(F32)
    cols = (t % GRID_W).astype(F32)
    n_freq = MLA_ROPE // 4
    inv = jnp.power(ROPE_BASE, -jnp.arange(n_freq, dtype=F32) / n_freq)
    ar = rows[:, None] * inv[None, :]
    ac = cols[:, None] * inv[None, :]
    ones = jnp.ones((S, MLA_NOPE), F32)
    zpad = jnp.zeros((S, HEAD_PAD - MLA_NOPE - MLA_ROPE), F32)
    cos = jnp.concatenate([ones, jnp.cos(ar), jnp.cos(ar), jnp.cos(ac), jnp.cos(ac), zpad], axis=1)
    sin = jnp.concatenate([0 * ones, jnp.sin(ar), jnp.sin(ar), jnp.sin(ac), jnp.sin(ac), zpad], axis=1)
    idc = jnp.concatenate([jnp.ones((TM, MLA_NOPE + MLA_ROPE), F32), jnp.zeros((TM, HEAD_PAD - MLA_NOPE - MLA_ROPE), F32)], axis=1)
    return jnp.concatenate([idc, cos], axis=0), jnp.concatenate([jnp.zeros((TM, HEAD_PAD), F32), sin], axis=0)


def _moe_plan(e_ids, ranks, counts, n_blocks):
    t = e_ids.shape[0]
    a = 2 * t
    padded = (counts + MOE_BM - 1) // MOE_BM * MOE_BM
    pends = jnp.cumsum(padded)
    pstarts = pends - padded
    eo = jnp.arange(N_EXPERTS, dtype=jnp.int32)
    pos = ranks + jnp.sum(jnp.where(e_ids[:, :, None] == eo, pstarts, 0), axis=-1)
    blk0 = jnp.arange(n_blocks, dtype=jnp.int32) * MOE_BM
    block_e = jnp.minimum(jnp.sum((pends[None, :] <= blk0[:, None]).astype(jnp.int32), axis=1), N_EXPERTS - 1)
    n_used = (pends[-1] // MOE_BM).astype(jnp.int32).reshape(1)
    nd = n_blocks * MOE_BM - a
    dcum = jnp.cumsum(padded - counts)
    d_e = jnp.sum((dcum[None, :] <= jnp.arange(nd, dtype=jnp.int32)[:, None]).astype(jnp.int32), axis=1)
    keys = jnp.concatenate([e_ids.reshape(a) * 2, d_e * 2 + 1])
    payload = jnp.concatenate([jnp.arange(a, dtype=jnp.int32), (jnp.arange(nd, dtype=jnp.int32) * 2) % a])
    shift = max(a - 1, 1).bit_length()
    assert (2 * N_EXPERTS + 2) << shift < 2 ** 31
    row_asg = jnp.sort(keys * (1 << shift) + payload) & ((1 << shift) - 1)
    row_src = (row_asg % 2) * t + row_asg // 2
    return row_src, pos, block_e, n_used


def kernel(x, c, ctx, c_ctx, w_ada, b_ada, w_in, b_in, q_norm_g, kv_norm_g, w_uq, w_uk, w_uv, rpb, w_o_mla, w_o_fnet, w_o_nat, w_out, b_out, ln1_g, ln1_b, w_rg, b_rg, w_re, b_re, w_gate_e, w_up_e, w_down_e, ln2_g, ln2_b):
    B, S, D = x.shape
    NC = ctx.shape[1]
    L = w_ada.shape[0]
    TC, TL = B * NC, B * S
    TA = TC + TL
    assert TC % TM == 0 and S % TM == 0 and NC % 128 == 0 and B < 8
    alpha = float((2 * L) ** 0.25)
    nct, npb = TC // TM, S // TM

    def mod_row(i):
        return jnp.where(i < nct, B, jnp.maximum(i - nct, 0) // npb)

    def tab_row(i):
        return jnp.where(i < nct, 0, 1 + jnp.maximum(i - nct, 0) % npb)

    cc = jnp.concatenate([c, c_ctx[None, :], jnp.zeros((8 - B - 1, D), F32)], axis=0)
    mod_all = _ada_all(cc, w_ada, b_ada).reshape(L, 8, 6, D)
    mod_all = jnp.concatenate([mod_all, jnp.zeros((L, 8, 2, D), F32)], axis=2)

    tabc, tabs = _rope_lane_tables(S)
    chan_scale = FNET_GROUP_DIM ** -0.5
    cch, sch = _dft_tables(FNET_GROUP_DIM, chan_scale)
    w_chan = jnp.concatenate([cch, sch], axis=1).astype(BF16)
    assert S % (FFT_N2 * FFT_NB) == 0 and TM % (S // FFT_N2) == 0 and TM % FFT_N2 == 0
    m1, w2 = _fft_tables(S)
    a_ctx = _seq_dft_matrix(NC)
    nat_bias = _nat_bias_table(rpb, S // GRID_W)
    qscale = float((MLA_NOPE + MLA_ROPE) ** -0.5) * LOG2E
    n_blocks = (2 * TA) // MOE_BM + N_EXPERTS
    tq = math.gcd(1024, math.gcd(S, TC))
    tk = math.gcd(2048, S)

    xall = jnp.concatenate([ctx.reshape(TC, D), x.reshape(TL, D)], axis=0)
    pending_ffn = None
    for l in range(L):
        mod = mod_all[l]
        w_in_r, b_in_r = _reorder_w_in(w_in[l], b_in[l])
        if pending_ffn is None:
            u, f_lat = _in_proj(xall, mod, w_in_r, b_in_r, mod_row, B, S, nct)
        else:
            xall, u, f_lat = _in_proj(xall, mod, w_in_r, b_in_r, mod_row, B, S, nct, ffn=pending_ffn)

        wqa, wqb, wk, wv, pa, pb, vone = _mla_weights(w_uq[l], w_uk[l], w_uv[l])
        q, k, v = _qkv(u, tabc, tabs, tab_row, q_norm_g[l][None, :], kv_norm_g[l][None, :],
                       wqa, wqb, wk, wv, pa, pb, vone, qscale)
        om_ctx, om_lat = _mla_attention(q, k, v, B, S, NC, tq, tk)

        on_ctx, on_lat = _nat_attention(u, nat_bias, l, B, S, NC)

        of_lat = _fourier_mix_long(f_lat, m1, w2, w_chan, B, S)
        of_ctx = _fourier_mix(u, 0, B, NC, a_ctx, w_chan, ts=NC, tm=NC, tk=2 * NC)
        of_ctx = of_ctx.reshape(NC, B, FNET_WIDTH).transpose(1, 0, 2).reshape(TC, FNET_WIDTH)

        w_r = jnp.concatenate([w_rg[l], w_re[l], jnp.zeros((D, LANES - N_GROUPS - N_EXPERTS), F32)], axis=1).astype(BF16)
        b_r = jnp.concatenate([b_rg[l], b_re[l], jnp.zeros((LANES - N_GROUPS - N_EXPERTS,), F32)])[None, :]
        xall, h_f, e_out, w_tok, r_out, cnt = _merge(om_lat, om_ctx, of_lat, of_ctx, on_lat, on_ctx, u, xall, mod, mod_row,
                                   w_o_mla[l].astype(BF16), w_o_fnet[l].astype(BF16), w_o_nat[l].astype(BF16),
                                   w_out[l].astype(BF16), b_out[l][None, :], ln1_g[l][None, :], ln1_b[l][None, :],
                                   w_r, b_r, alpha, B, S, NC)

        counts = cnt[0, N_GROUPS:N_GROUPS + N_EXPERTS]
        row_src, pos, block_e, n_used = _moe_plan(e_out[:, :2], r_out[:, :2], counts, n_blocks)
        xb = jnp.take(h_f.reshape(2 * TA, D), row_src, axis=0, mode="clip")
        yb = _moe_experts(xb, block_e, n_used, w_gate_e, w_up_e, w_down_e, l)
        y2 = jnp.take(yb, jnp.concatenate([pos[:, 0], pos[:, 1]]), axis=0, mode="clip")
        pending_ffn = (y2, w_tok, mod, ln2_g[l][None, :], ln2_b[l][None, :], alpha)
    y2, w_tok, mod, lg, lb, _ = pending_ffn
    out = _ffn_out(xall, y2, w_tok, mod, mod_row, lg, lb, alpha, first_tile=nct)
    return out.reshape(B, S, D)
```
